```python
import jax
import jax.numpy as jnp
from jax import lax
import numpy as np

D_MODEL = 1024
BATCH = 4
SEQ = 4096
DEPTH = 4

CTX_LEN = 256
GRID_W = 64
N_MOD = 6
ADA_SCALE = 0.5
LN_EPS = 1e-5
DN_ALPHA = (2 * DEPTH) ** 0.25
DN_BETA = (8 * DEPTH) ** -0.25

SSD_HEADS = 8
SSD_HEAD_DIM = 64
SSD_INNER = SSD_HEADS * SSD_HEAD_DIM
SSD_GROUPS = 2
SSD_STATE = 128
SSD_CONV = 3
SSD_CHUNK = 128
SSD_XBC = SSD_INNER + 2 * SSD_GROUPS * SSD_STATE

RWKV_HEADS = 4
RWKV_HEAD_DIM = 64
RWKV_WIDTH = RWKV_HEADS * RWKV_HEAD_DIM
RWKV_DECAY_LORA = 64
RWKV_ICLR_LORA = 64
RWKV_GATE_LORA = 128
RWKV_COLS = 3 * RWKV_WIDTH + 2 * RWKV_DECAY_LORA + 2 * RWKV_ICLR_LORA + RWKV_GATE_LORA
RWKV_LNX_EPS = 64e-5

GLA_HEADS = 4
GLA_KEY_DIM = 32
GLA_VAL_DIM = 64
GLA_K_WIDTH = GLA_HEADS * GLA_KEY_DIM
GLA_V_WIDTH = GLA_HEADS * GLA_VAL_DIM
GLA_GATE_LORA = 16
GLA_GATE_NORMALIZER = 16.0
GLA_CHUNK = 16

MIX_WIDTH = SSD_INNER + RWKV_WIDTH + GLA_V_WIDTH
IN_WIDTHS = (SSD_INNER, SSD_XBC, 2 * SSD_HEADS, RWKV_COLS, GLA_K_WIDTH, GLA_K_WIDTH, GLA_V_WIDTH, 2 * GLA_GATE_LORA, GLA_V_WIDTH)
N_IN = SSD_INNER + SSD_XBC + 2 * SSD_HEADS + RWKV_COLS + 2 * GLA_K_WIDTH + 2 * GLA_V_WIDTH + 2 * GLA_GATE_LORA

N_EXPERTS = 64
TOP_K = 8
N_GROUPS = 8
TOPK_GROUPS = 4
EXPERT_DIM = 256
SHARED_DIM = 256
ROUTED_SCALE = 2.5
MOE_BLOCK = 128

kernel_name = 'hybrid_ssd_rwkv7_gla_moe_diffusion'


def _split(z, widths):
    return jnp.split(z, [int(i) for i in np.cumsum(widths)[:-1]], axis=-1)


def _layernorm(x, w=None, b=None, eps=LN_EPS):
    xf = x.astype(jnp.float32)
    mu = jnp.mean(xf, -1, keepdims=True)
    var = jnp.mean(jnp.square(xf - mu), -1, keepdims=True)
    y = (xf - mu) * lax.rsqrt(var + eps)
    if w is not None:
        y = y * w.astype(jnp.float32) + b.astype(jnp.float32)
    return y.astype(x.dtype)


def _rmsnorm_groups(x, w, group, eps=1e-5):
    xf = x.astype(jnp.float32)
    shp = xf.shape
    g = xf.reshape(shp[:-1] + (shp[-1] // group, group))
    g = g * lax.rsqrt(jnp.mean(g * g, -1, keepdims=True) + eps)
    return (g.reshape(shp) * w.astype(jnp.float32)).astype(x.dtype)


def _seg_flip(z, n_ctx):
    return jnp.concatenate([jnp.flip(z[:, :n_ctx], 1), jnp.flip(z[:, n_ctx:], 1)], axis=1)


def _dwconv_centred(z, w, b):
    ch = z.shape[-1]
    pad = (SSD_CONV - 1) // 2
    y = lax.conv_general_dilated(z, w[:, None, :].astype(z.dtype), window_strides=(1,), padding=[(pad, pad)],
                                 dimension_numbers=('NWC', 'WIO', 'NWC'), feature_group_count=ch)
    return y + b


def _seq_shift_mix(z, mu_prev, mu_next):
    prev = jnp.pad(z, ((0, 0), (1, 0), (0, 0)))[:, :-1]
    nxt = jnp.pad(z, ((0, 0), (0, 1), (0, 0)))[:, 1:]
    return z + mu_prev * (prev - z) + mu_next * (nxt - z)


def _grid_shift_mix(z, mu):
    b, s, ch = z.shape
    rows = s // GRID_W
    g = z.reshape(b, rows, GRID_W, ch)
    left = jnp.pad(g, ((0, 0), (0, 0), (1, 0), (0, 0)))[:, :, :-1]
    right = jnp.pad(g, ((0, 0), (0, 0), (0, 1), (0, 0)))[:, :, 1:]
    up = jnp.pad(g, ((0, 0), (1, 0), (0, 0), (0, 0)))[:, :-1]
    down = jnp.pad(g, ((0, 0), (0, 1), (0, 0), (0, 0)))[:, 1:]
    out = g + mu[0] * (left - g) + mu[1] * (right - g) + mu[2] * (up - g) + mu[3] * (down - g)
    return out.reshape(b, s, ch)


def _ssd_chunked(x, dt, log_a, bm, cm):
    n, t, h, p = x.shape
    g, ds = bm.shape[2], bm.shape[3]
    q = SSD_CHUNK
    nc = t // q
    rep = h // g
    b_h = jnp.repeat(bm, rep, axis=2).reshape(n, nc, q, h, ds)
    c_h = jnp.repeat(cm, rep, axis=2).reshape(n, nc, q, h, ds)
    xdt = (x * dt[..., None]).reshape(n, nc, q, h, p)
    acs = jnp.cumsum(log_a.reshape(n, nc, q, h), axis=2)
    acs_h = jnp.moveaxis(acs, 3, 1)
    tril = jnp.tril(jnp.ones((q, q), dtype=bool))
    seg = acs_h[..., :, None] - acs_h[..., None, :]
    lmat = jnp.exp(jnp.where(tril, seg, -jnp.inf))
    scores = jnp.einsum('nclhd,ncshd->nhcls', c_h, b_h) * lmat
    y_diag = jnp.einsum('nhcls,ncshp->nclhp', scores, xdt)
    decay_end = jnp.exp(acs[:, :, -1:, :] - acs)
    states = jnp.einsum('nclhd,nclh,nclhp->nchpd', b_h, decay_end, xdt)
    chunk_decay = jnp.exp(acs[:, :, -1, :])

    def step(hprev, inp):
        st, dec = inp
        return hprev * dec[..., None, None] + st, hprev

    h0 = jnp.zeros((n, h, p, ds), x.dtype)
    _, h_in = lax.scan(step, h0, (jnp.moveaxis(states, 1, 0), jnp.moveaxis(chunk_decay, 1, 0)))
    h_in = jnp.moveaxis(h_in, 0, 1)
    y_off = jnp.einsum('nclhd,nchpd,nclh->nclhp', c_h, h_in, jnp.exp(acs))
    return (y_diag + y_off).reshape(n, t, h, p)


def _rwkv7_scan(r, w, k, v, kk, a):
    n, t, h, dk = r.shape

    def step(S, inp):
        r_t, w_t, k_t, v_t, kk_t, a_t = inp
        sa = -jnp.einsum('nhvk,nhk->nhv', S, kk_t)
        S = S * w_t[:, :, None, :] + sa[..., None] * (kk_t * a_t)[:, :, None, :] + v_t[..., None] * k_t[:, :, None, :]
        return S, jnp.einsum('nhvk,nhk->nhv', S, r_t)

    S0 = jnp.zeros((n, h, dk, dk), r.dtype)
    xs = (jnp.moveaxis(r, 1, 0), jnp.moveaxis(w, 1, 0), jnp.moveaxis(k, 1, 0),
          jnp.moveaxis(v, 1, 0), jnp.moveaxis(kk, 1, 0), jnp.moveaxis(a, 1, 0))
    _, y = lax.scan(step, S0, xs)
    return jnp.moveaxis(y, 0, 1)


def _gla_chunked(q, k, v, log_a):
    n, t, h, dk = q.shape
    dv = v.shape[-1]
    c = GLA_CHUNK
    nc = t // c
    q = q.reshape(n, nc, c, h, dk)
    k = k.reshape(n, nc, c, h, dk)
    v = v.reshape(n, nc, c, h, dv)
    acs = jnp.cumsum(log_a.reshape(n, nc, c, h, dk), axis=2)
    tril = jnp.tril(jnp.ones((c, c), dtype=bool))[:, :, None, None]
    seg = acs[:, :, :, None] - acs[:, :, None, :]
    decay = jnp.exp(jnp.where(tril, seg, -jnp.inf))
    scores = jnp.einsum('ncihd,ncjhd,ncijhd->nchij', q, k, decay)
    o_intra = jnp.einsum('nchij,ncjhv->ncihv', scores, v)
    q_dec = q * jnp.exp(acs)
    k_dec = k * jnp.exp(acs[:, :, -1:] - acs)
    states = jnp.einsum('ncjhd,ncjhv->nchdv', k_dec, v)
    chunk_decay = jnp.exp(acs[:, :, -1])

    def step(S, inp):
        st, dec = inp
        return S * dec[..., None] + st, S

    S0 = jnp.zeros((n, h, dk, dv), q.dtype)
    _, s_in = lax.scan(step, S0, (jnp.moveaxis(states, 1, 0), jnp.moveaxis(chunk_decay, 1, 0)))
    s_in = jnp.moveaxis(s_in, 0, 1)
    o_inter = jnp.einsum('ncihd,nchdv->ncihv', q_dec, s_in)
    return (o_intra + o_inter).reshape(n, t, h, dv)


def _token_mixer(u, n_ctx, w_in, ssd_conv_w, ssd_conv_b, ssd_dt_bias, ssd_a_log, ssd_d, ssd_norm_w,
                 rwkv_mu, rwkv_w0, rwkv_w2, rwkv_a0, rwkv_a2, rwkv_g2, rwkv_kk, rwkv_ka, rwkv_rk,
                 rwkv_lnx_w, rwkv_lnx_b, gla_gu, gla_gb, gla_norm_w, w_out):
    f32 = jnp.float32
    bsz, t, _ = u.shape
    flip = lambda zz: _seg_flip(zz, n_ctx)

    def both(fwd, bwd):
        s = jnp.stack([fwd, flip(bwd)])
        return s.reshape((2 * bsz,) + s.shape[2:])

    def unboth(y):
        y = y.reshape((2, bsz) + y.shape[1:])
        return y[0] + flip(y[1])

    z, xbc, dt, rw, gq, gk, gv, gd, og = _split(u @ w_in, IN_WIDTHS)

    xbc = jax.nn.silu(jnp.concatenate([_dwconv_centred(xbc[:, :n_ctx], ssd_conv_w, ssd_conv_b),
                                       _dwconv_centred(xbc[:, n_ctx:], ssd_conv_w, ssd_conv_b)], axis=1))
    xs, bm, cm = _split(xbc.astype(f32), (SSD_INNER, SSD_GROUPS * SSD_STATE, SSD_GROUPS * SSD_STATE))
    xs = xs.reshape(bsz, t, SSD_HEADS, SSD_HEAD_DIM)
    bm = bm.reshape(bsz, t, SSD_GROUPS, SSD_STATE)
    cm = cm.reshape(bsz, t, SSD_GROUPS, SSD_STATE)
    dt = jax.nn.softplus(dt.astype(f32).reshape(bsz, t, 2, SSD_HEADS) + ssd_dt_bias.astype(f32))
    neg_a = -jnp.exp(ssd_a_log.astype(f32))
    dt_f, dt_b = dt[:, :, 0], dt[:, :, 1]
    y = unboth(_ssd_chunked(both(xs, xs), both(dt_f, dt_b), both(dt_f * neg_a[0], dt_b * neg_a[1]),
                            both(bm, bm), both(cm, cm)))
    y = y + ssd_d.astype(f32)[:, None] * xs
    y_ssd = _rmsnorm_groups(y.reshape(bsz, t, SSD_INNER) * jax.nn.silu(z.astype(f32)), ssd_norm_w,
                            SSD_INNER // SSD_GROUPS)

    rw = jnp.concatenate([_seq_shift_mix(rw[:, :n_ctx], rwkv_mu[0], rwkv_mu[1]),
                          _grid_shift_mix(rw[:, n_ctx:], rwkv_mu)], axis=1)
    r, k, v, wd_f, wd_b, ad_f, ad_b, gd_r = _split(
        rw.astype(f32), (RWKV_WIDTH,) * 3 + (RWKV_DECAY_LORA,) * 2 + (RWKV_ICLR_LORA,) * 2 + (RWKV_GATE_LORA,))
    hd = lambda zz: zz.reshape(bsz, t, RWKV_HEADS, RWKV_HEAD_DIM)
    decay = lambda wd, i: jnp.exp(-jnp.exp(-jax.nn.softplus(-(rwkv_w0[i] + jnp.tanh(wd) @ rwkv_w2[i])) - 0.5))
    iclr = lambda ad, i: jax.nn.sigmoid(rwkv_a0[i] + ad @ rwkv_a2[i])
    w_f, w_b = hd(decay(wd_f, 0)), hd(decay(wd_b, 1))
    a_f, a_b = hd(iclr(ad_f, 0)), hd(iclr(ad_b, 1))
    r, k, v = hd(r), hd(k), hd(v)
    kk = k * rwkv_kk.reshape(RWKV_HEADS, RWKV_HEAD_DIM)
    kk = kk / jnp.maximum(jnp.sqrt(jnp.sum(kk * kk, -1, keepdims=True)), 1e-12)
    ka = rwkv_ka.reshape(RWKV_HEADS, RWKV_HEAD_DIM)
    k_f = k * (1.0 + (a_f - 1.0) * ka)
    k_b = k * (1.0 + (a_b - 1.0) * ka)
    y = unboth(_rwkv7_scan(both(r, r), both(w_f, w_b), both(k_f, k_b), both(v, v), both(kk, kk), both(a_f, a_b)))
    mu = jnp.mean(y, -1, keepdims=True)
    var = jnp.mean(jnp.square(y - mu), -1, keepdims=True)
    y = ((y - mu) * lax.rsqrt(var + RWKV_LNX_EPS)).reshape(bsz, t, RWKV_WIDTH) * rwkv_lnx_w + rwkv_lnx_b
    bonus = (jnp.sum(r * (k_f + k_b) * rwkv_rk, -1, keepdims=True) * v).reshape(bsz, t, RWKV_WIDTH)
    y_rwkv = (y + bonus) * (jax.nn.sigmoid(gd_r) @ rwkv_g2)

    gh = lambda zz, d: zz.reshape(bsz, t, GLA_HEADS, d)
    q = gh(gq.astype(f32), GLA_KEY_DIM) * GLA_KEY_DIM ** -0.5
    kg = gh(gk.astype(f32), GLA_KEY_DIM)
    vg = gh(gv.astype(f32), GLA_VAL_DIM)
    gd_f, gd_b = _split(gd.astype(f32), (GLA_GATE_LORA, GLA_GATE_LORA))
    la = lambda gg, i: gh(jax.nn.log_sigmoid(gg @ gla_gu[i] + gla_gb[i]) / GLA_GATE_NORMALIZER, GLA_KEY_DIM)
    o = unboth(_gla_chunked(both(q, q), both(kg, kg), both(vg, vg), both(la(gd_f, 0), la(gd_b, 1))))
    y_gla = _rmsnorm_groups(o.reshape(bsz, t, GLA_V_WIDTH), gla_norm_w, GLA_VAL_DIM) * jax.nn.silu(og.astype(f32))

    y_mix = jnp.concatenate([y_ssd, y_rwkv, y_gla], axis=-1).astype(u.dtype)
    return y_mix @ w_out


def _moe_ffn(h, router_w, router_b, w1, w3, w2, sw1, sw3, sw2):
    f32 = jnp.float32
    t = h.shape[0]
    per_group = N_EXPERTS // N_GROUPS
    scores = jax.nn.sigmoid((h @ router_w).astype(f32))
    sel = scores + router_b.astype(f32)
    grp_score = lax.top_k(sel.reshape(t, N_GROUPS, per_group), 2)[0].sum(-1)
    _, top_g = lax.top_k(grp_score, TOPK_GROUPS)
    gmask = jax.nn.one_hot(top_g, N_GROUPS, dtype=f32).sum(1) > 0
    sel = jnp.where(jnp.repeat(gmask, per_group, axis=1), sel, -jnp.inf)
    _, idx = lax.top_k(sel, TOP_K)
    wts = jnp.take_along_axis(scores, idx, axis=1)
    wts = wts / jnp.sum(wts, -1, keepdims=True) * ROUTED_SCALE

    n_assign = t * TOP_K
    flat_e = idx.reshape(-1)
    order = jnp.argsort(flat_e)
    sorted_e = flat_e[order]
    counts = jnp.bincount(flat_e, length=N_EXPERTS)
    padded = (counts + MOE_BLOCK - 1) // MOE_BLOCK * MOE_BLOCK
    pend = jnp.cumsum(padded)
    pstart = pend - padded
    sstart = jnp.cumsum(counts) - counts
    dest = pstart[sorted_e] + jnp.arange(n_assign, dtype=jnp.int32) - sstart[sorted_e]
    n_rows = -(-n_assign // MOE_BLOCK) * MOE_BLOCK + N_EXPERTS * MOE_BLOCK
    n_blocks = n_rows // MOE_BLOCK
    row_tok = jnp.zeros((n_rows,), jnp.int32).at[dest].set((order // TOP_K).astype(jnp.int32))
    row_w = jnp.zeros((n_rows,), f32).at[dest].set(wts.reshape(-1)[order])
    block_e = jnp.minimum(jnp.searchsorted(pend, jnp.arange(n_blocks, dtype=jnp.int32) * MOE_BLOCK, side='right'),
                          N_EXPERTS - 1)

    def body(acc, blk):
        tok, wt, e = blk
        xb = h[tok]
        hid = jax.nn.silu(xb @ w1[e]) * (xb @ w3[e])
        return acc.at[tok].add(wt[:, None] * (hid @ w2[e]).astype(f32)), None

    routed, _ = lax.scan(body, jnp.zeros(h.shape, f32),
                         (row_tok.reshape(n_blocks, MOE_BLOCK), row_w.reshape(n_blocks, MOE_BLOCK), block_e))
    shared = (jax.nn.silu(h @ sw1) * (h @ sw3)) @ sw2
    return routed.astype(h.dtype) + shared


def setup_inputs(seed: int = 0) -> dict:
    key = jax.random.key(seed)
    ks = jax.random.split(key, 40)
    L, D, E, F, Fs = DEPTH, D_MODEL, N_EXPERTS, EXPERT_DIM, SHARED_DIM
    nrm = lambda k, shape, s: jax.random.normal(k, shape, jnp.float32) * s
    one = lambda k, shape: 1.0 + 0.1 * jax.random.normal(k, shape, jnp.float32)
    dt0 = jnp.exp(jax.random.uniform(ks[9], (L, 2, SSD_HEADS), jnp.float32, np.log(1e-3), np.log(1e-1)))
    return {
        'x': nrm(ks[0], (BATCH, SEQ, D), 1.0),
        'c': nrm(ks[1], (BATCH, D), 1.0),
        'ctx': nrm(ks[2], (BATCH, CTX_LEN, D), 1.0),
        'c_ctx': nrm(ks[3], (D,), 1.0),
        'ada_w': nrm(ks[4], (L, D, N_MOD * D), D ** -0.5 * ADA_SCALE),
        'ada_b': nrm(ks[5], (L, N_MOD * D), 0.02),
        'w_in': nrm(ks[6], (L, D, N_IN), D ** -0.5),
        'ssd_conv_w': nrm(ks[7], (L, SSD_CONV, SSD_XBC), SSD_CONV ** -0.5),
        'ssd_conv_b': nrm(ks[8], (L, SSD_XBC), 0.02),
        'ssd_dt_bias': dt0 + jnp.log(-jnp.expm1(-dt0)),
        'ssd_a_log': jnp.log(jax.random.uniform(ks[10], (L, 2, SSD_HEADS), jnp.float32, 1.0, 16.0)),
        'ssd_d': one(ks[11], (L, SSD_HEADS)),
        'ssd_norm_w': one(ks[12], (L, SSD_INNER)),
        'rwkv_mu': jax.random.uniform(ks[13], (L, 4, RWKV_COLS), jnp.float32, 0.0, 0.5),
        'rwkv_w0': jax.random.uniform(ks[14], (L, 2, RWKV_WIDTH), jnp.float32, -6.0, 1.0),
        'rwkv_w2': nrm(ks[15], (L, 2, RWKV_DECAY_LORA, RWKV_WIDTH), 0.1),
        'rwkv_a0': nrm(ks[16], (L, 2, RWKV_WIDTH), 0.1),
        'rwkv_a2': nrm(ks[17], (L, 2, RWKV_ICLR_LORA, RWKV_WIDTH), 0.1),
        'rwkv_g2': nrm(ks[18], (L, RWKV_GATE_LORA, RWKV_WIDTH), RWKV_GATE_LORA ** -0.5),
        'rwkv_kk': one(ks[19], (L, RWKV_WIDTH)),
        'rwkv_ka': one(ks[20], (L, RWKV_WIDTH)),
        'rwkv_rk': nrm(ks[21], (L, RWKV_HEADS, RWKV_HEAD_DIM), 0.1),
        'rwkv_lnx_w': one(ks[22], (L, RWKV_WIDTH)),
        'rwkv_lnx_b': nrm(ks[23], (L, RWKV_WIDTH), 0.02),
        'gla_gu': nrm(ks[24], (L, 2, GLA_GATE_LORA, GLA_K_WIDTH), GLA_GATE_LORA ** -0.5),
        'gla_gb': nrm(ks[25], (L, 2, GLA_K_WIDTH), 0.5),
        'gla_norm_w': one(ks[26], (L, GLA_V_WIDTH)),
        'w_out': nrm(ks[27], (L, MIX_WIDTH, D), MIX_WIDTH ** -0.5 * DN_BETA),
        'ln1_w': one(ks[28], (L, D)),
        'ln1_b': nrm(ks[29], (L, D), 0.02),
        'ln2_w': one(ks[30], (L, D)),
        'ln2_b': nrm(ks[31], (L, D), 0.02),
        'router_w': nrm(ks[32], (L, D, E), D ** -0.5),
        'router_b': nrm(ks[33], (L, E), 0.01),
        'exp_w1': nrm(ks[34], (L, E, D, F), D ** -0.5),
        'exp_w3': nrm(ks[35], (L, E, D, F), D ** -0.5),
        'exp_w2': nrm(ks[36], (L, E, F, D), F ** -0.5 * DN_BETA),
        'sh_w1': nrm(ks[37], (L, D, Fs), D ** -0.5),
        'sh_w3': nrm(ks[38], (L, D, Fs), D ** -0.5),
        'sh_w2': nrm(ks[39], (L, Fs, D), Fs ** -0.5 * DN_BETA),
    }


def reference(x, c, ctx, c_ctx, ada_w, ada_b, w_in, ssd_conv_w, ssd_conv_b, ssd_dt_bias, ssd_a_log, ssd_d,
              ssd_norm_w, rwkv_mu, rwkv_w0, rwkv_w2, rwkv_a0, rwkv_a2, rwkv_g2, rwkv_kk, rwkv_ka, rwkv_rk,
              rwkv_lnx_w, rwkv_lnx_b, gla_gu, gla_gb, gla_norm_w, w_out, ln1_w, ln1_b, ln2_w, ln2_b,
              router_w, router_b, exp_w1, exp_w3, exp_w2, sh_w1, sh_w3, sh_w2):
    n_ctx = ctx.shape[1]
    xc = ctx
    for l in range(DEPTH):
        mod = jnp.split(jax.nn.silu(c) @ ada_w[l] + ada_b[l], N_MOD, axis=-1)
        modc = jnp.split(jax.nn.silu(c_ctx) @ ada_w[l] + ada_b[l], N_MOD, axis=-1)
        sh1, sc1, g1, sh2, sc2, g2 = [m[:, None, :] for m in mod]
        sh1c, sc1c, g1c, sh2c, sc2c, g2c = modc

        u = jnp.concatenate([_layernorm(xc) * (1.0 + sc1c) + sh1c,
                             _layernorm(x) * (1.0 + sc1) + sh1], axis=1)
        mix = _token_mixer(u, n_ctx, w_in[l], ssd_conv_w[l], ssd_conv_b[l], ssd_dt_bias[l], ssd_a_log[l], ssd_d[l],
                           ssd_norm_w[l], rwkv_mu[l], rwkv_w0[l], rwkv_w2[l], rwkv_a0[l], rwkv_a2[l], rwkv_g2[l],
                           rwkv_kk[l], rwkv_ka[l], rwkv_rk[l], rwkv_lnx_w[l], rwkv_lnx_b[l], gla_gu[l], gla_gb[l],
                           gla_norm_w[l], w_out[l])
        x = _layernorm(DN_ALPHA * x + g1 * mix[:, n_ctx:], ln1_w[l], ln1_b[l])
        xc = _layernorm(DN_ALPHA * xc + g1c * mix[:, :n_ctx], ln1_w[l], ln1_b[l])

        h = _layernorm(x) * (1.0 + sc2) + sh2
        moe_w = (router_w[l], router_b[l], exp_w1[l], exp_w3[l], exp_w2[l], sh_w1[l], sh_w3[l], sh_w2[l])
        if l < DEPTH - 1:
            hc = _layernorm(xc) * (1.0 + sc2c) + sh2c
            tok = jnp.concatenate([hc, h], axis=1)
            f = _moe_ffn(tok.reshape(-1, D_MODEL), *moe_w).reshape(tok.shape)
            xc = _layernorm(DN_ALPHA * xc + g2c * f[:, :n_ctx], ln2_w[l], ln2_b[l])
            f = f[:, n_ctx:]
        else:
            f = _moe_ffn(h.reshape(-1, D_MODEL), *moe_w).reshape(h.shape)
        x = _layernorm(DN_ALPHA * x + g2 * f, ln2_w[l], ln2_b[l])
    return x
```

```python
import functools

import jax
import jax.numpy as jnp
import numpy as np
from jax import lax
from jax.experimental import pallas as pl
from jax.experimental.pallas import tpu as pltpu

F32 = jnp.float32
BF16 = jnp.bfloat16

D_MODEL = 1024
N_CTX = 256
GRID_W = 64
N_MOD = 6
LN_EPS = 1e-5
DEPTH = 4
DN_ALPHA = (2 * DEPTH) ** 0.25

SSD_HEADS = 8
SSD_HEAD_DIM = 64
SSD_INNER = 512
SSD_STATE = 128
SSD_XBC = 1024

RWKV_WIDTH = 256
RWKV_HEAD_DIM = 64
RWKV_COLS = 1152
RWKV_LNX_EPS = 64e-5

GLA_K_WIDTH = 128
GLA_V_WIDTH = 256
GLA_KEY_DIM = 32
GLA_VAL_DIM = 64
GLA_GATE_LORA = 16
GLA_GATE_NORMALIZER = 16.0
GLA_SUB = 64

N_EXPERTS = 64
TOP_K = 8
N_GROUPS = 8
TOPK_GROUPS = 4
EXPERT_DIM = 256
ROUTED_SCALE = 2.5
EXPERTS_PER_STEP = 4

TOK_BLOCK = 256
SCAN_BLOCK = 128
MOE_TOK = 1024
V7X_VMEM_LIMIT = 56 * 1024 * 1024

_IN_PIECES = (("z", 512), ("xbc", 1024), ("rw", 1152), ("gq", 128), ("gk", 128), ("gv", 256), ("og", 256),
              ("small", 128))
IN_PAD = sum(w for _, w in _IN_PIECES)


def _cparams(sem):
    return pltpu.CompilerParams(dimension_semantics=sem, vmem_limit_bytes=V7X_VMEM_LIMIT)


def _split3(a):
    hi = a.astype(BF16)
    r1 = a - hi.astype(F32)
    mid = r1.astype(BF16)
    lo = (r1 - mid.astype(F32)).astype(BF16)
    return hi, mid, lo


def _dot(a, b):
    return jnp.dot(a, b, preferred_element_type=F32)


def _dot_nt(a, b):
    return lax.dot_general(a, b, (((1,), (1,)), ((), ())), preferred_element_type=F32)


def _dot_x_exact(a, e):
    hi, mid, lo = _split3(a)
    return _dot(hi, e) + (_dot(mid, e) + _dot(lo, e))


def _dot_exact_x(e, a):
    hi, mid, lo = _split3(a)
    return _dot(e, hi) + (_dot(e, mid) + _dot(e, lo))


def _dot_hp(a, b):
    ah, am, _ = _split3(a)
    bh, bm, _ = _split3(b)
    return _dot(ah, bh) + (_dot(ah, bm) + _dot(am, bh))


def _sigmoid(x):
    return 1.0 / (1.0 + jnp.exp(-x))


def _silu(x):
    return x * _sigmoid(x)


def _softplus(x):
    return jnp.maximum(x, 0.0) + jnp.log(1.0 + jnp.exp(-jnp.abs(x)))


def _layernorm(x):
    mu = jnp.mean(x, axis=-1, keepdims=True)
    xc = x - mu
    var = jnp.mean(xc * xc, axis=-1, keepdims=True)
    return xc * lax.rsqrt(var + LN_EPS)


def _iota(shape, dim):
    return lax.broadcasted_iota(jnp.int32, shape, dim)


def _tri(n, upper):
    r = _iota((n, n), 0)
    c = _iota((n, n), 1)
    return jnp.where(upper, c - r, r - c) >= 0


def _block_of(d, j, nb):
    return jnp.where(d == 0, j, jnp.where(j == 0, 0, nb - j))


def _group_ones(n, g):
    r = _iota((n, n), 0) // g
    c = _iota((n, n), 1) // g
    return (r == c).astype(BF16)


def _ada_body(c_ref, w_ref, b_ref, o_ref):
    a = _silu(c_ref[...]).astype(BF16)
    o_ref[0] = _dot(a, w_ref[0].astype(BF16)) + b_ref[0]


def _ada_all(cond, ada_w, ada_b):
    nl, d, n = ada_w.shape
    tn = 1536
    rows = cond.shape[0]
    return pl.pallas_call(
        _ada_body,
        grid=(nl, n // tn),
        in_specs=[pl.BlockSpec((rows, d), lambda l, k: (0, 0)),
                  pl.BlockSpec((1, d, tn), lambda l, k: (l, 0, k)),
                  pl.BlockSpec((1, 1, tn), lambda l, k: (l, 0, k))],
        out_specs=pl.BlockSpec((1, rows, tn), lambda l, k: (l, 0, k)),
        out_shape=jax.ShapeDtypeStruct((nl, rows, n), F32),
        compiler_params=_cparams(("arbitrary", "arbitrary")),
    )(cond, ada_w, ada_b.reshape(nl, 1, n))


def _inproj_body(x_ref, m_ref, w_ref, *out_refs):
    x = x_ref[0]
    m = m_ref[0, 0]
    u = _layernorm(x) * (1.0 + m[1:2]) + m[0:1]
    y = _dot(u.astype(BF16), w_ref[...])
    off = 0
    for (_, wdt), o in zip(_IN_PIECES, out_refs):
        o[0] = y[:, off:off + wdt]
        off += wdt


def _inproj(xs, msel, w_in_p):
    b, t, d = xs.shape
    nb = t // TOK_BLOCK
    out_shape = [jax.ShapeDtypeStruct((b, t, w), F32) for _, w in _IN_PIECES]
    out_specs = [pl.BlockSpec((1, TOK_BLOCK, w), lambda i, j: (i, j, 0)) for _, w in _IN_PIECES]
    return pl.pallas_call(
        _inproj_body,
        grid=(b, nb),
        in_specs=[pl.BlockSpec((1, TOK_BLOCK, d), lambda i, j: (i, j, 0)),
                  pl.BlockSpec((1, 1, N_MOD, d), lambda i, j: (i, jnp.minimum(j, 1), 0, 0)),
                  pl.BlockSpec((d, IN_PAD), lambda i, j: (0, 0))],
        out_specs=out_specs,
        out_shape=out_shape,
        compiler_params=_cparams(("arbitrary", "arbitrary")),
    )(xs, msel, w_in_p)


def _ssd_body(nb, xbc_ref, hp_ref, hn_ref, small_ref, cw_ref, cb_ref, dtb_ref, alog_ref, dsk_ref,
              y_ref, h_scr, buf_scr):
    q = TOK_BLOCK
    d = pl.program_id(1)
    j = pl.program_id(2)
    blk = _block_of(d, j, nb)

    @pl.when(j == 0)
    def _():
        h_scr[...] = jnp.zeros_like(h_scr)

    seg_first = (blk == 0) | (blk == 1)
    seg_last = (blk == 0) | (blk == nb - 1)
    cur = xbc_ref[0]
    buf_scr[8:8 + q] = cur
    buf_scr[0:8] = jnp.where(seg_first, 0.0, hp_ref[0, 0])
    buf_scr[8 + q:16 + q] = jnp.where(seg_last, 0.0, hn_ref[0, 0])
    prev = buf_scr[7:7 + q]
    nxt = buf_scr[9:9 + q]
    cw = cw_ref[0]
    act = _silu(prev * cw[0:1] + cur * cw[1:2] + nxt * cw[2:3] + cb_ref[...])
    xs = act[:, :SSD_INNER]

    upper = d == 1
    tri_b = _tri(q, upper)
    tri = tri_b.astype(BF16)
    tri_t = _tri(q, jnp.logical_not(upper)).astype(BF16)

    lane = _iota((1, 128), 1)
    dt_all = _softplus(small_ref[0] + dtb_ref[...])
    neg_a = jnp.where(lane < 2 * SSD_HEADS, -jnp.exp(alog_ref[...]), 0.0)
    la_all = dt_all * neg_a
    er = _iota((128, SSD_INNER), 0)
    ec = _iota((128, SSD_INNER), 1)
    e_exp = (er == SSD_HEADS * d + ec // SSD_HEAD_DIM).astype(BF16)
    er8 = _iota((128, 128), 0)
    ec8 = _iota((128, 128), 1)
    e8 = ((er8 == SSD_HEADS * d + ec8) & (ec8 < SSD_HEADS)).astype(BF16)

    dt_exp = _dot_x_exact(dt_all, e_exp)
    la_exp = _dot_x_exact(la_all, e_exp)
    la8 = _dot_x_exact(la_all, e8)
    cs_exp = _dot_exact_x(tri, la_exp)
    cs8 = _dot_exact_x(tri, la8)
    cs_t = _dot_x_exact(la8.T, tri_t)
    tot_exp = jnp.sum(la_exp, axis=0, keepdims=True)

    xdt = xs * dt_exp
    lane_q = _iota((q, 128), 1)
    ys = []
    for p in range(SSD_HEADS // 2):
        xdt_p = xdt[:, 128 * p:128 * (p + 1)].astype(BF16)
        halves = []
        for h in (2 * p, 2 * p + 1):
            g = h // (SSD_HEADS // 2)
            c_g = act[:, 768 + 128 * g:896 + 128 * g].astype(BF16)
            b_g = act[:, 512 + 128 * g:640 + 128 * g].astype(BF16)
            gmat = _dot_nt(c_g, b_g)
            seg = cs8[:, h:h + 1] - cs_t[h:h + 1, :]
            lmat = jnp.where(tri_b, jnp.exp(jnp.minimum(seg, 0.0)), 0.0)
            halves.append(_dot((gmat * lmat).astype(BF16), xdt_p))
        ys.append(jnp.where(lane_q < SSD_HEAD_DIM, halves[0], halves[1]))
    y = jnp.concatenate(ys, axis=1)

    dec = jnp.exp(tot_exp - cs_exp)
    xd = (xdt * dec).astype(BF16)
    offs = []
    for g in range(2):
        c_g = act[:, 768 + 128 * g:896 + 128 * g].astype(BF16)
        b_g = act[:, 512 + 128 * g:640 + 128 * g]
        hg = h_scr[g]
        offs.append(_dot(c_g, hg.astype(BF16)))
        hn = _dot(b_g.T.astype(BF16), xd[:, 256 * g:256 * (g + 1)])
        h_scr[g] = hg * jnp.exp(tot_exp[:, 256 * g:256 * (g + 1)]) + hn
    y = y + jnp.concatenate(offs, axis=1) * jnp.exp(cs_exp)
    y = y + jnp.where(d == 0, dsk_ref[...], 0.0) * xs
    y_ref[0, 0] = y


def _ssd(xbc, small, conv_w, conv_b, dtb_pad, alog_pad, dskip_exp):
    b, t, _ = xbc.shape
    nb = t // TOK_BLOCK
    r8 = TOK_BLOCK // 8
    xbc8 = xbc.reshape(b, t // 8, 8, SSD_XBC)

    def main_map(i, d, j):
        return (i, _block_of(d, j, nb), 0)

    def prev_map(i, d, j):
        return (i, jnp.maximum(_block_of(d, j, nb) * r8 - 1, 0), 0, 0)

    def next_map(i, d, j):
        return (i, jnp.minimum(_block_of(d, j, nb) * r8 + r8, t // 8 - 1), 0, 0)

    def out_map(i, d, j):
        return (d, i, _block_of(d, j, nb), 0)

    const2 = lambda i, d, j: (0, 0)
    return pl.pallas_call(
        functools.partial(_ssd_body, nb),
        grid=(b, 2, nb),
        in_specs=[pl.BlockSpec((1, TOK_BLOCK, SSD_XBC), main_map),
                  pl.BlockSpec((1, 1, 8, SSD_XBC), prev_map),
                  pl.BlockSpec((1, 1, 8, SSD_XBC), next_map),
                  pl.BlockSpec((1, TOK_BLOCK, 128), main_map),
                  pl.BlockSpec((1, 3, SSD_XBC), lambda i, d, j: (0, 0, 0)),
                  pl.BlockSpec((1, SSD_XBC), const2),
                  pl.BlockSpec((1, 128), const2),
                  pl.BlockSpec((1, 128), const2),
                  pl.BlockSpec((1, SSD_INNER), const2)],
        out_specs=pl.BlockSpec((1, 1, TOK_BLOCK, SSD_INNER), out_map),
        out_shape=jax.ShapeDtypeStruct((2, b, t, SSD_INNER), F32),
        scratch_shapes=[pltpu.VMEM((2, SSD_STATE, 256), F32),
                        pltpu.VMEM((TOK_BLOCK + 16, SSD_XBC), F32)],
        compiler_params=_cparams(("arbitrary", "arbitrary", "arbitrary")),
    )(xbc, xbc8, xbc8, small, conv_w[None], conv_b, dtb_pad, alog_pad, dskip_exp)


def _gla_body(nb, q_ref, k_ref, v_ref, small_ref, gu_ref, gb_ref, o_ref, s_scr, la_scr):
    d = pl.program_id(1)
    j = pl.program_id(2)
    c = GLA_SUB
    nsub = TOK_BLOCK // c

    @pl.when(j == 0)
    def _():
        s_scr[...] = jnp.zeros_like(s_scr)

    upper = d == 1
    tri_b = _tri(c, upper)
    tri = tri_b.astype(BF16)

    sm = small_ref[0]
    lr = _iota((128, 128), 0)
    lc = _iota((128, 128), 1)
    gsel = ((lr == 2 * SSD_HEADS + GLA_GATE_LORA * d + lc) & (lc < GLA_GATE_LORA)).astype(BF16)
    gd = _dot_x_exact(sm, gsel)
    gu = jnp.where(upper, gu_ref[1], gu_ref[0])
    gbias = jnp.where(upper, gb_ref[1:2], gb_ref[0:1])
    pre = _dot_hp(gd, gu) + gbias
    la_scr[...] = -_softplus(-pre) * (1.0 / GLA_GATE_NORMALIZER)

    lane_k = _iota((c, GLA_K_WIDTH), 1) // GLA_KEY_DIM
    lane_v = _iota((c, GLA_V_WIDTH), 1) // GLA_VAL_DIM
    st_mask = (_iota((GLA_V_WIDTH, GLA_K_WIDTH), 0) // GLA_VAL_DIM
               == _iota((GLA_V_WIDTH, GLA_K_WIDTH), 1) // GLA_KEY_DIM)

    for si in range(nsub):
        lo = pl.multiple_of(jnp.where(upper, (nsub - 1 - si) * c, si * c), c)
        rows = pl.ds(lo, c)
        la = la_scr[rows, :]
        qq = q_ref[0, rows, :] * (GLA_KEY_DIM ** -0.5)
        kk = k_ref[0, rows, :]
        vv = v_ref[0, rows, :]
        cs = _dot_exact_x(tri, la)
        tot = jnp.sum(la, axis=0, keepdims=True)
        qt = qq * jnp.exp(cs)
        kt = (kk * jnp.exp(jnp.minimum(-cs, 80.0))).astype(BF16)
        kdec = (kk * jnp.exp(tot - cs)).astype(BF16)
        st = s_scr[...]
        o = _dot_nt(qt.astype(BF16), st.astype(BF16))
        vb = vv.astype(BF16)
        for h in range(GLA_K_WIDTH // GLA_KEY_DIM):
            qh = jnp.where(lane_k == h, qt, 0.0).astype(BF16)
            a = jnp.where(tri_b, _dot_nt(qh, kt), 0.0)
            o = o + jnp.where(lane_v == h, _dot(a.astype(BF16), vb), 0.0)
        upd = _dot(vv.T.astype(BF16), kdec)
        s_scr[...] = st * jnp.exp(tot) + jnp.where(st_mask, upd, 0.0)
        o_ref[0, 0, rows, :] = o


def _gla(gq, gk, gv, small, gu_pad, gb):
    b, t, _ = gq.shape
    nb = t // TOK_BLOCK

    def main_map(i, d, j):
        return (i, _block_of(d, j, nb), 0)

    def out_map(i, d, j):
        return (d, i, _block_of(d, j, nb), 0)

    return pl.pallas_call(
        functools.partial(_gla_body, nb),
        grid=(b, 2, nb),
        in_specs=[pl.BlockSpec((1, TOK_BLOCK, GLA_K_WIDTH), main_map),
                  pl.BlockSpec((1, TOK_BLOCK, GLA_K_WIDTH), main_map),
                  pl.BlockSpec((1, TOK_BLOCK, GLA_V_WIDTH), main_map),
                  pl.BlockSpec((1, TOK_BLOCK, 128), main_map),
                  pl.BlockSpec((2, 128, GLA_K_WIDTH), lambda i, d, j: (0, 0, 0)),
                  pl.BlockSpec((2, GLA_K_WIDTH), lambda i, d, j: (0, 0))],
        out_specs=pl.BlockSpec((1, 1, TOK_BLOCK, GLA_V_WIDTH), out_map),
        out_shape=jax.ShapeDtypeStruct((2, b, t, GLA_V_WIDTH), F32),
        scratch_shapes=[pltpu.VMEM((GLA_V_WIDTH, GLA_K_WIDTH), F32),
                        pltpu.VMEM((TOK_BLOCK, GLA_K_WIDTH), F32)],
        compiler_params=_cparams(("arbitrary", "arbitrary", "arbitrary")),
    )(gq, gk, gv, small, gu_pad, gb)


_RWKV_OUTS = ("r", "v", "kk", "w_f", "k_f", "kka_f", "w_b", "k_b", "kka_b", "bonus", "gate")


def _rwkv_prep_body(nb, rw_ref, hp_ref, hn_ref, mu_ref, w0_ref, w2_ref, a0_ref, a2_ref, g2_ref, kkp_ref, ka_ref,
                    rk_ref, *refs):
    outs = dict(zip(_RWKV_OUTS, refs[:len(_RWKV_OUTS)]))
    buf_scr = refs[len(_RWKV_OUTS)]
    q = TOK_BLOCK
    j = pl.program_id(1)
    is_ctx = j == 0
    z = rw_ref[0]
    zeros8 = jnp.zeros((8, RWKV_COLS), F32)
    buf_scr[0:8] = zeros8
    buf_scr[8 + q:16 + q] = zeros8
    buf_scr[8:8 + q] = z
    t = _iota((q, 1), 0)
    col = t % GRID_W
    prev = jnp.where(jnp.where(is_ctx, t, col) == 0, 0.0, buf_scr[7:7 + q])
    nxt = jnp.where(jnp.where(is_ctx, t - (q - 1), col - (GRID_W - 1)) == 0, 0.0, buf_scr[9:9 + q])
    up = jnp.concatenate([jnp.where(j == 1, 0.0, hp_ref[0, 0]), z[:q - GRID_W]], axis=0)
    down = jnp.concatenate([z[GRID_W:], jnp.where(j == nb - 1, 0.0, hn_ref[0, 0])], axis=0)
    mu = mu_ref[...]
    vert = mu[2:3] * (up - z) + mu[3:4] * (down - z)
    mixed = z + mu[0:1] * (prev - z) + mu[1:2] * (nxt - z) + jnp.where(is_ctx, 0.0, vert)

    w = RWKV_WIDTH
    r = mixed[:, 0:w]
    k = mixed[:, w:2 * w]
    v = mixed[:, 2 * w:3 * w]
    wd = mixed[:, 3 * w:3 * w + 128]
    ad = mixed[:, 3 * w + 128:4 * w]
    gdr = mixed[:, 4 * w:4 * w + 128]

    lw = w0_ref[...] + _dot(jnp.tanh(wd).astype(BF16), w2_ref[...])
    dec = jnp.exp(-jnp.exp(-_softplus(-lw) - 0.5))
    a = _sigmoid(a0_ref[...] + _dot(ad.astype(BF16), a2_ref[...]))
    ones64 = _group_ones(w, RWKV_HEAD_DIM)
    kkv = k * kkp_ref[...]
    nrm = jnp.maximum(jnp.sqrt(_dot_x_exact(kkv * kkv, ones64)), 1e-12)
    kkn = kkv / nrm
    ka = ka_ref[...]
    a_f, a_b = a[:, :w], a[:, w:]
    k_f = k * (1.0 + (a_f - 1.0) * ka)
    k_b = k * (1.0 + (a_b - 1.0) * ka)
    outs["r"][0] = r
    outs["v"][0] = v
    outs["kk"][0] = kkn
    outs["w_f"][0] = dec[:, :w]
    outs["w_b"][0] = dec[:, w:]
    outs["k_f"][0] = k_f
    outs["k_b"][0] = k_b
    outs["kka_f"][0] = kkn * a_f
    outs["kka_b"][0] = kkn * a_b
    outs["bonus"][0] = _dot_x_exact(r * (k_f + k_b) * rk_ref[...], ones64) * v
    outs["gate"][0] = _dot(_sigmoid(gdr).astype(BF16), g2_ref[...])


def _rwkv_prep(rw, mu, w0cat, w2bd, a0cat, a2bd, g2, kkp, ka, rk):
    b, t, _ = rw.shape
    nb = t // TOK_BLOCK
    hb = TOK_BLOCK // GRID_W
    rw64 = rw.reshape(b, t // GRID_W, GRID_W, RWKV_COLS)
    w = RWKV_WIDTH
    c2 = lambda i, j: (0, 0)
    return pl.pallas_call(
        functools.partial(_rwkv_prep_body, nb),
        grid=(b, nb),
        in_specs=[pl.BlockSpec((1, TOK_BLOCK, RWKV_COLS), lambda i, j: (i, j, 0)),
                  pl.BlockSpec((1, 1, GRID_W, RWKV_COLS), lambda i, j: (i, jnp.maximum(j * hb - 1, 0), 0, 0)),
                  pl.BlockSpec((1, 1, GRID_W, RWKV_COLS),
                               lambda i, j: (i, jnp.minimum(j * hb + hb, t // GRID_W - 1), 0, 0)),
                  pl.BlockSpec((4, RWKV_COLS), c2),
                  pl.BlockSpec((1, 2 * w), c2),
                  pl.BlockSpec((128, 2 * w), c2),
                  pl.BlockSpec((1, 2 * w), c2),
                  pl.BlockSpec((128, 2 * w), c2),
                  pl.BlockSpec((128, w), c2),
                  pl.BlockSpec((1, w), c2),
                  pl.BlockSpec((1, w), c2),
                  pl.BlockSpec((1, w), c2)],
        out_specs=[pl.BlockSpec((1, TOK_BLOCK, w), lambda i, j: (i, j, 0)) for _ in _RWKV_OUTS],
        out_shape=[jax.ShapeDtypeStruct((b, t, w), F32) for _ in _RWKV_OUTS],
        scratch_shapes=[pltpu.VMEM((TOK_BLOCK + 16, RWKV_COLS), F32)],
        compiler_params=_cparams(("arbitrary", "arbitrary")),
    )(rw, rw64, rw64, mu, w0cat, w2bd, a0cat, a2bd, g2, kkp, ka, rk)


def _rwkv_scan_body(nbatch, fwd_refs, bwd_refs, yf_ref, yb_ref, s_scr):
    j = pl.program_id(0)
    n = SCAN_BLOCK
    hd = RWKV_HEAD_DIM

    @pl.when(j == 0)
    def _():
        s_scr[...] = jnp.zeros_like(s_scr)

    yf_ref[...] = jnp.zeros_like(yf_ref)
    yb_ref[...] = jnp.zeros_like(yb_ref)

    ones_half = _group_ones(128, hd)
    ones2 = jnp.concatenate([ones_half, ones_half], axis=0)
    diag = (_iota((hd, 128), 1) % hd) == _iota((hd, 128), 0)
    lane_t = _iota((hd, 128), 1) % hd

    def group_sum(x):
        hi = x.astype(BF16)
        lo = (x - hi.astype(F32)).astype(BF16)
        return _dot(jnp.concatenate([hi, lo], axis=1), ones2)

    streams = [(fwd_refs, bi, False) for bi in range(nbatch)] + [(bwd_refs, bi, True) for bi in range(nbatch)]

    def group_body(g, carry):
        base_f = pl.multiple_of(g * 8, 8)
        base_b = pl.multiple_of(n - 8 - g * 8, 8)
        loaded = [[refs[ni][bi, pl.ds(base_b if rev else base_f, 8), :] for ni in range(6)]
                  for refs, bi, rev in streams]

        def rows(name_idx, r8):
            parts = []
            for si, (_, _, rev) in enumerate(streams):
                rr = 7 - r8 if rev else r8
                for p in range(2):
                    row = loaded[si][name_idx][rr:rr + 1, 128 * p:128 * (p + 1)]
                    parts.append(jnp.broadcast_to(row, (hd, 128)))
            return parts

        for r8 in range(8):
            s = s_scr[...]
            sa = -group_sum(s * jnp.concatenate(rows(2, r8), axis=0))
            vcol = group_sum(jnp.concatenate([jnp.where(diag, part, 0.0) for part in rows(1, r8)], axis=0))
            s = (s * jnp.concatenate(rows(3, r8), axis=0) + sa * jnp.concatenate(rows(5, r8), axis=0)
                 + vcol * jnp.concatenate(rows(4, r8), axis=0))
            s_scr[...] = s
            y = group_sum(s * jnp.concatenate(rows(0, r8), axis=0))
            idx = 0
            for _, bi, rev in streams:
                ti = (base_b + 7 - r8) if rev else (base_f + r8)
                oref = yb_ref if rev else yf_ref
                tile = ti // hd
                sel = lane_t == ti % hd
                for p in range(2):
                    piece = y[idx * hd:(idx + 1) * hd]
                    old = oref[bi, p, pl.ds(tile, 1)]
                    oref[bi, p, pl.ds(tile, 1)] = jnp.where(sel[None], piece[None], old)
                    idx += 1
        return carry

    lax.fori_loop(0, n // 8, group_body, 0)


def _rwkv_scan(pre):
    b, t, w = pre["r"].shape
    nbs = t // SCAN_BLOCK
    ctx_blocks = N_CTX // SCAN_BLOCK
    tiles = SCAN_BLOCK // RWKV_HEAD_DIM

    def fwd_blk(j):
        return j

    def bwd_blk(j):
        return jnp.where(j < ctx_blocks, ctx_blocks - 1 - j, nbs + ctx_blocks - 1 - j)

    in_f = pl.BlockSpec((b, SCAN_BLOCK, w), lambda j: (0, fwd_blk(j), 0))
    in_b = pl.BlockSpec((b, SCAN_BLOCK, w), lambda j: (0, bwd_blk(j), 0))
    out_f = pl.BlockSpec((b, 2, tiles, RWKV_HEAD_DIM, 128), lambda j: (0, 0, fwd_blk(j), 0, 0))
    out_b = pl.BlockSpec((b, 2, tiles, RWKV_HEAD_DIM, 128), lambda j: (0, 0, bwd_blk(j), 0, 0))
    oshape = jax.ShapeDtypeStruct((b, 2, t // RWKV_HEAD_DIM, RWKV_HEAD_DIM, 128), F32)

    def body(*refs):
        _rwkv_scan_body(b, refs[0:6], refs[6:12], refs[12], refs[13], refs[14])

    yf, yb = pl.pallas_call(
        body,
        grid=(nbs,),
        in_specs=[in_f] * 6 + [in_b] * 6,
        out_specs=[out_f, out_b],
        out_shape=[oshape, oshape],
        scratch_shapes=[pltpu.VMEM((2 * b * 2 * RWKV_HEAD_DIM, 128), F32)],
        compiler_params=_cparams(("arbitrary",)),
    )(pre["r"], pre["v"], pre["kk"], pre["w_f"], pre["k_f"], pre["kka_f"],
      pre["r"], pre["v"], pre["kk"], pre["w_b"], pre["k_b"], pre["kka_b"])

    def untile(y):
        y = y.reshape(b, 2, t // RWKV_HEAD_DIM, RWKV_HEAD_DIM, 2, RWKV_HEAD_DIM)
        return jnp.transpose(y, (0, 2, 5, 1, 4, 3)).reshape(b, t, w)

    return untile(yf), untile(yb)


def _outproj_body(x_ref, m_ref, s0_ref, s1_ref, z_ref, rf_ref, rb_ref, bonus_ref, gate_ref, g0_ref, g1_ref, og_ref,
                  snw_ref, lxw_ref, lxb_ref, gnw_ref, wo_ref, l1w_ref, l1b_ref, rtw_ref, rtb_ref,
                  x1_ref, h_ref, g_ref):
    m = m_ref[0, 0]
    ys = (s0_ref[0, 0] + s1_ref[0, 0]) * _silu(z_ref[0])
    gw = SSD_INNER // 2
    parts = []
    for g in range(2):
        part = ys[:, gw * g:gw * (g + 1)]
        parts.append(part * lax.rsqrt(jnp.mean(part * part, axis=-1, keepdims=True) + 1e-5))
    y_ssd = jnp.concatenate(parts, axis=1) * snw_ref[...]
    ones64 = _group_ones(RWKV_WIDTH, RWKV_HEAD_DIM)
    inv = 1.0 / RWKV_HEAD_DIM
    yr = rf_ref[0] + rb_ref[0]
    mu = _dot_x_exact(yr, ones64) * inv
    yc = yr - mu
    var = _dot_x_exact(yc * yc, ones64) * inv
    yn = yc * lax.rsqrt(var + RWKV_LNX_EPS) * lxw_ref[...] + lxb_ref[...]
    y_rwkv = (yn + bonus_ref[0]) * gate_ref[0]
    o = g0_ref[0, 0] + g1_ref[0, 0]
    ms = _dot_x_exact(o * o, ones64) * (1.0 / GLA_VAL_DIM)
    y_gla = o * lax.rsqrt(ms + 1e-5) * gnw_ref[...] * _silu(og_ref[0])

    y_mix = jnp.concatenate([y_ssd, y_rwkv, y_gla], axis=1).astype(BF16)
    mix = _dot(y_mix, wo_ref[...])
    x1 = _layernorm(DN_ALPHA * x_ref[0] + m[2:3] * mix) * l1w_ref[...] + l1b_ref[...]
    x1_ref[0] = x1
    h = _layernorm(x1) * (1.0 + m[4:5]) + m[3:4]
    h_ref[0] = h.astype(BF16)
    g_ref[0] = _route(h, rtw_ref[...], rtb_ref[...])


def _outproj(xs, msel, ssd_y, z, rf, rb, bonus, gate, gla_o, og, snw, lxw, lxb, gnw, wo, l1w, l1b, rtw, rtb):
    b, t, d = xs.shape
    nb = t // TOK_BLOCK
    tok = lambda w: pl.BlockSpec((1, TOK_BLOCK, w), lambda i, j: (i, j, 0))
    dirspec = lambda w, dd: pl.BlockSpec((1, 1, TOK_BLOCK, w), lambda i, j: (dd, i, j, 0))
    c2 = lambda i, j: (0, 0)
    row = lambda w: pl.BlockSpec((1, w), c2)
    return pl.pallas_call(
        _outproj_body,
        grid=(b, nb),
        in_specs=[tok(d),
                  pl.BlockSpec((1, 1, N_MOD, d), lambda i, j: (i, jnp.minimum(j, 1), 0, 0)),
                  dirspec(SSD_INNER, 0), dirspec(SSD_INNER, 1), tok(SSD_INNER),
                  tok(RWKV_WIDTH), tok(RWKV_WIDTH), tok(RWKV_WIDTH), tok(RWKV_WIDTH),
                  dirspec(GLA_V_WIDTH, 0), dirspec(GLA_V_WIDTH, 1), tok(GLA_V_WIDTH),
                  row(SSD_INNER), row(RWKV_WIDTH), row(RWKV_WIDTH), row(GLA_V_WIDTH),
                  pl.BlockSpec((d, d), c2), row(d), row(d),
                  pl.BlockSpec((d, 128), c2), row(128)],
        out_specs=[tok(d), tok(d), tok(128)],
        out_shape=[jax.ShapeDtypeStruct((b, t, d), F32), jax.ShapeDtypeStruct((b, t, d), BF16),
                   jax.ShapeDtypeStruct((b, t, 128), F32)],
        compiler_params=_cparams(("arbitrary", "arbitrary")),
    )(xs, msel, ssd_y, ssd_y, z, rf, rb, bonus, gate, gla_o, gla_o, og, snw, lxw, lxb, gnw, wo, l1w, l1b, rtw, rtb)


def _route(h, rw, rb):
    tm = h.shape[0]
    logits = _dot_hp(h, rw)
    lane = _iota((tm, 128), 1)
    valid = lane < N_EXPERTS
    scores = _sigmoid(logits)
    neg = -jnp.inf
    sel = jnp.where(valid, scores + rb, neg)
    per_group = N_EXPERTS // N_GROUPS
    grp = lane // per_group

    def take_max(cur):
        mx = jnp.max(cur, axis=-1, keepdims=True)
        first = jnp.min(jnp.where(cur == mx, lane, 1 << 20), axis=-1, keepdims=True)
        hit = lane == first
        return mx, hit

    gscore = jnp.full((tm, 128), neg, F32)
    for g in range(N_GROUPS):
        cur = jnp.where(grp == g, sel, neg)
        m1, hit = take_max(cur)
        m2 = jnp.max(jnp.where(hit, neg, cur), axis=-1, keepdims=True)
        gscore = jnp.where(lane == g, m1 + m2, gscore)
    allowed = jnp.zeros((tm, 128), jnp.bool_)
    cur = gscore
    for _ in range(TOPK_GROUPS):
        _, hit = take_max(cur)
        cur = jnp.where(hit, neg, cur)
        gidx = jnp.min(jnp.where(hit, lane, 1 << 20), axis=-1, keepdims=True)
        allowed = allowed | (grp == gidx)
    cur = jnp.where(allowed & valid, sel, neg)
    chosen = jnp.zeros((tm, 128), jnp.bool_)
    for _ in range(TOP_K):
        _, hit = take_max(cur)
        cur = jnp.where(hit, neg, cur)
        chosen = chosen | hit
    wts = jnp.where(chosen, scores, 0.0)
    return wts / jnp.sum(wts, axis=-1, keepdims=True) * ROUTED_SCALE


def _moe_body(h_ref, g_ref, w1_ref, w3_ref, w2_ref, o_ref):
    e0 = pl.program_id(1) * EXPERTS_PER_STEP

    @pl.when(pl.program_id(1) == 0)
    def _():
        o_ref[...] = jnp.zeros_like(o_ref)

    hb = h_ref[...]
    gates = g_ref[...]
    lane = _iota(gates.shape, 1)
    hids = []
    for e in range(EXPERTS_PER_STEP):
        gcol = jnp.sum(jnp.where(lane == e0 + e, gates, 0.0), axis=-1, keepdims=True)
        hid = _silu(_dot(hb, w1_ref[e])) * _dot(hb, w3_ref[e])
        hids.append((hid * gcol).astype(BF16))
    o_ref[...] += _dot(jnp.concatenate(hids, axis=1), w2_ref[0])


def _moe(h2, gates, w1, w3, w2g):
    n, d = h2.shape
    tm = n // 8 if n % (8 * 128) == 0 else 1024
    ne, _, f = w1.shape
    eps = EXPERTS_PER_STEP
    return pl.pallas_call(
        _moe_body,
        grid=(n // tm, ne // eps),
        in_specs=[pl.BlockSpec((tm, d), lambda i, g: (i, 0)),
                  pl.BlockSpec((tm, 128), lambda i, g: (i, 0)),
                  pl.BlockSpec((eps, d, f), lambda i, g: (g, 0, 0)),
                  pl.BlockSpec((eps, d, f), lambda i, g: (g, 0, 0)),
                  pl.BlockSpec((1, eps * f, d), lambda i, g: (g, 0, 0))],
        out_specs=pl.BlockSpec((tm, d), lambda i, g: (i, 0)),
        out_shape=jax.ShapeDtypeStruct((n, d), F32),
        compiler_params=_cparams(("arbitrary", "arbitrary")),
    )(h2, gates, w1, w3, w2g)


def _final_body(x1_ref, h_ref, rt_ref, m_ref, s13_ref, s2_ref, l2w_ref, l2b_ref, o_ref):
    m = m_ref[0, 0]
    a = _dot(h_ref[0], s13_ref[...])
    fs = a.shape[1] // 2
    hid = (_silu(a[:, :fs]) * a[:, fs:]).astype(BF16)
    f = rt_ref[0] + _dot(hid, s2_ref[...])
    o_ref[0] = _layernorm(DN_ALPHA * x1_ref[0] + m[5:6] * f) * l2w_ref[...] + l2b_ref[...]


def _final(x1, h, routed, msel, s13, s2, l2w, l2b):
    b, t, d = x1.shape
    nb = t // TOK_BLOCK
    tok = pl.BlockSpec((1, TOK_BLOCK, d), lambda i, j: (i, j, 0))
    c2 = lambda i, j: (0, 0)
    return pl.pallas_call(
        _final_body,
        grid=(b, nb),
        in_specs=[tok, tok, tok,
                  pl.BlockSpec((1, 1, N_MOD, d), lambda i, j: (i, jnp.minimum(j, 1), 0, 0)),
                  pl.BlockSpec(s13.shape, c2), pl.BlockSpec(s2.shape, c2),
                  pl.BlockSpec((1, d), c2), pl.BlockSpec((1, d), c2)],
        out_specs=tok,
        out_shape=jax.ShapeDtypeStruct((b, t, d), F32),
        compiler_params=_cparams(("arbitrary", "arbitrary")),
    )(x1, h, routed, msel, s13, s2, l2w, l2b)


def _prep_w_in(w):
    z, xbc, dt, rw, gq, gk, gv, gd, og = jnp.split(
        w, np.cumsum([512, 1024, 16, 1152, 128, 128, 256, 32, 256])[:-1].tolist(), axis=1)
    small = jnp.concatenate([dt, gd, jnp.zeros((w.shape[0], 128 - 48), w.dtype)], axis=1)
    return jnp.concatenate([z, xbc, rw, gq, gk, gv, og, small], axis=1).astype(BF16)


def _block_diag2(m):
    r, c = m.shape[1:]
    zero = jnp.zeros((r, c), m.dtype)
    return jnp.concatenate([jnp.concatenate([m[0], zero], axis=1),
                            jnp.concatenate([zero, m[1]], axis=1)], axis=0)


def _pad_lanes(v, n=128):
    v = v.reshape(1, -1)
    return jnp.pad(v, ((0, 0), (0, n - v.shape[1])))


def _token_mixer(xs, msel, l, p):
    z, xbc, rw, gq, gk, gv, og, small = _inproj(xs, msel, _prep_w_in(p["w_in"][l]))
    ssd_y = _ssd(xbc, small, p["ssd_conv_w"][l], p["ssd_conv_b"][l].reshape(1, -1),
                 _pad_lanes(p["ssd_dt_bias"][l]), _pad_lanes(p["ssd_a_log"][l]),
                 jnp.repeat(p["ssd_d"][l], SSD_HEAD_DIM).reshape(1, -1))
    gu_pad = jnp.pad(p["gla_gu"][l], ((0, 0), (0, 128 - GLA_GATE_LORA), (0, 0)))
    gla_o = _gla(gq, gk, gv, small, gu_pad, p["gla_gb"][l])
    pre = dict(zip(_RWKV_OUTS, _rwkv_prep(
        rw, p["rwkv_mu"][l], p["rwkv_w0"][l].reshape(1, -1), _block_diag2(p["rwkv_w2"][l]).astype(BF16),
        p["rwkv_a0"][l].reshape(1, -1), _block_diag2(p["rwkv_a2"][l]).astype(BF16),
        p["rwkv_g2"][l].astype(BF16), p["rwkv_kk"][l].reshape(1, -1), p["rwkv_ka"][l].reshape(1, -1),
        p["rwkv_rk"][l].reshape(1, -1))))
    rf, rb = _rwkv_scan(pre)
    return z, og, ssd_y, gla_o, rf, rb, pre["bonus"], pre["gate"]


def kernel(x, c, ctx, c_ctx, ada_w, ada_b, w_in, ssd_conv_w, ssd_conv_b, ssd_dt_bias, ssd_a_log, ssd_d, ssd_norm_w, rwkv_mu, rwkv_w0, rwkv_w2, rwkv_a0, rwkv_a2, rwkv_g2, rwkv_kk, rwkv_ka, rwkv_rk, rwkv_lnx_w, rwkv_lnx_b, gla_gu, gla_gb, gla_norm_w, w_out, ln1_w, ln1_b, ln2_w, ln2_b, router_w, router_b, exp_w1, exp_w3, exp_w2, sh_w1, sh_w3, sh_w2):
    p = dict(w_in=w_in, ssd_conv_w=ssd_conv_w, ssd_conv_b=ssd_conv_b, ssd_dt_bias=ssd_dt_bias, ssd_a_log=ssd_a_log,
             ssd_d=ssd_d, rwkv_mu=rwkv_mu, rwkv_w0=rwkv_w0, rwkv_w2=rwkv_w2, rwkv_a0=rwkv_a0, rwkv_a2=rwkv_a2,
             rwkv_g2=rwkv_g2, rwkv_kk=rwkv_kk, rwkv_ka=rwkv_ka, rwkv_rk=rwkv_rk, gla_gu=gla_gu, gla_gb=gla_gb)
    bsz, seq, d = x.shape
    n_ctx = ctx.shape[1]
    assert n_ctx == N_CTX and d == D_MODEL and seq % TOK_BLOCK == 0 and seq // GRID_W >= 8
    depth = ada_w.shape[0]
    row = lambda a: a.reshape(1, -1)

    cond = jnp.concatenate([c, c_ctx[None], jnp.zeros((8 - bsz - 1, d), F32)], axis=0)
    mods = _ada_all(cond, ada_w, ada_b).reshape(depth, 8, N_MOD, d)
    xs = jnp.concatenate([ctx, x], axis=1)
    t = xs.shape[1]
    for l in range(depth):
        msel = jnp.stack([jnp.broadcast_to(mods[l, bsz], (bsz, N_MOD, d)), mods[l, :bsz]], axis=1)
        z, og, ssd_y, gla_o, rf, rb, bonus, gate = _token_mixer(xs, msel, l, p)
        x1, h, gates = _outproj(xs, msel, ssd_y, z, rf, rb, bonus, gate, gla_o, og,
                                row(ssd_norm_w[l]), row(rwkv_lnx_w[l]), row(rwkv_lnx_b[l]), row(gla_norm_w[l]),
                                w_out[l].astype(BF16), row(ln1_w[l]), row(ln1_b[l]),
                                jnp.pad(router_w[l], ((0, 0), (0, 128 - N_EXPERTS))), _pad_lanes(router_b[l]))
        routed = _moe(h.reshape(bsz * t, d), gates.reshape(bsz * t, 128),
                      exp_w1[l].astype(BF16), exp_w3[l].astype(BF16),
                      exp_w2[l].astype(BF16).reshape(N_EXPERTS // EXPERTS_PER_STEP, -1, d))
        s13 = jnp.concatenate([sh_w1[l], sh_w3[l]], axis=1).astype(BF16)
        xs = _final(x1, h, routed.reshape(bsz, t, d), msel, s13, sh_w2[l].astype(BF16), row(ln2_w[l]), row(ln2_b[l]))
    return xs[:, n_ctx:]
```

```python
import functools

import jax
import jax.numpy as jnp
import numpy as np
from jax import lax
from jax.experimental import pallas as pl
from jax.experimental.pallas import tpu as pltpu

F32 = jnp.float32
BF16 = jnp.bfloat16

D_MODEL = 1024
N_CTX = 256
GRID_W = 64
N_MOD = 6
LN_EPS = 1e-5
DEPTH = 4
DN_ALPHA = (2 * DEPTH) ** 0.25

SSD_HEADS = 8
SSD_HEAD_DIM = 64
SSD_INNER = 512
SSD_STATE = 128
SSD_XBC = 1024

RWKV_WIDTH = 256
RWKV_HEAD_DIM = 64
RWKV_COLS = 1152
RWKV_LNX_EPS = 64e-5

GLA_K_WIDTH = 128
GLA_V_WIDTH = 256
GLA_KEY_DIM = 32
GLA_VAL_DIM = 64
GLA_GATE_LORA = 16
GLA_GATE_NORMALIZER = 16.0
GLA_SUB = 64

N_EXPERTS = 64
TOP_K = 8
N_GROUPS = 8
TOPK_GROUPS = 4
EXPERT_DIM = 256
ROUTED_SCALE = 2.5
EXPERTS_PER_STEP = 4

TOK_BLOCK = 256
SCAN_BLOCK = 64
MOE_TOK = 1024
V7X_VMEM_LIMIT = 56 * 1024 * 1024

_IN_PIECES = (("z", 512), ("xbc", 1024), ("rw", 1152), ("gq", 128), ("gk", 128), ("gv", 256), ("og", 256),
              ("small", 128))
IN_PAD = sum(w for _, w in _IN_PIECES)


def _cparams(sem):
    return pltpu.CompilerParams(dimension_semantics=sem, vmem_limit_bytes=V7X_VMEM_LIMIT)


def _split3(a):
    hi = a.astype(BF16)
    r1 = a - hi.astype(F32)
    mid = r1.astype(BF16)
    lo = (r1 - mid.astype(F32)).astype(BF16)
    return hi, mid, lo


def _dot(a, b):
    return jnp.dot(a, b, preferred_element_type=F32)


def _dot_nt(a, b):
    return lax.dot_general(a, b, (((1,), (1,)), ((), ())), preferred_element_type=F32)


def _dot_x_exact(a, e):
    hi, mid, lo = _split3(a)
    return _dot(hi, e) + (_dot(mid, e) + _dot(lo, e))


def _dot_exact_x(e, a):
    hi, mid, lo = _split3(a)
    return _dot(e, hi) + (_dot(e, mid) + _dot(e, lo))


def _dot_hp(a, b):
    ah, am, _ = _split3(a)
    bh, bm, _ = _split3(b)
    return _dot(ah, bh) + (_dot(ah, bm) + _dot(am, bh))


def _sigmoid(x):
    return 1.0 / (1.0 + jnp.exp(-x))


def _silu(x):
    return x * _sigmoid(x)


def _softplus(x):
    return jnp.maximum(x, 0.0) + jnp.log(1.0 + jnp.exp(-jnp.abs(x)))


def _layernorm(x):
    mu = jnp.mean(x, axis=-1, keepdims=True)
    xc = x - mu
    var = jnp.mean(xc * xc, axis=-1, keepdims=True)
    return xc * lax.rsqrt(var + LN_EPS)


def _iota(shape, dim):
    return lax.broadcasted_iota(jnp.int32, shape, dim)


def _tri(n, upper):
    r = _iota((n, n), 0)
    c = _iota((n, n), 1)
    return jnp.where(upper, c - r, r - c) >= 0


def _block_of(d, j, nb):
    return jnp.where(d == 0, j, jnp.where(j == 0, 0, nb - j))


def _group_ones(n, g):
    r = _iota((n, n), 0) // g
    c = _iota((n, n), 1) // g
    return (r == c).astype(BF16)


def _ada_body(c_ref, w_ref, b_ref, o_ref):
    a = _silu(c_ref[...]).astype(BF16)
    o_ref[0] = _dot(a, w_ref[0].astype(BF16)) + b_ref[0]


def _ada_all(cond, ada_w, ada_b):
    nl, d, n = ada_w.shape
    tn = 1536
    rows = cond.shape[0]
    return pl.pallas_call(
        _ada_body,
        grid=(nl, n // tn),
        in_specs=[pl.BlockSpec((rows, d), lambda l, k: (0, 0)),
                  pl.BlockSpec((1, d, tn), lambda l, k: (l, 0, k)),
                  pl.BlockSpec((1, 1, tn), lambda l, k: (l, 0, k))],
        out_specs=pl.BlockSpec((1, rows, tn), lambda l, k: (l, 0, k)),
        out_shape=jax.ShapeDtypeStruct((nl, rows, n), F32),
        compiler_params=_cparams(("arbitrary", "arbitrary")),
        name="ada_mod",
    )(cond, ada_w, ada_b.reshape(nl, 1, n))


def _inproj_body(x_ref, m_ref, w_ref, *out_refs):
    x = x_ref[0]
    m = m_ref[0, 0]
    u = _layernorm(x) * (1.0 + m[1:2]) + m[0:1]
    y = _dot(u.astype(BF16), w_ref[...])
    off = 0
    for (_, wdt), o in zip(_IN_PIECES, out_refs):
        o[0] = y[:, off:off + wdt]
        off += wdt


def _inproj(xs, msel, w_in_p):
    b, t, d = xs.shape
    nb = t // TOK_BLOCK
    out_shape = [jax.ShapeDtypeStruct((b, t, w), F32) for _, w in _IN_PIECES]
    out_specs = [pl.BlockSpec((1, TOK_BLOCK, w), lambda i, j: (i, j, 0)) for _, w in _IN_PIECES]
    return pl.pallas_call(
        _inproj_body,
        grid=(b, nb),
        in_specs=[pl.BlockSpec((1, TOK_BLOCK, d), lambda i, j: (i, j, 0)),
                  pl.BlockSpec((1, 1, N_MOD, d), lambda i, j: (i, jnp.minimum(j, 1), 0, 0)),
                  pl.BlockSpec((d, IN_PAD), lambda i, j: (0, 0))],
        out_specs=out_specs,
        out_shape=out_shape,
        compiler_params=_cparams(("arbitrary", "arbitrary")),
        name="in_proj",
    )(xs, msel, w_in_p)


def _ssd_body(nb, xbc_ref, hp_ref, hn_ref, small_ref, cw_ref, cb_ref, dtb_ref, alog_ref, dsk_ref,
              y_ref, h_scr, buf_scr):
    q = TOK_BLOCK
    d = pl.program_id(1)
    j = pl.program_id(2)
    blk = _block_of(d, j, nb)

    @pl.when(j == 0)
    def _():
        h_scr[...] = jnp.zeros_like(h_scr)

    seg_first = (blk == 0) | (blk == 1)
    seg_last = (blk == 0) | (blk == nb - 1)
    cur = xbc_ref[0]
    buf_scr[8:8 + q] = cur
    buf_scr[0:8] = jnp.where(seg_first, 0.0, hp_ref[0])
    buf_scr[8 + q:16 + q] = jnp.where(seg_last, 0.0, hn_ref[0])
    prev = buf_scr[7:7 + q]
    nxt = buf_scr[9:9 + q]
    cw = cw_ref[0]
    act = _silu(prev * cw[0:1] + cur * cw[1:2] + nxt * cw[2:3] + cb_ref[...])
    xs = act[:, :SSD_INNER]

    upper = d == 1
    tri_b = _tri(q, upper)
    tri = tri_b.astype(BF16)
    tri_t = _tri(q, jnp.logical_not(upper)).astype(BF16)

    lane = _iota((1, 128), 1)
    dt_all = _softplus(small_ref[0] + dtb_ref[...])
    neg_a = jnp.where(lane < 2 * SSD_HEADS, -jnp.exp(alog_ref[...]), 0.0)
    la_all = dt_all * neg_a
    er = _iota((128, SSD_INNER), 0)
    ec = _iota((128, SSD_INNER), 1)
    e_exp = (er == SSD_HEADS * d + ec // SSD_HEAD_DIM).astype(BF16)
    er8 = _iota((128, 128), 0)
    ec8 = _iota((128, 128), 1)
    e8 = ((er8 == SSD_HEADS * d + ec8) & (ec8 < SSD_HEADS)).astype(BF16)

    dt_exp = _dot_x_exact(dt_all, e_exp)
    la_exp = _dot_x_exact(la_all, e_exp)
    la8 = _dot_x_exact(la_all, e8)
    cs_exp = _dot_exact_x(tri, la_exp)
    cs8 = _dot_exact_x(tri, la8)
    cs_t = _dot_x_exact(la8.T, tri_t)
    tot_exp = jnp.sum(la_exp, axis=0, keepdims=True)

    xdt = xs * dt_exp
    lane_q = _iota((q, 128), 1)
    ys = []
    for p in range(SSD_HEADS // 2):
        xdt_p = xdt[:, 128 * p:128 * (p + 1)].astype(BF16)
        halves = []
        for h in (2 * p, 2 * p + 1):
            g = h // (SSD_HEADS // 2)
            c_g = act[:, 768 + 128 * g:896 + 128 * g].astype(BF16)
            b_g = act[:, 512 + 128 * g:640 + 128 * g].astype(BF16)
            gmat = _dot_nt(c_g, b_g)
            seg = cs8[:, h:h + 1] - cs_t[h:h + 1, :]
            lmat = jnp.where(tri_b, jnp.exp(jnp.minimum(seg, 0.0)), 0.0)
            halves.append(_dot((gmat * lmat).astype(BF16), xdt_p))
        ys.append(jnp.where(lane_q < SSD_HEAD_DIM, halves[0], halves[1]))
    y = jnp.concatenate(ys, axis=1)

    dec = jnp.exp(tot_exp - cs_exp)
    xd = (xdt * dec).astype(BF16)
    offs = []
    for g in range(2):
        c_g = act[:, 768 + 128 * g:896 + 128 * g].astype(BF16)
        b_g = act[:, 512 + 128 * g:640 + 128 * g]
        hg = h_scr[g]
        offs.append(_dot(c_g, hg.astype(BF16)))
        hn = _dot(b_g.T.astype(BF16), xd[:, 256 * g:256 * (g + 1)])
        h_scr[g] = hg * jnp.exp(tot_exp[:, 256 * g:256 * (g + 1)]) + hn
    y = y + jnp.concatenate(offs, axis=1) * jnp.exp(cs_exp)
    y = y + jnp.where(d == 0, dsk_ref[...], 0.0) * xs
    y_ref[0, 0] = y


def _ssd(xbc, small, conv_w, conv_b, dtb_pad, alog_pad, dskip_exp):
    b, t, _ = xbc.shape
    nb = t // TOK_BLOCK
    r8 = TOK_BLOCK // 8

    def main_map(i, d, j):
        return (i, _block_of(d, j, nb), 0)

    def prev_map(i, d, j):
        return (i, jnp.maximum(_block_of(d, j, nb) * r8 - 1, 0), 0)

    def next_map(i, d, j):
        return (i, jnp.minimum(_block_of(d, j, nb) * r8 + r8, t // 8 - 1), 0)

    def out_map(i, d, j):
        return (d, i, _block_of(d, j, nb), 0)

    const2 = lambda i, d, j: (0, 0)
    return pl.pallas_call(
        functools.partial(_ssd_body, nb),
        grid=(b, 2, nb),
        in_specs=[pl.BlockSpec((1, TOK_BLOCK, SSD_XBC), main_map),
                  pl.BlockSpec((1, 8, SSD_XBC), prev_map),
                  pl.BlockSpec((1, 8, SSD_XBC), next_map),
                  pl.BlockSpec((1, TOK_BLOCK, 128), main_map),
                  pl.BlockSpec((1, 3, SSD_XBC), lambda i, d, j: (0, 0, 0)),
                  pl.BlockSpec((1, SSD_XBC), const2),
                  pl.BlockSpec((1, 128), const2),
                  pl.BlockSpec((1, 128), const2),
                  pl.BlockSpec((1, SSD_INNER), const2)],
        out_specs=pl.BlockSpec((1, 1, TOK_BLOCK, SSD_INNER), out_map),
        out_shape=jax.ShapeDtypeStruct((2, b, t, SSD_INNER), F32),
        scratch_shapes=[pltpu.VMEM((2, SSD_STATE, 256), F32),
                        pltpu.VMEM((TOK_BLOCK + 16, SSD_XBC), F32)],
        compiler_params=_cparams(("arbitrary", "arbitrary", "arbitrary")),
        name="ssd",
    )(xbc, xbc, xbc, small, conv_w[None], conv_b, dtb_pad, alog_pad, dskip_exp)


def _gla_body(nb, q_ref, k_ref, v_ref, small_ref, gu_ref, gb_ref, o_ref, s_scr, la_scr):
    d = pl.program_id(1)
    j = pl.program_id(2)
    c = GLA_SUB
    nsub = TOK_BLOCK // c

    @pl.when(j == 0)
    def _():
        s_scr[...] = jnp.zeros_like(s_scr)

    upper = d == 1
    tri_b = _tri(c, upper)
    tri = tri_b.astype(BF16)

    sm = small_ref[0]
    lr = _iota((128, 128), 0)
    lc = _iota((128, 128), 1)
    gsel = ((lr == 2 * SSD_HEADS + GLA_GATE_LORA * d + lc) & (lc < GLA_GATE_LORA)).astype(BF16)
    gd = _dot_x_exact(sm, gsel)
    gu = jnp.where(upper, gu_ref[1], gu_ref[0])
    gbias = jnp.where(upper, gb_ref[1:2], gb_ref[0:1])
    pre = _dot_hp(gd, gu) + gbias
    la_scr[...] = -_softplus(-pre) * (1.0 / GLA_GATE_NORMALIZER)

    lane_k = _iota((c, GLA_K_WIDTH), 1) // GLA_KEY_DIM
    lane_v = _iota((c, GLA_V_WIDTH), 1) // GLA_VAL_DIM
    st_mask = (_iota((GLA_V_WIDTH, GLA_K_WIDTH), 0) // GLA_VAL_DIM
               == _iota((GLA_V_WIDTH, GLA_K_WIDTH), 1) // GLA_KEY_DIM)

    for si in range(nsub):
        lo = pl.multiple_of(jnp.where(upper, (nsub - 1 - si) * c, si * c), c)
        rows = pl.ds(lo, c)
        la = la_scr[rows, :]
        qq = q_ref[0, rows, :] * (GLA_KEY_DIM ** -0.5)
        kk = k_ref[0, rows, :]
        vv = v_ref[0, rows, :]
        cs = _dot_exact_x(tri, la)
        tot = jnp.sum(la, axis=0, keepdims=True)
        qt = qq * jnp.exp(cs)
        kt = (kk * jnp.exp(jnp.minimum(-cs, 80.0))).astype(BF16)
        kdec = (kk * jnp.exp(tot - cs)).astype(BF16)
        st = s_scr[...]
        o = _dot_nt(qt.astype(BF16), st.astype(BF16))
        vb = vv.astype(BF16)
        for h in range(GLA_K_WIDTH // GLA_KEY_DIM):
            qh = jnp.where(lane_k == h, qt, 0.0).astype(BF16)
            a = jnp.where(tri_b, _dot_nt(qh, kt), 0.0)
            o = o + jnp.where(lane_v == h, _dot(a.astype(BF16), vb), 0.0)
        upd = _dot(vv.T.astype(BF16), kdec)
        s_scr[...] = st * jnp.exp(tot) + jnp.where(st_mask, upd, 0.0)
        o_ref[0, 0, rows, :] = o


def _gla(gq, gk, gv, small, gu_pad, gb):
    b, t, _ = gq.shape
    nb = t // TOK_BLOCK

    def main_map(i, d, j):
        return (i, _block_of(d, j, nb), 0)

    def out_map(i, d, j):
        return (d, i, _block_of(d, j, nb), 0)

    return pl.pallas_call(
        functools.partial(_gla_body, nb),
        grid=(b, 2, nb),
        in_specs=[pl.BlockSpec((1, TOK_BLOCK, GLA_K_WIDTH), main_map),
                  pl.BlockSpec((1, TOK_BLOCK, GLA_K_WIDTH), main_map),
                  pl.BlockSpec((1, TOK_BLOCK, GLA_V_WIDTH), main_map),
                  pl.BlockSpec((1, TOK_BLOCK, 128), main_map),
                  pl.BlockSpec((2, 128, GLA_K_WIDTH), lambda i, d, j: (0, 0, 0)),
                  pl.BlockSpec((2, GLA_K_WIDTH), lambda i, d, j: (0, 0))],
        out_specs=pl.BlockSpec((1, 1, TOK_BLOCK, GLA_V_WIDTH), out_map),
        out_shape=jax.ShapeDtypeStruct((2, b, t, GLA_V_WIDTH), F32),
        scratch_shapes=[pltpu.VMEM((GLA_V_WIDTH, GLA_K_WIDTH), F32),
                        pltpu.VMEM((TOK_BLOCK, GLA_K_WIDTH), F32)],
        compiler_params=_cparams(("arbitrary", "arbitrary", "arbitrary")),
        name="gla",
    )(gq, gk, gv, small, gu_pad, gb)


_RWKV_OUTS = ("r", "v", "kk", "lw_f", "k_f", "kka_f", "lw_b", "k_b", "kka_b", "bonus", "gate")


def _rwkv_prep_body(nb, rw_ref, hp_ref, hn_ref, mu_ref, w0_ref, w2_ref, a0_ref, a2_ref, g2_ref, kkp_ref, ka_ref,
                    rk_ref, *refs):
    outs = dict(zip(_RWKV_OUTS, refs[:len(_RWKV_OUTS)]))
    buf_scr = refs[len(_RWKV_OUTS)]
    q = TOK_BLOCK
    j = pl.program_id(1)
    is_ctx = j == 0
    z = rw_ref[0]
    zeros8 = jnp.zeros((8, RWKV_COLS), F32)
    buf_scr[0:8] = zeros8
    buf_scr[8 + q:16 + q] = zeros8
    buf_scr[8:8 + q] = z
    t = _iota((q, 1), 0)
    col = t % GRID_W
    prev = jnp.where(jnp.where(is_ctx, t, col) == 0, 0.0, buf_scr[7:7 + q])
    nxt = jnp.where(jnp.where(is_ctx, t - (q - 1), col - (GRID_W - 1)) == 0, 0.0, buf_scr[9:9 + q])
    up = jnp.concatenate([jnp.where(j == 1, 0.0, hp_ref[0]), z[:q - GRID_W]], axis=0)
    down = jnp.concatenate([z[GRID_W:], jnp.where(j == nb - 1, 0.0, hn_ref[0])], axis=0)
    mu = mu_ref[...]
    vert = mu[2:3] * (up - z) + mu[3:4] * (down - z)
    mixed = z + mu[0:1] * (prev - z) + mu[1:2] * (nxt - z) + jnp.where(is_ctx, 0.0, vert)

    w = RWKV_WIDTH
    r = mixed[:, 0:w]
    k = mixed[:, w:2 * w]
    v = mixed[:, 2 * w:3 * w]
    wd = mixed[:, 3 * w:3 * w + 128]
    ad = mixed[:, 3 * w + 128:4 * w]
    gdr = mixed[:, 4 * w:4 * w + 128]

    lw = w0_ref[...] + _dot(jnp.tanh(wd).astype(BF16), w2_ref[...])
    log_dec = -jnp.exp(-_softplus(-lw) - 0.5)
    a = _sigmoid(a0_ref[...] + _dot(ad.astype(BF16), a2_ref[...]))
    ones64 = _group_ones(w, RWKV_HEAD_DIM)
    kkv = k * kkp_ref[...]
    nrm = jnp.maximum(jnp.sqrt(_dot_x_exact(kkv * kkv, ones64)), 1e-12)
    kkn = kkv / nrm
    ka = ka_ref[...]
    a_f, a_b = a[:, :w], a[:, w:]
    k_f = k * (1.0 + (a_f - 1.0) * ka)
    k_b = k * (1.0 + (a_b - 1.0) * ka)
    outs["r"][0] = r
    outs["v"][0] = v
    outs["kk"][0] = kkn
    outs["lw_f"][0] = log_dec[:, :w]
    outs["lw_b"][0] = log_dec[:, w:]
    outs["k_f"][0] = k_f
    outs["k_b"][0] = k_b
    outs["kka_f"][0] = kkn * a_f
    outs["kka_b"][0] = kkn * a_b
    outs["bonus"][0] = _dot_x_exact(r * (k_f + k_b) * rk_ref[...], ones64) * v
    outs["gate"][0] = _dot(_sigmoid(gdr).astype(BF16), g2_ref[...])


def _rwkv_prep(rw, mu, w0cat, w2bd, a0cat, a2bd, g2, kkp, ka, rk):
    b, t, _ = rw.shape
    nb = t // TOK_BLOCK
    hb = TOK_BLOCK // GRID_W
    w = RWKV_WIDTH
    c2 = lambda i, j: (0, 0)
    return pl.pallas_call(
        functools.partial(_rwkv_prep_body, nb),
        grid=(b, nb),
        in_specs=[pl.BlockSpec((1, TOK_BLOCK, RWKV_COLS), lambda i, j: (i, j, 0)),
                  pl.BlockSpec((1, GRID_W, RWKV_COLS), lambda i, j: (i, jnp.maximum(j * hb - 1, 0), 0)),
                  pl.BlockSpec((1, GRID_W, RWKV_COLS),
                               lambda i, j: (i, jnp.minimum(j * hb + hb, t // GRID_W - 1), 0)),
                  pl.BlockSpec((4, RWKV_COLS), c2),
                  pl.BlockSpec((1, 2 * w), c2),
                  pl.BlockSpec((128, 2 * w), c2),
                  pl.BlockSpec((1, 2 * w), c2),
                  pl.BlockSpec((128, 2 * w), c2),
                  pl.BlockSpec((128, w), c2),
                  pl.BlockSpec((1, w), c2),
                  pl.BlockSpec((1, w), c2),
                  pl.BlockSpec((1, w), c2)],
        out_specs=[pl.BlockSpec((1, TOK_BLOCK, w), lambda i, j: (i, j, 0)) for _ in _RWKV_OUTS],
        out_shape=[jax.ShapeDtypeStruct((b, t, w), F32) for _ in _RWKV_OUTS],
        scratch_shapes=[pltpu.VMEM((TOK_BLOCK + 16, RWKV_COLS), F32)],
        compiler_params=_cparams(("arbitrary", "arbitrary")),
        name="rwkv_prep",
    )(rw, rw, rw, mu, w0cat, w2bd, a0cat, a2bd, g2, kkp, ka, rk)


def _rwkv_scan_body(nbatch, fwd_refs, bwd_refs, yf_ref, yb_ref, s_scr):
    j = pl.program_id(0)
    n = SCAN_BLOCK
    hd = RWKV_HEAD_DIM

    @pl.when(j == 0)
    def _():
        s_scr[...] = jnp.zeros_like(s_scr)

    nh = RWKV_WIDTH // hd
    w = RWKV_WIDTH
    lane_head = _iota((1, w), 1) // hd
    row_h = _iota((nh * n, nh * n), 0) // n
    row_c = _iota((nh * n, nh * n), 0) % n
    col_h = _iota((nh * n, nh * n), 1) // n
    col_c = _iota((nh * n, nh * n), 1) % n
    cat_c = _iota((n, nh * n), 0)
    cat_cc = _iota((n, nh * n), 1) % n
    eye = (_iota((nh * n, nh * n), 0) == _iota((nh * n, nh * n), 1)).astype(F32)

    def stack(x):
        return jnp.concatenate([jnp.where(lane_head == h, x, 0.0) for h in range(nh)], axis=0)

    def chunk(refs, bi, upper, s_prev):
        r, v, kk, lw, k, kka = (ref[bi] for ref in refs)
        tri = _tri(n, upper).astype(BF16)
        before = (col_c > row_c) if upper else (col_c < row_c)
        strict = (row_h == col_h) & before
        incl = (cat_cc >= cat_c) if upper else (cat_cc <= cat_c)
        cs = _dot_exact_x(tri, lw)
        tot = jnp.sum(lw, axis=0, keepdims=True)
        g_inv = jnp.exp(-cs)
        dec_end = jnp.exp(tot - cs)
        a_t = -kk * jnp.exp(cs - lw)
        r_t = (r * jnp.exp(cs)).astype(BF16)
        a_s = stack(a_t).astype(BF16)
        bk_s = jnp.concatenate([stack(kka * g_inv), stack(k * g_inv)], axis=0).astype(BF16)
        x1 = _dot_nt(a_s, bk_s)
        x2 = _dot_nt(r_t, bk_s)
        nmat = jnp.where(strict, x1[:, :nh * n], 0.0)
        ak = jnp.where(strict, x1[:, nh * n:], 0.0).astype(BF16)
        rb = jnp.where(incl, x2[:, :nh * n], 0.0).astype(BF16)
        rk = jnp.where(incl, x2[:, nh * n:], 0.0).astype(BF16)
        tinv = eye + nmat
        p = nmat.astype(BF16)
        steps = n.bit_length() - 2
        for i in range(steps):
            p2 = _dot(p, p)
            tinv = tinv + _dot(tinv.astype(BF16), p2.astype(BF16))
            p = p2.astype(BF16)
        v_s = stack(v).astype(BF16)
        s_b = s_prev.astype(BF16)
        wmat = _dot_nt(a_s, s_b) + _dot(ak, v_s)
        u = _dot(tinv.astype(BF16), wmat.astype(BF16)).astype(BF16)
        y = _dot_nt(r_t, s_b) + _dot(rb, u) + _dot(rk, v_s)
        uv = jnp.concatenate([u, v_s], axis=0)
        bk_dec = jnp.concatenate([stack(kka * dec_end), stack(k * dec_end)], axis=0).astype(BF16)
        s_new = s_prev * jnp.exp(tot) + lax.dot_general(uv, bk_dec, (((0,), (0,)), ((), ())),
                                                        preferred_element_type=F32)
        return y, s_new

    for bi in range(nbatch):
        y, s_new = chunk(fwd_refs, bi, False, s_scr[bi])
        yf_ref[bi] = y
        s_scr[bi] = s_new
        y, s_new = chunk(bwd_refs, bi, True, s_scr[nbatch + bi])
        yb_ref[bi] = y
        s_scr[nbatch + bi] = s_new


def _rwkv_scan(pre):
    b, t, w = pre["r"].shape
    nbs = t // SCAN_BLOCK
    ctx_blocks = N_CTX // SCAN_BLOCK

    def fwd_blk(j):
        return j

    def bwd_blk(j):
        return jnp.where(j < ctx_blocks, ctx_blocks - 1 - j, nbs + ctx_blocks - 1 - j)

    in_f = pl.BlockSpec((b, SCAN_BLOCK, w), lambda j: (0, fwd_blk(j), 0))
    in_b = pl.BlockSpec((b, SCAN_BLOCK, w), lambda j: (0, bwd_blk(j), 0))
    oshape = jax.ShapeDtypeStruct((b, t, w), F32)

    def body(*refs):
        _rwkv_scan_body(b, refs[0:6], refs[6:12], refs[12], refs[13], refs[14])

    return pl.pallas_call(
        body,
        grid=(nbs,),
        in_specs=[in_f] * 6 + [in_b] * 6,
        out_specs=[in_f, in_b],
        out_shape=[oshape, oshape],
        scratch_shapes=[pltpu.VMEM((2 * b, w, w), F32)],
        compiler_params=_cparams(("arbitrary",)),
        name="rwkv_scan",
    )(pre["r"], pre["v"], pre["kk"], pre["lw_f"], pre["k_f"], pre["kka_f"],
      pre["r"], pre["v"], pre["kk"], pre["lw_b"], pre["k_b"], pre["kka_b"])


def _outproj_body(x_ref, m_ref, s0_ref, s1_ref, z_ref, rf_ref, rb_ref, bonus_ref, gate_ref, g0_ref, g1_ref, og_ref,
                  snw_ref, lxw_ref, lxb_ref, gnw_ref, wo_ref, l1w_ref, l1b_ref, rtw_ref, rtb_ref,
                  x1_ref, h_ref, g_ref):
    m = m_ref[0, 0]
    ys = (s0_ref[0, 0] + s1_ref[0, 0]) * _silu(z_ref[0])
    gw = SSD_INNER // 2
    parts = []
    for g in range(2):
        part = ys[:, gw * g:gw * (g + 1)]
        parts.append(part * lax.rsqrt(jnp.mean(part * part, axis=-1, keepdims=True) + 1e-5))
    y_ssd = jnp.concatenate(parts, axis=1) * snw_ref[...]
    ones64 = _group_ones(RWKV_WIDTH, RWKV_HEAD_DIM)
    inv = 1.0 / RWKV_HEAD_DIM
    yr = rf_ref[0] + rb_ref[0]
    mu = _dot_x_exact(yr, ones64) * inv
    yc = yr - mu
    var = _dot_x_exact(yc * yc, ones64) * inv
    yn = yc * lax.rsqrt(var + RWKV_LNX_EPS) * lxw_ref[...] + lxb_ref[...]
    y_rwkv = (yn + bonus_ref[0]) * gate_ref[0]
    o = g0_ref[0, 0] + g1_ref[0, 0]
    ms = _dot_x_exact(o * o, ones64) * (1.0 / GLA_VAL_DIM)
    y_gla = o * lax.rsqrt(ms + 1e-5) * gnw_ref[...] * _silu(og_ref[0])

    y_mix = jnp.concatenate([y_ssd, y_rwkv, y_gla], axis=1).astype(BF16)
    mix = _dot(y_mix, wo_ref[...])
    x1 = _layernorm(DN_ALPHA * x_ref[0] + m[2:3] * mix) * l1w_ref[...] + l1b_ref[...]
    x1_ref[0] = x1
    h = _layernorm(x1) * (1.0 + m[4:5]) + m[3:4]
    h_ref[0] = h.astype(BF16)
    g_ref[0] = _route(h, rtw_ref[...], rtb_ref[...])


def _outproj(xs, msel, ssd_y, z, rf, rb, bonus, gate, gla_o, og, snw, lxw, lxb, gnw, wo, l1w, l1b, rtw, rtb):
    b, t, d = xs.shape
    nb = t // TOK_BLOCK
    tok = lambda w: pl.BlockSpec((1, TOK_BLOCK, w), lambda i, j: (i, j, 0))
    dirspec = lambda w, dd: pl.BlockSpec((1, 1, TOK_BLOCK, w), lambda i, j: (dd, i, j, 0))
    c2 = lambda i, j: (0, 0)
    row = lambda w: pl.BlockSpec((1, w), c2)
    return pl.pallas_call(
        _outproj_body,
        grid=(b, nb),
        in_specs=[tok(d),
                  pl.BlockSpec((1, 1, N_MOD, d), lambda i, j: (i, jnp.minimum(j, 1), 0, 0)),
                  dirspec(SSD_INNER, 0), dirspec(SSD_INNER, 1), tok(SSD_INNER),
                  tok(RWKV_WIDTH), tok(RWKV_WIDTH), tok(RWKV_WIDTH), tok(RWKV_WIDTH),
                  dirspec(GLA_V_WIDTH, 0), dirspec(GLA_V_WIDTH, 1), tok(GLA_V_WIDTH),
                  row(SSD_INNER), row(RWKV_WIDTH), row(RWKV_WIDTH), row(GLA_V_WIDTH),
                  pl.BlockSpec((d, d), c2), row(d), row(d),
                  pl.BlockSpec((d, 128), c2), row(128)],
        out_specs=[tok(d), tok(d), tok(128)],
        out_shape=[jax.ShapeDtypeStruct((b, t, d), F32), jax.ShapeDtypeStruct((b, t, d), BF16),
                   jax.ShapeDtypeStruct((b, t, 128), F32)],
        compiler_params=_cparams(("arbitrary", "arbitrary")),
        name="out_proj_router",
    )(xs, msel, ssd_y, ssd_y, z, rf, rb, bonus, gate, gla_o, gla_o, og, snw, lxw, lxb, gnw, wo, l1w, l1b, rtw, rtb)


def _route(h, rw, rb):
    tm = h.shape[0]
    logits = _dot_hp(h, rw)
    lane = _iota((tm, 128), 1)
    valid = lane < N_EXPERTS
    scores = _sigmoid(logits)
    neg = -jnp.inf
    sel = jnp.where(valid, scores + rb, neg)
    per_group = N_EXPERTS // N_GROUPS
    grp = lane // per_group

    def take_max(cur):
        mx = jnp.max(cur, axis=-1, keepdims=True)
        first = jnp.min(jnp.where(cur == mx, lane, 1 << 20), axis=-1, keepdims=True)
        hit = lane == first
        return mx, hit

    gscore = jnp.full((tm, 128), neg, F32)
    for g in range(N_GROUPS):
        cur = jnp.where(grp == g, sel, neg)
        m1, hit = take_max(cur)
        m2 = jnp.max(jnp.where(hit, neg, cur), axis=-1, keepdims=True)
        gscore = jnp.where(lane == g, m1 + m2, gscore)
    allowed = jnp.zeros((tm, 128), jnp.bool_)
    cur = gscore
    for _ in range(TOPK_GROUPS):
        _, hit = take_max(cur)
        cur = jnp.where(hit, neg, cur)
        gidx = jnp.min(jnp.where(hit, lane, 1 << 20), axis=-1, keepdims=True)
        allowed = allowed | (grp == gidx)
    cur = jnp.where(allowed & valid, sel, neg)
    chosen = jnp.zeros((tm, 128), jnp.bool_)
    for _ in range(TOP_K):
        _, hit = take_max(cur)
        cur = jnp.where(hit, neg, cur)
        chosen = chosen | hit
    wts = jnp.where(chosen, scores, 0.0)
    return wts / jnp.sum(wts, axis=-1, keepdims=True) * ROUTED_SCALE


def _moe_body(h_ref, g_ref, w1_ref, w3_ref, w2_ref, o_ref):
    e0 = pl.program_id(1) * EXPERTS_PER_STEP

    @pl.when(pl.program_id(1) == 0)
    def _():
        o_ref[...] = jnp.zeros_like(o_ref)

    hb = h_ref[...]
    gates = g_ref[...]
    lane = _iota(gates.shape, 1)
    hids = []
    for e in range(EXPERTS_PER_STEP):
        gcol = jnp.sum(jnp.where(lane == e0 + e, gates, 0.0), axis=-1, keepdims=True)
        hid = _silu(_dot(hb, w1_ref[e])) * _dot(hb, w3_ref[e])
        hids.append((hid * gcol).astype(BF16))
    o_ref[...] += _dot(jnp.concatenate(hids, axis=1), w2_ref[0])


def _moe(h2, gates, w1, w3, w2g):
    n, d = h2.shape
    tm = n // 8 if n % (8 * 128) == 0 else 1024
    ne, _, f = w1.shape
    eps = EXPERTS_PER_STEP
    return pl.pallas_call(
        _moe_body,
        grid=(n // tm, ne // eps),
        in_specs=[pl.BlockSpec((tm, d), lambda i, g: (i, 0)),
                  pl.BlockSpec((tm, 128), lambda i, g: (i, 0)),
                  pl.BlockSpec((eps, d, f), lambda i, g: (g, 0, 0)),
                  pl.BlockSpec((eps, d, f), lambda i, g: (g, 0, 0)),
                  pl.BlockSpec((1, eps * f, d), lambda i, g: (g, 0, 0))],
        out_specs=pl.BlockSpec((tm, d), lambda i, g: (i, 0)),
        out_shape=jax.ShapeDtypeStruct((n, d), F32),
        compiler_params=_cparams(("arbitrary", "arbitrary")),
        name="moe_experts",
    )(h2, gates, w1, w3, w2g)


def _final_body(x1_ref, h_ref, rt_ref, m_ref, s13_ref, s2_ref, l2w_ref, l2b_ref, o_ref):
    m = m_ref[0, 0]
    a = _dot(h_ref[0], s13_ref[...])
    fs = a.shape[1] // 2
    hid = (_silu(a[:, :fs]) * a[:, fs:]).astype(BF16)
    f = rt_ref[0] + _dot(hid, s2_ref[...])
    o_ref[0] = _layernorm(DN_ALPHA * x1_ref[0] + m[5:6] * f) * l2w_ref[...] + l2b_ref[...]


def _final(x1, h, routed, msel, s13, s2, l2w, l2b):
    b, t, d = x1.shape
    nb = t // TOK_BLOCK
    tok = pl.BlockSpec((1, TOK_BLOCK, d), lambda i, j: (i, j, 0))
    c2 = lambda i, j: (0, 0)
    return pl.pallas_call(
        _final_body,
        grid=(b, nb),
        in_specs=[tok, tok, tok,
                  pl.BlockSpec((1, 1, N_MOD, d), lambda i, j: (i, jnp.minimum(j, 1), 0, 0)),
                  pl.BlockSpec(s13.shape, c2), pl.BlockSpec(s2.shape, c2),
                  pl.BlockSpec((1, d), c2), pl.BlockSpec((1, d), c2)],
        out_specs=tok,
        out_shape=jax.ShapeDtypeStruct((b, t, d), F32),
        compiler_params=_cparams(("arbitrary", "arbitrary")),
        name="shared_ffn_ln",
    )(x1, h, routed, msel, s13, s2, l2w, l2b)


def _prep_w_in(w):
    z, xbc, dt, rw, gq, gk, gv, gd, og = jnp.split(
        w, np.cumsum([512, 1024, 16, 1152, 128, 128, 256, 32, 256])[:-1].tolist(), axis=1)
    small = jnp.concatenate([dt, gd, jnp.zeros((w.shape[0], 128 - 48), w.dtype)], axis=1)
    return jnp.concatenate([z, xbc, rw, gq, gk, gv, og, small], axis=1).astype(BF16)


def _block_diag2(m):
    r, c = m.shape[1:]
    zero = jnp.zeros((r, c), m.dtype)
    return jnp.concatenate([jnp.concatenate([m[0], zero], axis=1),
                            jnp.concatenate([zero, m[1]], axis=1)], axis=0)


def _pad_lanes(v, n=128):
    v = v.reshape(1, -1)
    return jnp.pad(v, ((0, 0), (0, n - v.shape[1])))


def _token_mixer(xs, msel, l, p):
    z, xbc, rw, gq, gk, gv, og, small = _inproj(xs, msel, _prep_w_in(p["w_in"][l]))
    ssd_y = _ssd(xbc, small, p["ssd_conv_w"][l], p["ssd_conv_b"][l].reshape(1, -1),
                 _pad_lanes(p["ssd_dt_bias"][l]), _pad_lanes(p["ssd_a_log"][l]),
                 jnp.repeat(p["ssd_d"][l], SSD_HEAD_DIM).reshape(1, -1))
    gu_pad = jnp.pad(p["gla_gu"][l], ((0, 0), (0, 128 - GLA_GATE_LORA), (0, 0)))
    gla_o = _gla(gq, gk, gv, small, gu_pad, p["gla_gb"][l])
    pre = dict(zip(_RWKV_OUTS, _rwkv_prep(
        rw, p["rwkv_mu"][l], p["rwkv_w0"][l].reshape(1, -1), _block_diag2(p["rwkv_w2"][l]).astype(BF16),
        p["rwkv_a0"][l].reshape(1, -1), _block_diag2(p["rwkv_a2"][l]).astype(BF16),
        p["rwkv_g2"][l].astype(BF16), p["rwkv_kk"][l].reshape(1, -1), p["rwkv_ka"][l].reshape(1, -1),
        p["rwkv_rk"][l].reshape(1, -1))))
    rf, rb = _rwkv_scan(pre)
    return z, og, ssd_y, gla_o, rf, rb, pre["bonus"], pre["gate"]


def kernel(x, c, ctx, c_ctx, ada_w, ada_b, w_in, ssd_conv_w, ssd_conv_b, ssd_dt_bias, ssd_a_log, ssd_d, ssd_norm_w, rwkv_mu, rwkv_w0, rwkv_w2, rwkv_a0, rwkv_a2, rwkv_g2, rwkv_kk, rwkv_ka, rwkv_rk, rwkv_lnx_w, rwkv_lnx_b, gla_gu, gla_gb, gla_norm_w, w_out, ln1_w, ln1_b, ln2_w, ln2_b, router_w, router_b, exp_w1, exp_w3, exp_w2, sh_w1, sh_w3, sh_w2):
    p = dict(w_in=w_in, ssd_conv_w=ssd_conv_w, ssd_conv_b=ssd_conv_b, ssd_dt_bias=ssd_dt_bias, ssd_a_log=ssd_a_log,
             ssd_d=ssd_d, rwkv_mu=rwkv_mu, rwkv_w0=rwkv_w0, rwkv_w2=rwkv_w2, rwkv_a0=rwkv_a0, rwkv_a2=rwkv_a2,
             rwkv_g2=rwkv_g2, rwkv_kk=rwkv_kk, rwkv_ka=rwkv_ka, rwkv_rk=rwkv_rk, gla_gu=gla_gu, gla_gb=gla_gb)
    bsz, seq, d = x.shape
    n_ctx = ctx.shape[1]
    assert n_ctx == N_CTX and d == D_MODEL and seq % TOK_BLOCK == 0 and seq // GRID_W >= 8
    depth = ada_w.shape[0]
    row = lambda a: a.reshape(1, -1)

    cond = jnp.concatenate([c, c_ctx[None], jnp.zeros((8 - bsz - 1, d), F32)], axis=0)
    mods = _ada_all(cond, ada_w, ada_b).reshape(depth, 8, N_MOD, d)
    xs = jnp.concatenate([ctx, x], axis=1)
    t = xs.shape[1]
    for l in range(depth):
        msel = jnp.stack([jnp.broadcast_to(mods[l, bsz], (bsz, N_MOD, d)), mods[l, :bsz]], axis=1)
        z, og, ssd_y, gla_o, rf, rb, bonus, gate = _token_mixer(xs, msel, l, p)
        x1, h, gates = _outproj(xs, msel, ssd_y, z, rf, rb, bonus, gate, gla_o, og,
                                row(ssd_norm_w[l]), row(rwkv_lnx_w[l]), row(rwkv_lnx_b[l]), row(gla_norm_w[l]),
                                w_out[l].astype(BF16), row(ln1_w[l]), row(ln1_b[l]),
                                jnp.pad(router_w[l], ((0, 0), (0, 128 - N_EXPERTS))), _pad_lanes(router_b[l]))
        routed = _moe(h.reshape(bsz * t, d), gates.reshape(bsz * t, 128),
                      exp_w1[l].astype(BF16), exp_w3[l].astype(BF16),
                      exp_w2[l].astype(BF16).reshape(N_EXPERTS // EXPERTS_PER_STEP, -1, d))
        s13 = jnp.concatenate([sh_w1[l], sh_w3[l]], axis=1).astype(BF16)
        xs = _final(x1, h, routed.reshape(bsz, t, d), msel, s13, sh_w2[l].astype(BF16), row(ln2_w[l]), row(ln2_b[l]))
    return xs[:, n_ctx:]
```

```python
import functools

import jax
import jax.numpy as jnp
import numpy as np
from jax import lax
from jax.experimental import pallas as pl
from jax.experimental.pallas import tpu as pltpu

F32 = jnp.float32
BF16 = jnp.bfloat16

D_MODEL = 1024
N_CTX = 256
GRID_W = 64
N_MOD = 6
LN_EPS = 1e-5
DEPTH = 4
DN_ALPHA = (2 * DEPTH) ** 0.25

SSD_HEADS = 8
SSD_HEAD_DIM = 64
SSD_INNER = 512
SSD_STATE = 128
SSD_XBC = 1024

RWKV_WIDTH = 256
RWKV_HEAD_DIM = 64
RWKV_COLS = 1152
RWKV_LNX_EPS = 64e-5

GLA_K_WIDTH = 128
GLA_V_WIDTH = 256
GLA_KEY_DIM = 32
GLA_VAL_DIM = 64
GLA_GATE_LORA = 16
GLA_GATE_NORMALIZER = 16.0
GLA_SUB = 64

N_EXPERTS = 64
TOP_K = 8
N_GROUPS = 8
TOPK_GROUPS = 4
EXPERT_DIM = 256
ROUTED_SCALE = 2.5
EXPERTS_PER_STEP = 4

TOK_BLOCK = 256
SCAN_BLOCK = 64
MOE_TOK = 1024
V7X_VMEM_LIMIT = 56 * 1024 * 1024

_IN_PIECES = (("z", 512), ("xbc", 1024), ("rw", 1152), ("gq", 128), ("gk", 128), ("gv", 256), ("og", 256),
              ("small", 128))
IN_PAD = sum(w for _, w in _IN_PIECES)


def _cparams(sem):
    return pltpu.CompilerParams(dimension_semantics=sem, vmem_limit_bytes=V7X_VMEM_LIMIT)


def _split3(a):
    hi = a.astype(BF16)
    r1 = a - hi.astype(F32)
    mid = r1.astype(BF16)
    lo = (r1 - mid.astype(F32)).astype(BF16)
    return hi, mid, lo


def _dot(a, b):
    return jnp.dot(a, b, preferred_element_type=F32)


def _dot_nt(a, b):
    return lax.dot_general(a, b, (((1,), (1,)), ((), ())), preferred_element_type=F32)


def _dot_x_exact(a, e):
    hi, mid, lo = _split3(a)
    return _dot(hi, e) + (_dot(mid, e) + _dot(lo, e))


def _dot_exact_x(e, a):
    hi, mid, lo = _split3(a)
    return _dot(e, hi) + (_dot(e, mid) + _dot(e, lo))


def _dot_hp(a, b):
    ah, am, _ = _split3(a)
    bh, bm, _ = _split3(b)
    return _dot(ah, bh) + (_dot(ah, bm) + _dot(am, bh))


def _sigmoid(x):
    return 1.0 / (1.0 + jnp.exp(-x))


def _silu(x):
    return x * _sigmoid(x)


def _softplus(x):
    return jnp.maximum(x, 0.0) + jnp.log(1.0 + jnp.exp(-jnp.abs(x)))


def _layernorm(x):
    mu = jnp.mean(x, axis=-1, keepdims=True)
    xc = x - mu
    var = jnp.mean(xc * xc, axis=-1, keepdims=True)
    return xc * lax.rsqrt(var + LN_EPS)


def _iota(shape, dim):
    return lax.broadcasted_iota(jnp.int32, shape, dim)


def _tri(n, upper):
    r = _iota((n, n), 0)
    c = _iota((n, n), 1)
    return jnp.where(upper, c - r, r - c) >= 0


def _block_of(d, j, nb):
    return jnp.where(d == 0, j, jnp.where(j == 0, 0, nb - j))


def _group_ones(n, g):
    r = _iota((n, n), 0) // g
    c = _iota((n, n), 1) // g
    return (r == c).astype(BF16)


def _run_interleaved(gens):
    results = [None] * len(gens)
    active = list(range(len(gens)))
    while active:
        for i in list(active):
            try:
                next(gens[i])
            except StopIteration as stop:
                results[i] = stop.value
                active.remove(i)
    return results


def _ada_body(c_ref, w_ref, b_ref, o_ref):
    a = _silu(c_ref[...]).astype(BF16)
    o_ref[0] = _dot(a, w_ref[0].astype(BF16)) + b_ref[0]


def _ada_all(cond, ada_w, ada_b):
    nl, d, n = ada_w.shape
    tn = 1536
    rows = cond.shape[0]
    return pl.pallas_call(
        _ada_body,
        grid=(nl, n // tn),
        in_specs=[pl.BlockSpec((rows, d), lambda l, k: (0, 0)),
                  pl.BlockSpec((1, d, tn), lambda l, k: (l, 0, k)),
                  pl.BlockSpec((1, 1, tn), lambda l, k: (l, 0, k))],
        out_specs=pl.BlockSpec((1, rows, tn), lambda l, k: (l, 0, k)),
        out_shape=jax.ShapeDtypeStruct((nl, rows, n), F32),
        compiler_params=_cparams(("arbitrary", "arbitrary")),
        name="ada_mod",
    )(cond, ada_w, ada_b.reshape(nl, 1, n))


def _inproj_body(x_ref, m_ref, w_ref, *out_refs):
    x = x_ref[0]
    m = m_ref[0, 0]
    u = _layernorm(x) * (1.0 + m[1:2]) + m[0:1]
    y = _dot(u.astype(BF16), w_ref[...])
    off = 0
    for (_, wdt), o in zip(_IN_PIECES, out_refs):
        o[0] = y[:, off:off + wdt]
        off += wdt


def _inproj(xs, msel, w_in_p):
    b, t, d = xs.shape
    nb = t // TOK_BLOCK
    out_shape = [jax.ShapeDtypeStruct((b, t, w), F32) for _, w in _IN_PIECES]
    out_specs = [pl.BlockSpec((1, TOK_BLOCK, w), lambda i, j: (i, j, 0)) for _, w in _IN_PIECES]
    return pl.pallas_call(
        _inproj_body,
        grid=(b, nb),
        in_specs=[pl.BlockSpec((1, TOK_BLOCK, d), lambda i, j: (i, j, 0)),
                  pl.BlockSpec((1, 1, N_MOD, d), lambda i, j: (i, jnp.minimum(j, 1), 0, 0)),
                  pl.BlockSpec((d, IN_PAD), lambda i, j: (0, 0))],
        out_specs=out_specs,
        out_shape=out_shape,
        compiler_params=_cparams(("arbitrary", "arbitrary")),
        name="in_proj",
    )(xs, msel, w_in_p)


def _ssd_body(nb, xbc_ref, hp_ref, hn_ref, small_ref, cw_ref, cb_ref, dtb_ref, alog_ref, dsk_ref,
              y_ref, h_scr, buf_scr):
    q = TOK_BLOCK
    d = pl.program_id(1)
    j = pl.program_id(2)
    blk = _block_of(d, j, nb)

    @pl.when(j == 0)
    def _():
        h_scr[...] = jnp.zeros_like(h_scr)

    seg_first = (blk == 0) | (blk == 1)
    seg_last = (blk == 0) | (blk == nb - 1)
    cur = xbc_ref[0]
    buf_scr[8:8 + q] = cur
    buf_scr[0:8] = jnp.where(seg_first, 0.0, hp_ref[0])
    buf_scr[8 + q:16 + q] = jnp.where(seg_last, 0.0, hn_ref[0])
    prev = buf_scr[7:7 + q]
    nxt = buf_scr[9:9 + q]
    cw = cw_ref[0]
    act = _silu(prev * cw[0:1] + cur * cw[1:2] + nxt * cw[2:3] + cb_ref[...])
    xs = act[:, :SSD_INNER]

    upper = d == 1
    tri_b = _tri(q, upper)
    tri = tri_b.astype(BF16)
    tri_t = _tri(q, jnp.logical_not(upper)).astype(BF16)

    lane = _iota((1, 128), 1)
    dt_all = _softplus(small_ref[0] + dtb_ref[...])
    neg_a = jnp.where(lane < 2 * SSD_HEADS, -jnp.exp(alog_ref[...]), 0.0)
    la_all = dt_all * neg_a
    er = _iota((128, SSD_INNER), 0)
    ec = _iota((128, SSD_INNER), 1)
    e_exp = (er == SSD_HEADS * d + ec // SSD_HEAD_DIM).astype(BF16)
    er8 = _iota((128, 128), 0)
    ec8 = _iota((128, 128), 1)
    e8 = ((er8 == SSD_HEADS * d + ec8) & (ec8 < SSD_HEADS)).astype(BF16)

    dt_exp = _dot_x_exact(dt_all, e_exp)
    la_exp = _dot_x_exact(la_all, e_exp)
    la8 = _dot_x_exact(la_all, e8)
    cs_exp = _dot_exact_x(tri, la_exp)
    cs8 = _dot_exact_x(tri, la8)
    cs_t = _dot_x_exact(la8.T, tri_t)
    tot_exp = jnp.sum(la_exp, axis=0, keepdims=True)

    xdt = xs * dt_exp
    lane_q = _iota((q, 128), 1)
    ys = []
    for p in range(SSD_HEADS // 2):
        xdt_p = xdt[:, 128 * p:128 * (p + 1)].astype(BF16)
        halves = []
        for h in (2 * p, 2 * p + 1):
            g = h // (SSD_HEADS // 2)
            c_g = act[:, 768 + 128 * g:896 + 128 * g].astype(BF16)
            b_g = act[:, 512 + 128 * g:640 + 128 * g].astype(BF16)
            gmat = _dot_nt(c_g, b_g)
            seg = cs8[:, h:h + 1] - cs_t[h:h + 1, :]
            lmat = jnp.where(tri_b, jnp.exp(jnp.minimum(seg, 0.0)), 0.0)
            halves.append(_dot((gmat * lmat).astype(BF16), xdt_p))
        ys.append(jnp.where(lane_q < SSD_HEAD_DIM, halves[0], halves[1]))
    y = jnp.concatenate(ys, axis=1)

    dec = jnp.exp(tot_exp - cs_exp)
    xd = (xdt * dec).astype(BF16)
    offs = []
    for g in range(2):
        c_g = act[:, 768 + 128 * g:896 + 128 * g].astype(BF16)
        b_g = act[:, 512 + 128 * g:640 + 128 * g]
        hg = h_scr[g]
        offs.append(_dot(c_g, hg.astype(BF16)))
        hn = _dot(b_g.T.astype(BF16), xd[:, 256 * g:256 * (g + 1)])
        h_scr[g] = hg * jnp.exp(tot_exp[:, 256 * g:256 * (g + 1)]) + hn
    y = y + jnp.concatenate(offs, axis=1) * jnp.exp(cs_exp)
    y = y + jnp.where(d == 0, dsk_ref[...], 0.0) * xs
    y_ref[0, 0] = y


def _ssd(xbc, small, conv_w, conv_b, dtb_pad, alog_pad, dskip_exp):
    b, t, _ = xbc.shape
    nb = t // TOK_BLOCK
    r8 = TOK_BLOCK // 8

    def main_map(i, d, j):
        return (i, _block_of(d, j, nb), 0)

    def prev_map(i, d, j):
        return (i, jnp.maximum(_block_of(d, j, nb) * r8 - 1, 0), 0)

    def next_map(i, d, j):
        return (i, jnp.minimum(_block_of(d, j, nb) * r8 + r8, t // 8 - 1), 0)

    def out_map(i, d, j):
        return (d, i, _block_of(d, j, nb), 0)

    const2 = lambda i, d, j: (0, 0)
    return pl.pallas_call(
        functools.partial(_ssd_body, nb),
        grid=(b, 2, nb),
        in_specs=[pl.BlockSpec((1, TOK_BLOCK, SSD_XBC), main_map),
                  pl.BlockSpec((1, 8, SSD_XBC), prev_map),
                  pl.BlockSpec((1, 8, SSD_XBC), next_map),
                  pl.BlockSpec((1, TOK_BLOCK, 128), main_map),
                  pl.BlockSpec((1, 3, SSD_XBC), lambda i, d, j: (0, 0, 0)),
                  pl.BlockSpec((1, SSD_XBC), const2),
                  pl.BlockSpec((1, 128), const2),
                  pl.BlockSpec((1, 128), const2),
                  pl.BlockSpec((1, SSD_INNER), const2)],
        out_specs=pl.BlockSpec((1, 1, TOK_BLOCK, SSD_INNER), out_map),
        out_shape=jax.ShapeDtypeStruct((2, b, t, SSD_INNER), F32),
        scratch_shapes=[pltpu.VMEM((2, SSD_STATE, 256), F32),
                        pltpu.VMEM((TOK_BLOCK + 16, SSD_XBC), F32)],
        compiler_params=_cparams(("arbitrary", "arbitrary", "arbitrary")),
        name="ssd",
    )(xbc, xbc, xbc, small, conv_w[None], conv_b, dtb_pad, alog_pad, dskip_exp)


def _gla_body(nbatch, fwd_refs, bwd_refs, gu_ref, gb_ref, of_ref, ob_ref, s_scr):
    j = pl.program_id(0)
    c = GLA_SUB
    nsub = TOK_BLOCK // c

    @pl.when(j == 0)
    def _():
        s_scr[...] = jnp.zeros_like(s_scr)

    lr = _iota((128, 128), 0)
    lc = _iota((128, 128), 1)
    lane_k = _iota((c, GLA_K_WIDTH), 1) // GLA_KEY_DIM
    lane_v = _iota((c, GLA_V_WIDTH), 1) // GLA_VAL_DIM
    st_mask = (_iota((GLA_V_WIDTH, GLA_K_WIDTH), 0) // GLA_VAL_DIM
               == _iota((GLA_V_WIDTH, GLA_K_WIDTH), 1) // GLA_KEY_DIM)

    def stream(refs, bi, upper, st):
        q_ref, k_ref, v_ref, small_ref = refs
        d = 1 if upper else 0
        tri_b = _tri(c, upper)
        tri = tri_b.astype(BF16)
        gsel = ((lr == 2 * SSD_HEADS + GLA_GATE_LORA * d + lc) & (lc < GLA_GATE_LORA)).astype(BF16)
        gd = _dot_x_exact(small_ref[bi], gsel)
        yield
        pre = _dot_hp(gd, gu_ref[d]) + gb_ref[d:d + 1]
        la_all = -_softplus(-pre) * (1.0 / GLA_GATE_NORMALIZER)
        yield
        outs = [None] * nsub
        for si in range(nsub):
            sub = nsub - 1 - si if upper else si
            lo = sub * c
            la = la_all[lo:lo + c]
            qq = q_ref[bi, lo:lo + c, :] * (GLA_KEY_DIM ** -0.5)
            kk = k_ref[bi, lo:lo + c, :]
            vv = v_ref[bi, lo:lo + c, :]
            cs = _dot_exact_x(tri, la)
            yield
            tot = jnp.sum(la, axis=0, keepdims=True)
            qt = qq * jnp.exp(cs)
            kt = (kk * jnp.exp(jnp.minimum(-cs, 80.0))).astype(BF16)
            kdec = (kk * jnp.exp(tot - cs)).astype(BF16)
            vb = vv.astype(BF16)
            o = _dot_nt(qt.astype(BF16), st.astype(BF16))
            amats = []
            for h in range(GLA_K_WIDTH // GLA_KEY_DIM):
                qh = jnp.where(lane_k == h, qt, 0.0).astype(BF16)
                amats.append(jnp.where(tri_b, _dot_nt(qh, kt), 0.0).astype(BF16))
            upd = _dot(vv.T.astype(BF16), kdec)
            yield
            for h, a in enumerate(amats):
                o = o + jnp.where(lane_v == h, _dot(a, vb), 0.0)
            st = st * jnp.exp(tot) + jnp.where(st_mask, upd, 0.0)
            outs[sub] = o
            yield
        return jnp.concatenate(outs, axis=0), st

    gens = ([stream(fwd_refs, bi, False, s_scr[bi]) for bi in range(nbatch)]
            + [stream(bwd_refs, bi, True, s_scr[nbatch + bi]) for bi in range(nbatch)])
    for si, (o, st) in enumerate(_run_interleaved(gens)):
        if si < nbatch:
            of_ref[si] = o
        else:
            ob_ref[si - nbatch] = o
        s_scr[si] = st


def _gla(gq, gk, gv, small, gu_pad, gb):
    b, t, _ = gq.shape
    nb = t // TOK_BLOCK

    def bwd_blk(j):
        return jnp.where(j == 0, 0, nb - j)

    def specs(blk):
        return [pl.BlockSpec((b, TOK_BLOCK, GLA_K_WIDTH), lambda j: (0, blk(j), 0)),
                pl.BlockSpec((b, TOK_BLOCK, GLA_K_WIDTH), lambda j: (0, blk(j), 0)),
                pl.BlockSpec((b, TOK_BLOCK, GLA_V_WIDTH), lambda j: (0, blk(j), 0)),
                pl.BlockSpec((b, TOK_BLOCK, 128), lambda j: (0, blk(j), 0))]

    def body(*refs):
        _gla_body(b, refs[0:4], refs[4:8], *refs[8:])

    oshape = jax.ShapeDtypeStruct((b, t, GLA_V_WIDTH), F32)
    return pl.pallas_call(
        body,
        grid=(nb,),
        in_specs=specs(lambda j: j) + specs(bwd_blk) + [
            pl.BlockSpec((2, 128, GLA_K_WIDTH), lambda j: (0, 0, 0)),
            pl.BlockSpec((2, GLA_K_WIDTH), lambda j: (0, 0))],
        out_specs=[pl.BlockSpec((b, TOK_BLOCK, GLA_V_WIDTH), lambda j: (0, j, 0)),
                   pl.BlockSpec((b, TOK_BLOCK, GLA_V_WIDTH), lambda j: (0, bwd_blk(j), 0))],
        out_shape=[oshape, oshape],
        scratch_shapes=[pltpu.VMEM((2 * b, GLA_V_WIDTH, GLA_K_WIDTH), F32)],
        compiler_params=_cparams(("arbitrary",)),
        name="gla",
    )(gq, gk, gv, small, gq, gk, gv, small, gu_pad, gb)


_RWKV_OUTS = ("r", "v", "kk", "lw_f", "k_f", "kka_f", "lw_b", "k_b", "kka_b", "bonus", "gate")


def _rwkv_prep_body(nb, rw_ref, hp_ref, hn_ref, mu_ref, w0_ref, w2_ref, a0_ref, a2_ref, g2_ref, kkp_ref, ka_ref,
                    rk_ref, *refs):
    outs = dict(zip(_RWKV_OUTS, refs[:len(_RWKV_OUTS)]))
    buf_scr = refs[len(_RWKV_OUTS)]
    q = TOK_BLOCK
    j = pl.program_id(1)
    is_ctx = j == 0
    z = rw_ref[0]
    zeros8 = jnp.zeros((8, RWKV_COLS), F32)
    buf_scr[0:8] = zeros8
    buf_scr[8 + q:16 + q] = zeros8
    buf_scr[8:8 + q] = z
    t = _iota((q, 1), 0)
    col = t % GRID_W
    prev = jnp.where(jnp.where(is_ctx, t, col) == 0, 0.0, buf_scr[7:7 + q])
    nxt = jnp.where(jnp.where(is_ctx, t - (q - 1), col - (GRID_W - 1)) == 0, 0.0, buf_scr[9:9 + q])
    up = jnp.concatenate([jnp.where(j == 1, 0.0, hp_ref[0]), z[:q - GRID_W]], axis=0)
    down = jnp.concatenate([z[GRID_W:], jnp.where(j == nb - 1, 0.0, hn_ref[0])], axis=0)
    mu = mu_ref[...]
    vert = mu[2:3] * (up - z) + mu[3:4] * (down - z)
    mixed = z + mu[0:1] * (prev - z) + mu[1:2] * (nxt - z) + jnp.where(is_ctx, 0.0, vert)

    w = RWKV_WIDTH
    r = mixed[:, 0:w]
    k = mixed[:, w:2 * w]
    v = mixed[:, 2 * w:3 * w]
    wd = mixed[:, 3 * w:3 * w + 128]
    ad = mixed[:, 3 * w + 128:4 * w]
    gdr = mixed[:, 4 * w:4 * w + 128]

    lw = w0_ref[...] + _dot(jnp.tanh(wd).astype(BF16), w2_ref[...])
    log_dec = -jnp.exp(-_softplus(-lw) - 0.5)
    a = _sigmoid(a0_ref[...] + _dot(ad.astype(BF16), a2_ref[...]))
    ones64 = _group_ones(w, RWKV_HEAD_DIM)
    kkv = k * kkp_ref[...]
    nrm = jnp.maximum(jnp.sqrt(_dot_x_exact(kkv * kkv, ones64)), 1e-12)
    kkn = kkv / nrm
    ka = ka_ref[...]
    a_f, a_b = a[:, :w], a[:, w:]
    k_f = k * (1.0 + (a_f - 1.0) * ka)
    k_b = k * (1.0 + (a_b - 1.0) * ka)
    outs["r"][0] = r
    outs["v"][0] = v
    outs["kk"][0] = kkn
    outs["lw_f"][0] = log_dec[:, :w]
    outs["lw_b"][0] = log_dec[:, w:]
    outs["k_f"][0] = k_f
    outs["k_b"][0] = k_b
    outs["kka_f"][0] = kkn * a_f
    outs["kka_b"][0] = kkn * a_b
    outs["bonus"][0] = _dot_x_exact(r * (k_f + k_b) * rk_ref[...], ones64) * v
    outs["gate"][0] = _dot(_sigmoid(gdr).astype(BF16), g2_ref[...])


def _rwkv_prep(rw, mu, w0cat, w2bd, a0cat, a2bd, g2, kkp, ka, rk):
    b, t, _ = rw.shape
    nb = t // TOK_BLOCK
    hb = TOK_BLOCK // GRID_W
    w = RWKV_WIDTH
    c2 = lambda i, j: (0, 0)
    return pl.pallas_call(
        functools.partial(_rwkv_prep_body, nb),
        grid=(b, nb),
        in_specs=[pl.BlockSpec((1, TOK_BLOCK, RWKV_COLS), lambda i, j: (i, j, 0)),
                  pl.BlockSpec((1, GRID_W, RWKV_COLS), lambda i, j: (i, jnp.maximum(j * hb - 1, 0), 0)),
                  pl.BlockSpec((1, GRID_W, RWKV_COLS),
                               lambda i, j: (i, jnp.minimum(j * hb + hb, t // GRID_W - 1), 0)),
                  pl.BlockSpec((4, RWKV_COLS), c2),
                  pl.BlockSpec((1, 2 * w), c2),
                  pl.BlockSpec((128, 2 * w), c2),
                  pl.BlockSpec((1, 2 * w), c2),
                  pl.BlockSpec((128, 2 * w), c2),
                  pl.BlockSpec((128, w), c2),
                  pl.BlockSpec((1, w), c2),
                  pl.BlockSpec((1, w), c2),
                  pl.BlockSpec((1, w), c2)],
        out_specs=[pl.BlockSpec((1, TOK_BLOCK, w), lambda i, j: (i, j, 0)) for _ in _RWKV_OUTS],
        out_shape=[jax.ShapeDtypeStruct((b, t, w), F32) for _ in _RWKV_OUTS],
        scratch_shapes=[pltpu.VMEM((TOK_BLOCK + 16, RWKV_COLS), F32)],
        compiler_params=_cparams(("arbitrary", "arbitrary")),
        name="rwkv_prep",
    )(rw, rw, rw, mu, w0cat, w2bd, a0cat, a2bd, g2, kkp, ka, rk)


def _rwkv_scan_body(nbatch, fwd_refs, bwd_refs, yf_ref, yb_ref, s_scr):
    j = pl.program_id(0)
    n = SCAN_BLOCK
    hd = RWKV_HEAD_DIM

    @pl.when(j == 0)
    def _():
        s_scr[...] = jnp.zeros_like(s_scr)

    nh = RWKV_WIDTH // hd
    w = RWKV_WIDTH
    lane_head = _iota((1, w), 1) // hd
    row_h = _iota((nh * n, nh * n), 0) // n
    row_c = _iota((nh * n, nh * n), 0) % n
    col_h = _iota((nh * n, nh * n), 1) // n
    col_c = _iota((nh * n, nh * n), 1) % n
    cat_c = _iota((n, nh * n), 0)
    cat_cc = _iota((n, nh * n), 1) % n
    eye = (_iota((nh * n, nh * n), 0) == _iota((nh * n, nh * n), 1)).astype(F32)

    def stack(x):
        return jnp.concatenate([jnp.where(lane_head == h, x, 0.0) for h in range(nh)], axis=0)

    def chunk(refs, bi, upper, s_prev):
        r, v, kk, lw, k, kka = (ref[bi] for ref in refs)
        tri = _tri(n, upper).astype(BF16)
        before = (col_c > row_c) if upper else (col_c < row_c)
        strict = (row_h == col_h) & before
        incl = (cat_cc >= cat_c) if upper else (cat_cc <= cat_c)
        cs = _dot_exact_x(tri, lw)
        yield
        tot = jnp.sum(lw, axis=0, keepdims=True)
        g_inv = jnp.exp(-cs)
        dec_end = jnp.exp(tot - cs)
        a_t = -kk * jnp.exp(cs - lw)
        r_t = (r * jnp.exp(cs)).astype(BF16)
        a_s = stack(a_t).astype(BF16)
        bk_s = jnp.concatenate([stack(kka * g_inv), stack(k * g_inv)], axis=0).astype(BF16)
        bk_dec = jnp.concatenate([stack(kka * dec_end), stack(k * dec_end)], axis=0).astype(BF16)
        v_s = stack(v).astype(BF16)
        x1 = _dot_nt(a_s, bk_s)
        x2 = _dot_nt(r_t, bk_s)
        yield
        nmat = jnp.where(strict, x1[:, :nh * n], 0.0)
        ak = jnp.where(strict, x1[:, nh * n:], 0.0).astype(BF16)
        rb = jnp.where(incl, x2[:, :nh * n], 0.0).astype(BF16)
        rk = jnp.where(incl, x2[:, nh * n:], 0.0).astype(BF16)
        s_b = s_prev.astype(BF16)
        w0 = _dot_nt(a_s, s_b) + _dot(ak, v_s)
        y0 = _dot_nt(r_t, s_b) + _dot(rk, v_s)
        tinv = eye + nmat
        p = nmat.astype(BF16)
        for i in range(n.bit_length() - 2):
            p2 = _dot(p, p)
            yield
            tinv = tinv + _dot(tinv.astype(BF16), p2.astype(BF16))
            p = p2.astype(BF16)
            yield
        u = _dot(tinv.astype(BF16), w0.astype(BF16)).astype(BF16)
        yield
        y = y0 + _dot(rb, u)
        uv = jnp.concatenate([u, v_s], axis=0)
        s_new = s_prev * jnp.exp(tot) + lax.dot_general(uv, bk_dec, (((0,), (0,)), ((), ())),
                                                        preferred_element_type=F32)
        return y, s_new

    gens = ([chunk(fwd_refs, bi, False, s_scr[bi]) for bi in range(nbatch)]
            + [chunk(bwd_refs, bi, True, s_scr[nbatch + bi]) for bi in range(nbatch)])
    for si, (y, s_new) in enumerate(_run_interleaved(gens)):
        if si < nbatch:
            yf_ref[si] = y
        else:
            yb_ref[si - nbatch] = y
        s_scr[si] = s_new


def _rwkv_scan(pre):
    b, t, w = pre["r"].shape
    nbs = t // SCAN_BLOCK
    ctx_blocks = N_CTX // SCAN_BLOCK

    def fwd_blk(j):
        return j

    def bwd_blk(j):
        return jnp.where(j < ctx_blocks, ctx_blocks - 1 - j, nbs + ctx_blocks - 1 - j)

    in_f = pl.BlockSpec((b, SCAN_BLOCK, w), lambda j: (0, fwd_blk(j), 0))
    in_b = pl.BlockSpec((b, SCAN_BLOCK, w), lambda j: (0, bwd_blk(j), 0))
    oshape = jax.ShapeDtypeStruct((b, t, w), F32)

    def body(*refs):
        _rwkv_scan_body(b, refs[0:6], refs[6:12], refs[12], refs[13], refs[14])

    return pl.pallas_call(
        body,
        grid=(nbs,),
        in_specs=[in_f] * 6 + [in_b] * 6,
        out_specs=[in_f, in_b],
        out_shape=[oshape, oshape],
        scratch_shapes=[pltpu.VMEM((2 * b, w, w), F32)],
        compiler_params=_cparams(("arbitrary",)),
        name="rwkv_scan",
    )(pre["r"], pre["v"], pre["kk"], pre["lw_f"], pre["k_f"], pre["kka_f"],
      pre["r"], pre["v"], pre["kk"], pre["lw_b"], pre["k_b"], pre["kka_b"])


def _outproj_body(x_ref, m_ref, s0_ref, s1_ref, z_ref, rf_ref, rb_ref, bonus_ref, gate_ref, g0_ref, g1_ref, og_ref,
                  snw_ref, lxw_ref, lxb_ref, gnw_ref, wo_ref, l1w_ref, l1b_ref, rtw_ref, rtb_ref,
                  x1_ref, h_ref, g_ref):
    m = m_ref[0, 0]
    ys = (s0_ref[0, 0] + s1_ref[0, 0]) * _silu(z_ref[0])
    gw = SSD_INNER // 2
    parts = []
    for g in range(2):
        part = ys[:, gw * g:gw * (g + 1)]
        parts.append(part * lax.rsqrt(jnp.mean(part * part, axis=-1, keepdims=True) + 1e-5))
    y_ssd = jnp.concatenate(parts, axis=1) * snw_ref[...]
    ones64 = _group_ones(RWKV_WIDTH, RWKV_HEAD_DIM)
    inv = 1.0 / RWKV_HEAD_DIM
    yr = rf_ref[0] + rb_ref[0]
    mu = _dot_x_exact(yr, ones64) * inv
    yc = yr - mu
    var = _dot_x_exact(yc * yc, ones64) * inv
    yn = yc * lax.rsqrt(var + RWKV_LNX_EPS) * lxw_ref[...] + lxb_ref[...]
    y_rwkv = (yn + bonus_ref[0]) * gate_ref[0]
    o = g0_ref[0] + g1_ref[0]
    ms = _dot_x_exact(o * o, ones64) * (1.0 / GLA_VAL_DIM)
    y_gla = o * lax.rsqrt(ms + 1e-5) * gnw_ref[...] * _silu(og_ref[0])

    y_mix = jnp.concatenate([y_ssd, y_rwkv, y_gla], axis=1).astype(BF16)
    mix = _dot(y_mix, wo_ref[...])
    x1 = _layernorm(DN_ALPHA * x_ref[0] + m[2:3] * mix) * l1w_ref[...] + l1b_ref[...]
    x1_ref[0] = x1
    h = _layernorm(x1) * (1.0 + m[4:5]) + m[3:4]
    h_ref[0] = h.astype(BF16)
    g_ref[0] = _route(h, rtw_ref[...], rtb_ref[...])


def _outproj(xs, msel, ssd_y, z, rf, rb, bonus, gate, gla_f, gla_b, og, snw, lxw, lxb, gnw, wo, l1w, l1b, rtw, rtb):
    b, t, d = xs.shape
    nb = t // TOK_BLOCK
    tok = lambda w: pl.BlockSpec((1, TOK_BLOCK, w), lambda i, j: (i, j, 0))
    dirspec = lambda w, dd: pl.BlockSpec((1, 1, TOK_BLOCK, w), lambda i, j: (dd, i, j, 0))
    c2 = lambda i, j: (0, 0)
    row = lambda w: pl.BlockSpec((1, w), c2)
    return pl.pallas_call(
        _outproj_body,
        grid=(b, nb),
        in_specs=[tok(d),
                  pl.BlockSpec((1, 1, N_MOD, d), lambda i, j: (i, jnp.minimum(j, 1), 0, 0)),
                  dirspec(SSD_INNER, 0), dirspec(SSD_INNER, 1), tok(SSD_INNER),
                  tok(RWKV_WIDTH), tok(RWKV_WIDTH), tok(RWKV_WIDTH), tok(RWKV_WIDTH),
                  tok(GLA_V_WIDTH), tok(GLA_V_WIDTH), tok(GLA_V_WIDTH),
                  row(SSD_INNER), row(RWKV_WIDTH), row(RWKV_WIDTH), row(GLA_V_WIDTH),
                  pl.BlockSpec((d, d), c2), row(d), row(d),
                  pl.BlockSpec((N_EXPERTS, d), c2), pl.BlockSpec((N_EXPERTS, 128), c2)],
        out_specs=[tok(d), tok(d), tok(128)],
        out_shape=[jax.ShapeDtypeStruct((b, t, d), F32), jax.ShapeDtypeStruct((b, t, d), BF16),
                   jax.ShapeDtypeStruct((b, t, 128), F32)],
        compiler_params=_cparams(("arbitrary", "arbitrary")),
        name="out_proj_router",
    )(xs, msel, ssd_y, ssd_y, z, rf, rb, bonus, gate, gla_f, gla_b, og, snw, lxw, lxb, gnw, wo, l1w, l1b, rtw, rtb)


def _route(h, rw_t, rb_t):
    tm = h.shape[0]
    ne = N_EXPERTS
    per_group = ne // N_GROUPS
    ah, am, _ = _split3(rw_t)
    bh, bm, _ = _split3(h)
    logits = _dot_nt(ah, bh) + (_dot_nt(ah, bm) + _dot_nt(am, bh))
    scores = _sigmoid(logits)
    sel = scores + jnp.concatenate([rb_t] * (tm // 128), axis=1)
    neg = -jnp.inf
    sub8 = _iota((per_group, tm), 0).astype(F32)
    eidx = _iota((ne, tm), 0).astype(F32)

    def first_max(cur, idx, sentinel):
        mx = jnp.max(cur, axis=0, keepdims=True)
        first = jnp.min(jnp.where(cur == mx, idx, sentinel), axis=0, keepdims=True)
        return mx, idx == first

    gs = []
    for g in range(N_GROUPS):
        blk = sel[per_group * g:per_group * (g + 1)]
        m1, hit = first_max(blk, sub8, float(per_group))
        m2 = jnp.max(jnp.where(hit, neg, blk), axis=0, keepdims=True)
        gs.append(m1 + m2)
    cur = jnp.concatenate(gs, axis=0)
    allowed_g = jnp.zeros((N_GROUPS, tm), F32)
    for _ in range(TOPK_GROUPS):
        _, hit = first_max(cur, sub8, float(N_GROUPS))
        cur = jnp.where(hit, neg, cur)
        allowed_g = jnp.where(hit, 1.0, allowed_g)
    allowed = jnp.concatenate([jnp.broadcast_to(allowed_g[g:g + 1], (per_group, tm)) for g in range(N_GROUPS)],
                              axis=0)
    cur = jnp.where(allowed > 0.0, sel, neg)
    wts = jnp.zeros((ne, tm), F32)
    for _ in range(TOP_K):
        _, hit = first_max(cur, eidx, float(ne))
        cur = jnp.where(hit, neg, cur)
        wts = jnp.where(hit, scores, wts)
    gates_t = wts / jnp.sum(wts, axis=0, keepdims=True) * ROUTED_SCALE
    return jnp.concatenate([gates_t, jnp.zeros((128 - ne, tm), F32)], axis=0).T


def _moe_body(h_ref, g_ref, w1_ref, w3_ref, w2_ref, o_ref):
    e0 = pl.program_id(1) * EXPERTS_PER_STEP

    @pl.when(pl.program_id(1) == 0)
    def _():
        o_ref[...] = jnp.zeros_like(o_ref)

    hb = h_ref[...]
    gates = g_ref[...]
    lane = _iota(gates.shape, 1)
    hids = []
    for e in range(EXPERTS_PER_STEP):
        gcol = jnp.sum(jnp.where(lane == e0 + e, gates, 0.0), axis=-1, keepdims=True)
        hid = _silu(_dot(hb, w1_ref[e])) * _dot(hb, w3_ref[e])
        hids.append((hid * gcol).astype(BF16))
    o_ref[...] += _dot(jnp.concatenate(hids, axis=1), w2_ref[0])


def _moe(h2, gates, w1, w3, w2g):
    n, d = h2.shape
    tm = n // 8 if n % (8 * 128) == 0 else 1024
    ne, _, f = w1.shape
    eps = EXPERTS_PER_STEP
    return pl.pallas_call(
        _moe_body,
        grid=(n // tm, ne // eps),
        in_specs=[pl.BlockSpec((tm, d), lambda i, g: (i, 0)),
                  pl.BlockSpec((tm, 128), lambda i, g: (i, 0)),
                  pl.BlockSpec((eps, d, f), lambda i, g: (g, 0, 0)),
                  pl.BlockSpec((eps, d, f), lambda i, g: (g, 0, 0)),
                  pl.BlockSpec((1, eps * f, d), lambda i, g: (g, 0, 0))],
        out_specs=pl.BlockSpec((tm, d), lambda i, g: (i, 0)),
        out_shape=jax.ShapeDtypeStruct((n, d), F32),
        compiler_params=_cparams(("arbitrary", "arbitrary")),
        name="moe_experts",
    )(h2, gates, w1, w3, w2g)


def _final_body(x1_ref, h_ref, rt_ref, m_ref, s13_ref, s2_ref, l2w_ref, l2b_ref, o_ref):
    m = m_ref[0, 0]
    a = _dot(h_ref[0], s13_ref[...])
    fs = a.shape[1] // 2
    hid = (_silu(a[:, :fs]) * a[:, fs:]).astype(BF16)
    f = rt_ref[0] + _dot(hid, s2_ref[...])
    o_ref[0] = _layernorm(DN_ALPHA * x1_ref[0] + m[5:6] * f) * l2w_ref[...] + l2b_ref[...]


def _final(x1, h, routed, msel, s13, s2, l2w, l2b, latent_only):
    b, t, d = x1.shape
    nb = t // TOK_BLOCK
    skip = N_CTX // TOK_BLOCK if latent_only else 0
    tok = pl.BlockSpec((1, TOK_BLOCK, d), lambda i, j: (i, j + skip, 0))
    c2 = lambda i, j: (0, 0)
    return pl.pallas_call(
        _final_body,
        grid=(b, nb - skip),
        in_specs=[tok, tok, tok,
                  pl.BlockSpec((1, 1, N_MOD, d), lambda i, j: (i, jnp.minimum(j + skip, 1), 0, 0)),
                  pl.BlockSpec(s13.shape, c2), pl.BlockSpec(s2.shape, c2),
                  pl.BlockSpec((1, d), c2), pl.BlockSpec((1, d), c2)],
        out_specs=pl.BlockSpec((1, TOK_BLOCK, d), lambda i, j: (i, j, 0)),
        out_shape=jax.ShapeDtypeStruct((b, t - skip * TOK_BLOCK, d), F32),
        compiler_params=_cparams(("arbitrary", "arbitrary")),
        name="shared_ffn_ln",
    )(x1, h, routed, msel, s13, s2, l2w, l2b)


def _prep_w_in(w):
    z, xbc, dt, rw, gq, gk, gv, gd, og = jnp.split(
        w, np.cumsum([512, 1024, 16, 1152, 128, 128, 256, 32, 256])[:-1].tolist(), axis=1)
    small = jnp.concatenate([dt, gd, jnp.zeros((w.shape[0], 128 - 48), w.dtype)], axis=1)
    return jnp.concatenate([z, xbc, rw, gq, gk, gv, og, small], axis=1).astype(BF16)


def _block_diag2(m):
    r, c = m.shape[1:]
    zero = jnp.zeros((r, c), m.dtype)
    return jnp.concatenate([jnp.concatenate([m[0], zero], axis=1),
                            jnp.concatenate([zero, m[1]], axis=1)], axis=0)


def _pad_lanes(v, n=128):
    v = v.reshape(1, -1)
    return jnp.pad(v, ((0, 0), (0, n - v.shape[1])))


def _token_mixer(xs, msel, l, p):
    z, xbc, rw, gq, gk, gv, og, small = _inproj(xs, msel, _prep_w_in(p["w_in"][l]))
    ssd_y = _ssd(xbc, small, p["ssd_conv_w"][l], p["ssd_conv_b"][l].reshape(1, -1),
                 _pad_lanes(p["ssd_dt_bias"][l]), _pad_lanes(p["ssd_a_log"][l]),
                 jnp.repeat(p["ssd_d"][l], SSD_HEAD_DIM).reshape(1, -1))
    gu_pad = jnp.pad(p["gla_gu"][l], ((0, 0), (0, 128 - GLA_GATE_LORA), (0, 0)))
    gla_f, gla_b = _gla(gq, gk, gv, small, gu_pad, p["gla_gb"][l])
    pre = dict(zip(_RWKV_OUTS, _rwkv_prep(
        rw, p["rwkv_mu"][l], p["rwkv_w0"][l].reshape(1, -1), _block_diag2(p["rwkv_w2"][l]).astype(BF16),
        p["rwkv_a0"][l].reshape(1, -1), _block_diag2(p["rwkv_a2"][l]).astype(BF16),
        p["rwkv_g2"][l].astype(BF16), p["rwkv_kk"][l].reshape(1, -1), p["rwkv_ka"][l].reshape(1, -1),
        p["rwkv_rk"][l].reshape(1, -1))))
    rf, rb = _rwkv_scan(pre)
    return z, og, ssd_y, gla_f, gla_b, rf, rb, pre["bonus"], pre["gate"]


def kernel(x, c, ctx, c_ctx, ada_w, ada_b, w_in, ssd_conv_w, ssd_conv_b, ssd_dt_bias, ssd_a_log, ssd_d, ssd_norm_w, rwkv_mu, rwkv_w0, rwkv_w2, rwkv_a0, rwkv_a2, rwkv_g2, rwkv_kk, rwkv_ka, rwkv_rk, rwkv_lnx_w, rwkv_lnx_b, gla_gu, gla_gb, gla_norm_w, w_out, ln1_w, ln1_b, ln2_w, ln2_b, router_w, router_b, exp_w1, exp_w3, exp_w2, sh_w1, sh_w3, sh_w2):
    p = dict(w_in=w_in, ssd_conv_w=ssd_conv_w, ssd_conv_b=ssd_conv_b, ssd_dt_bias=ssd_dt_bias, ssd_a_log=ssd_a_log,
             ssd_d=ssd_d, rwkv_mu=rwkv_mu, rwkv_w0=rwkv_w0, rwkv_w2=rwkv_w2, rwkv_a0=rwkv_a0, rwkv_a2=rwkv_a2,
             rwkv_g2=rwkv_g2, rwkv_kk=rwkv_kk, rwkv_ka=rwkv_ka, rwkv_rk=rwkv_rk, gla_gu=gla_gu, gla_gb=gla_gb)
    bsz, seq, d = x.shape
    n_ctx = ctx.shape[1]
    assert n_ctx == N_CTX and d == D_MODEL and seq % TOK_BLOCK == 0 and seq // GRID_W >= 8
    depth = ada_w.shape[0]
    row = lambda a: a.reshape(1, -1)

    cond = jnp.concatenate([c, c_ctx[None], jnp.zeros((8 - bsz - 1, d), F32)], axis=0)
    mods = _ada_all(cond, ada_w, ada_b).reshape(depth, 8, N_MOD, d)
    xs = jnp.concatenate([ctx, x], axis=1)
    t = xs.shape[1]
    for l in range(depth):
        msel = jnp.stack([jnp.broadcast_to(mods[l, bsz], (bsz, N_MOD, d)), mods[l, :bsz]], axis=1)
        z, og, ssd_y, gla_f, gla_b, rf, rb, bonus, gate = _token_mixer(xs, msel, l, p)
        x1, h, gates = _outproj(xs, msel, ssd_y, z, rf, rb, bonus, gate, gla_f, gla_b, og,
                                row(ssd_norm_w[l]), row(rwkv_lnx_w[l]), row(rwkv_lnx_b[l]), row(gla_norm_w[l]),
                                w_out[l].astype(BF16), row(ln1_w[l]), row(ln1_b[l]),
                                router_w[l].T, jnp.broadcast_to(router_b[l][:, None], (N_EXPERTS, 128)))
        routed = _moe(h.reshape(bsz * t, d), gates.reshape(bsz * t, 128),
                      exp_w1[l].astype(BF16), exp_w3[l].astype(BF16),
                      exp_w2[l].astype(BF16).reshape(N_EXPERTS // EXPERTS_PER_STEP, -1, d))
        s13 = jnp.concatenate([sh_w1[l], sh_w3[l]], axis=1).astype(BF16)
        xs = _final(x1, h, routed.reshape(bsz, t, d), msel, s13, sh_w2[l].astype(BF16), row(ln2_w[l]), row(ln2_b[l]),
                    latent_only=(l == depth - 1))
    return xs
```

```python
import functools

import jax
import jax.numpy as jnp
import numpy as np
from jax import lax
from jax.experimental import pallas as pl
from jax.experimental.pallas import tpu as pltpu

F32 = jnp.float32
BF16 = jnp.bfloat16

D_MODEL = 1024
N_CTX = 256
GRID_W = 64
N_MOD = 6
LN_EPS = 1e-5
DEPTH = 4
DN_ALPHA = (2 * DEPTH) ** 0.25

SSD_HEADS = 8
SSD_HEAD_DIM = 64
SSD_INNER = 512
SSD_STATE = 128
SSD_XBC = 1024

RWKV_WIDTH = 256
RWKV_HEAD_DIM = 64
RWKV_COLS = 1152
RWKV_LNX_EPS = 64e-5

GLA_K_WIDTH = 128
GLA_V_WIDTH = 256
GLA_KEY_DIM = 32
GLA_VAL_DIM = 64
GLA_GATE_LORA = 16
GLA_GATE_NORMALIZER = 16.0
GLA_SUB = 64
GLA_BLK = 16

N_EXPERTS = 64
TOP_K = 8
N_GROUPS = 8
TOPK_GROUPS = 4
EXPERT_DIM = 256
ROUTED_SCALE = 2.5
EXPERTS_PER_STEP = 4

TOK_BLOCK = 256
SCAN_BLOCK = 64
V7X_VMEM_LIMIT = 56 * 1024 * 1024

_IN_PIECES = (("z", 512), ("xbc", 1024), ("rw", 1152), ("gq", 128), ("gk", 128), ("gv", 256), ("og", 256),
              ("small", 128))
IN_PAD = sum(w for _, w in _IN_PIECES)


def _cparams(sem):
    return pltpu.CompilerParams(dimension_semantics=sem, vmem_limit_bytes=V7X_VMEM_LIMIT)


def _split3(a):
    hi = a.astype(BF16)
    r1 = a - hi.astype(F32)
    mid = r1.astype(BF16)
    lo = (r1 - mid.astype(F32)).astype(BF16)
    return hi, mid, lo


def _dot(a, b):
    return jnp.dot(a, b, preferred_element_type=F32)


def _dot_nt(a, b):
    return lax.dot_general(a, b, (((1,), (1,)), ((), ())), preferred_element_type=F32)


def _dot_x_exact(a, e):
    hi, mid, lo = _split3(a)
    return _dot(hi, e) + (_dot(mid, e) + _dot(lo, e))


def _dot_exact_x(e, a):
    hi, mid, lo = _split3(a)
    return _dot(e, hi) + (_dot(e, mid) + _dot(e, lo))


def _dot_hp(a, b):
    ah, am, _ = _split3(a)
    bh, bm, _ = _split3(b)
    return _dot(ah, bh) + (_dot(ah, bm) + _dot(am, bh))


def _sigmoid(x):
    return 1.0 / (1.0 + jnp.exp(-x))


def _silu(x):
    return x * _sigmoid(x)


def _softplus(x):
    return jnp.maximum(x, 0.0) + jnp.log(1.0 + jnp.exp(-jnp.abs(x)))


def _layernorm(x):
    mu = jnp.mean(x, axis=-1, keepdims=True)
    xc = x - mu
    var = jnp.mean(xc * xc, axis=-1, keepdims=True)
    return xc * lax.rsqrt(var + LN_EPS)


def _iota(shape, dim):
    return lax.broadcasted_iota(jnp.int32, shape, dim)


def _tri(n, upper):
    r = _iota((n, n), 0)
    c = _iota((n, n), 1)
    return jnp.where(upper, c - r, r - c) >= 0


def _group_ones(n, g):
    r = _iota((n, n), 0) // g
    c = _iota((n, n), 1) // g
    return (r == c).astype(BF16)


def _run_interleaved(gens):
    results = [None] * len(gens)
    active = list(range(len(gens)))
    while active:
        for i in list(active):
            try:
                next(gens[i])
            except StopIteration as stop:
                results[i] = stop.value
                active.remove(i)
    return results


def _ada_body(c_ref, w_ref, b_ref, o_ref):
    a = _silu(c_ref[...]).astype(BF16)
    o_ref[0] = _dot(a, w_ref[0].astype(BF16)) + b_ref[0]


def _ada_all(cond, ada_w, ada_b):
    nl, d, n = ada_w.shape
    tn = 1536
    rows = cond.shape[0]
    return pl.pallas_call(
        _ada_body,
        grid=(nl, n // tn),
        in_specs=[pl.BlockSpec((rows, d), lambda l, k: (0, 0)),
                  pl.BlockSpec((1, d, tn), lambda l, k: (l, 0, k)),
                  pl.BlockSpec((1, 1, tn), lambda l, k: (l, 0, k))],
        out_specs=pl.BlockSpec((1, rows, tn), lambda l, k: (l, 0, k)),
        out_shape=jax.ShapeDtypeStruct((nl, rows, n), F32),
        compiler_params=_cparams(("arbitrary", "arbitrary")),
        name="ada_mod",
    )(cond, ada_w, ada_b.reshape(nl, 1, n))


def _inproj_store(x, m, w_ref, out_refs):
    u = _layernorm(x) * (1.0 + m[1:2]) + m[0:1]
    y = _dot(u.astype(BF16), w_ref[...])
    off = 0
    for (_, wdt), o in zip(_IN_PIECES, out_refs):
        o[0] = y[:, off:off + wdt]
        off += wdt


def _inproj_body(x_ref, m_ref, w_ref, *out_refs):
    _inproj_store(x_ref[0], m_ref[0, 0], w_ref, out_refs)


def _inproj(xs, msel, w_in_p):
    b, t, d = xs.shape
    nb = t // TOK_BLOCK
    out_shape = [jax.ShapeDtypeStruct((b, t, w), F32) for _, w in _IN_PIECES]
    out_specs = [pl.BlockSpec((1, TOK_BLOCK, w), lambda i, j: (i, j, 0)) for _, w in _IN_PIECES]
    return pl.pallas_call(
        _inproj_body,
        grid=(b, nb),
        in_specs=[pl.BlockSpec((1, TOK_BLOCK, d), lambda i, j: (i, j, 0)),
                  pl.BlockSpec((1, 1, N_MOD, d), lambda i, j: (i, jnp.minimum(j, 1), 0, 0)),
                  pl.BlockSpec((d, IN_PAD), lambda i, j: (0, 0))],
        out_specs=out_specs,
        out_shape=out_shape,
        compiler_params=_cparams(("arbitrary", "arbitrary")),
        name="in_proj",
    )(xs, msel, w_in_p)


def _ssd_body(nbatch, nb, fwd_refs, bwd_refs, cw_ref, cb_ref, dtb_ref, alog_ref, dsk_ref, yf_ref, yb_ref,
              h_scr, buf_scr):
    q = TOK_BLOCK
    j = pl.program_id(0)

    @pl.when(j == 0)
    def _():
        h_scr[...] = jnp.zeros_like(h_scr)

    lane = _iota((1, 128), 1)
    neg_a = jnp.where(lane < 2 * SSD_HEADS, -jnp.exp(alog_ref[...]), 0.0)
    cw = cw_ref[0]
    er = _iota((128, SSD_INNER), 0)
    ec = _iota((128, SSD_INNER), 1)
    er8 = _iota((128, 128), 0)
    ec8 = _iota((128, 128), 1)
    lane_q = _iota((q, 128), 1)
    hpg = SSD_HEADS // 2

    def stream(refs, bi, upper, si, blk):
        xbc_ref, hp_ref, hn_ref, small_ref = refs
        d = 1 if upper else 0
        seg_first = (blk == 0) | (blk == 1)
        seg_last = (blk == 0) | (blk == nb - 1)
        cur = xbc_ref[bi]
        buf_scr[8:8 + q] = cur
        buf_scr[0:8] = jnp.where(seg_first, 0.0, hp_ref[bi])
        buf_scr[8 + q:16 + q] = jnp.where(seg_last, 0.0, hn_ref[bi])
        prev = buf_scr[7:7 + q]
        nxt = buf_scr[9:9 + q]
        act = _silu(prev * cw[0:1] + cur * cw[1:2] + nxt * cw[2:3] + cb_ref[...])
        xs = act[:, :SSD_INNER]
        b_grp = [act[:, 512 + 128 * g:640 + 128 * g] for g in range(2)]
        c_grp = [act[:, 768 + 128 * g:896 + 128 * g].astype(BF16) for g in range(2)]

        tri_b = _tri(q, upper)
        tri = tri_b.astype(BF16)
        tri_t = _tri(q, not upper).astype(BF16)
        dt_all = _softplus(small_ref[bi] + dtb_ref[...])
        la_all = dt_all * neg_a
        e_exp = (er == SSD_HEADS * d + ec // SSD_HEAD_DIM).astype(BF16)
        e8 = ((er8 == SSD_HEADS * d + ec8) & (ec8 < SSD_HEADS)).astype(BF16)
        dt_exp = _dot_x_exact(dt_all, e_exp)
        la_exp = _dot_x_exact(la_all, e_exp)
        la8 = _dot_x_exact(la_all, e8)
        cs_exp = _dot_exact_x(tri, la_exp)
        cs8 = _dot_exact_x(tri, la8)
        cs_t = _dot_x_exact(la8.T, tri_t)
        tot_exp = jnp.sum(la_exp, axis=0, keepdims=True)

        xdt = xs * dt_exp
        gmats = [_dot_nt(c_grp[g], b_grp[g].astype(BF16)) for g in range(2)]
        ys = []
        for p in range(SSD_HEADS // 2):
            xdt_p = xdt[:, 128 * p:128 * (p + 1)].astype(BF16)
            halves = []
            for h in (2 * p, 2 * p + 1):
                lmat = jnp.where(tri_b, jnp.exp(cs8[:, h:h + 1] - cs_t[h:h + 1, :]), 0.0)
                halves.append(_dot((gmats[h // hpg] * lmat).astype(BF16), xdt_p))
            ys.append(jnp.where(lane_q < SSD_HEAD_DIM, halves[0], halves[1]))
        y = jnp.concatenate(ys, axis=1)

        xd = (xdt * jnp.exp(tot_exp - cs_exp)).astype(BF16)
        offs = []
        for g in range(2):
            hg = h_scr[si, g]
            offs.append(_dot(c_grp[g], hg.astype(BF16)))
            hn = _dot(b_grp[g].T.astype(BF16), xd[:, 256 * g:256 * (g + 1)])
            h_scr[si, g] = hg * jnp.exp(tot_exp[:, 256 * g:256 * (g + 1)]) + hn
        y = y + jnp.concatenate(offs, axis=1) * jnp.exp(cs_exp)
        if not upper:
            y = y + dsk_ref[...] * xs
        return y

    blk_b = jnp.where(j == 0, 0, nb - j)
    for bi in range(nbatch):
        yf_ref[bi] = stream(fwd_refs, bi, False, bi, j)
        yb_ref[bi] = stream(bwd_refs, bi, True, nbatch + bi, blk_b)


def _ssd(xbc, small, conv_w, conv_b, dtb_pad, alog_pad, dskip_exp):
    b, t, _ = xbc.shape
    nb = t // TOK_BLOCK
    r8 = TOK_BLOCK // 8

    def bwd_blk(j):
        return jnp.where(j == 0, 0, nb - j)

    def specs(blk):
        return [pl.BlockSpec((b, TOK_BLOCK, SSD_XBC), lambda j: (0, blk(j), 0)),
                pl.BlockSpec((b, 8, SSD_XBC), lambda j: (0, jnp.maximum(blk(j) * r8 - 1, 0), 0)),
                pl.BlockSpec((b, 8, SSD_XBC), lambda j: (0, jnp.minimum(blk(j) * r8 + r8, t // 8 - 1), 0)),
                pl.BlockSpec((b, TOK_BLOCK, 128), lambda j: (0, blk(j), 0))]

    def body(*refs):
        _ssd_body(b, nb, refs[0:4], refs[4:8], *refs[8:])

    const2 = lambda j: (0, 0)
    oshape = jax.ShapeDtypeStruct((b, t, SSD_INNER), F32)
    return pl.pallas_call(
        body,
        grid=(nb,),
        in_specs=specs(lambda j: j) + specs(bwd_blk) + [
            pl.BlockSpec((1, 3, SSD_XBC), lambda j: (0, 0, 0)),
            pl.BlockSpec((1, SSD_XBC), const2),
            pl.BlockSpec((1, 128), const2),
            pl.BlockSpec((1, 128), const2),
            pl.BlockSpec((1, SSD_INNER), const2)],
        out_specs=[pl.BlockSpec((b, TOK_BLOCK, SSD_INNER), lambda j: (0, j, 0)),
                   pl.BlockSpec((b, TOK_BLOCK, SSD_INNER), lambda j: (0, bwd_blk(j), 0))],
        out_shape=[oshape, oshape],
        scratch_shapes=[pltpu.VMEM((2 * b, 2, SSD_STATE, 256), F32),
                        pltpu.VMEM((TOK_BLOCK + 16, SSD_XBC), F32)],
        compiler_params=_cparams(("arbitrary",)),
        name="ssd",
    )(xbc, xbc, xbc, small, xbc, xbc, xbc, small, conv_w[None], conv_b, dtb_pad, alog_pad, dskip_exp)


def _gla_body(nbatch, fwd_refs, bwd_refs, gu_ref, gb_ref, of_ref, ob_ref, s_scr):
    j = pl.program_id(0)
    c = GLA_SUB
    nsub = TOK_BLOCK // c

    @pl.when(j == 0)
    def _():
        s_scr[...] = jnp.zeros_like(s_scr)

    lr = _iota((128, 128), 0)
    lc = _iota((128, 128), 1)
    lane_k = _iota((c, GLA_K_WIDTH), 1) // GLA_KEY_DIM
    lane_kx = _iota((GLA_BLK * (c // GLA_BLK) * (c // GLA_BLK - 1) // 2, GLA_K_WIDTH), 1) // GLA_KEY_DIM
    lane_v = _iota((c, GLA_V_WIDTH), 1) // GLA_VAL_DIM
    st_mask = (_iota((GLA_V_WIDTH, GLA_K_WIDTH), 0) // GLA_VAL_DIM
               == _iota((GLA_V_WIDTH, GLA_K_WIDTH), 1) // GLA_KEY_DIM)

    nblk = c // GLA_BLK
    tok_r = _iota((c, c), 0)
    tok_c = _iota((c, c), 1)
    col_blk = _iota((GLA_BLK, c), 1) // GLA_BLK

    def direction_consts(upper):
        first = (tok_r // GLA_BLK) * GLA_BLK + (GLA_BLK - 1 if upper else 0)
        last = (tok_r // GLA_BLK) * GLA_BLK + (0 if upper else GLA_BLK - 1)
        pairs = [(ti, tj) for ti in range(nblk) for tj in range(nblk) if (tj > ti if upper else tj < ti)]
        return dict(tri_b=_tri(c, upper), sel_first=(tok_c == first).astype(BF16),
                    sel_last=(tok_c == last).astype(BF16), pairs=pairs)

    consts = {False: direction_consts(False), True: direction_consts(True)}

    def stream(refs, bi, upper, st):
        q_ref, k_ref, v_ref, small_ref = refs
        d = 1 if upper else 0
        cst = consts[upper]
        tri_b = cst["tri_b"]
        tri = tri_b.astype(BF16)
        gsel = ((lr == 2 * SSD_HEADS + GLA_GATE_LORA * d + lc) & (lc < GLA_GATE_LORA)).astype(BF16)
        gd = _dot_x_exact(small_ref[bi], gsel)
        yield
        pre = _dot_hp(gd, gu_ref[d]) + gb_ref[d:d + 1]
        la_all = -_softplus(-pre) * (1.0 / GLA_GATE_NORMALIZER)
        yield
        outs = [None] * nsub
        for si in range(nsub):
            sub = nsub - 1 - si if upper else si
            lo = sub * c
            la = la_all[lo:lo + c]
            qq = q_ref[bi, lo:lo + c, :] * (GLA_KEY_DIM ** -0.5)
            kk = k_ref[bi, lo:lo + c, :]
            vv = v_ref[bi, lo:lo + c, :]
            cs = _dot_exact_x(tri, la)
            yield
            rs = _dot_exact_x(cst["sel_first"], cs - la)
            re = _dot_exact_x(cst["sel_last"], cs)
            yield
            tot = jnp.sum(la, axis=0, keepdims=True)
            qt = qq * jnp.exp(cs)
            kdec = (kk * jnp.exp(tot - cs)).astype(BF16)
            q_in = qq * jnp.exp(cs - rs)
            k_in = (kk * jnp.exp(jnp.minimum(rs - cs, 80.0))).astype(BF16)
            k_out = (kk * jnp.exp(re - cs)).astype(BF16)
            q_cross = jnp.concatenate(
                [q_in[GLA_BLK * ti:GLA_BLK * (ti + 1)]
                 * jnp.exp(rs[GLA_BLK * ti:GLA_BLK * ti + 1] - re[GLA_BLK * tj:GLA_BLK * tj + 1])
                 for ti, tj in cst["pairs"]], axis=0)
            vb = vv.astype(BF16)
            o = _dot_nt(qt.astype(BF16), st.astype(BF16))
            amats = []
            for h in range(GLA_K_WIDTH // GLA_KEY_DIM):
                r_in = _dot_nt(jnp.where(lane_k == h, q_in, 0.0).astype(BF16), k_in)
                r_cross = _dot_nt(jnp.where(lane_kx == h, q_cross, 0.0).astype(BF16), k_out)
                rows = []
                for ti in range(nblk):
                    blk_rows = slice(GLA_BLK * ti, GLA_BLK * (ti + 1))
                    part = jnp.where(tri_b[blk_rows] & (col_blk == ti), r_in[blk_rows], 0.0)
                    for pi, (pti, ptj) in enumerate(cst["pairs"]):
                        if pti == ti:
                            part = part + jnp.where(col_blk == ptj, r_cross[GLA_BLK * pi:GLA_BLK * (pi + 1)], 0.0)
                    rows.append(part)
                amats.append(jnp.concatenate(rows, axis=0).astype(BF16))
            upd = _dot(vv.T.astype(BF16), kdec)
            yield
            for h, a in enumerate(amats):
                o = o + jnp.where(lane_v == h, _dot(a, vb), 0.0)
            st = st * jnp.exp(tot) + jnp.where(st_mask, upd, 0.0)
            outs[sub] = o
            yield
        return jnp.concatenate(outs, axis=0), st

    gens = ([stream(fwd_refs, bi, False, s_scr[bi]) for bi in range(nbatch)]
            + [stream(bwd_refs, bi, True, s_scr[nbatch + bi]) for bi in range(nbatch)])
    for si, (o, st) in enumerate(_run_interleaved(gens)):
        if si < nbatch:
            of_ref[si] = o
        else:
            ob_ref[si - nbatch] = o
        s_scr[si] = st


def _gla(gq, gk, gv, small, gu_pad, gb):
    b, t, _ = gq.shape
    nb = t // TOK_BLOCK

    def bwd_blk(j):
        return jnp.where(j == 0, 0, nb - j)

    def specs(blk):
        return [pl.BlockSpec((b, TOK_BLOCK, GLA_K_WIDTH), lambda j: (0, blk(j), 0)),
                pl.BlockSpec((b, TOK_BLOCK, GLA_K_WIDTH), lambda j: (0, blk(j), 0)),
                pl.BlockSpec((b, TOK_BLOCK, GLA_V_WIDTH), lambda j: (0, blk(j), 0)),
                pl.BlockSpec((b, TOK_BLOCK, 128), lambda j: (0, blk(j), 0))]

    def body(*refs):
        _gla_body(b, refs[0:4], refs[4:8], *refs[8:])

    oshape = jax.ShapeDtypeStruct((b, t, GLA_V_WIDTH), F32)
    return pl.pallas_call(
        body,
        grid=(nb,),
        in_specs=specs(lambda j: j) + specs(bwd_blk) + [
            pl.BlockSpec((2, 128, GLA_K_WIDTH), lambda j: (0, 0, 0)),
            pl.BlockSpec((2, GLA_K_WIDTH), lambda j: (0, 0))],
        out_specs=[pl.BlockSpec((b, TOK_BLOCK, GLA_V_WIDTH), lambda j: (0, j, 0)),
                   pl.BlockSpec((b, TOK_BLOCK, GLA_V_WIDTH), lambda j: (0, bwd_blk(j), 0))],
        out_shape=[oshape, oshape],
        scratch_shapes=[pltpu.VMEM((2 * b, GLA_V_WIDTH, GLA_K_WIDTH), F32)],
        compiler_params=_cparams(("arbitrary",)),
        name="gla",
    )(gq, gk, gv, small, gq, gk, gv, small, gu_pad, gb)


_RWKV_OUTS = ("r", "v", "kk", "lw_f", "k_f", "kka_f", "lw_b", "k_b", "kka_b", "bonus", "gate")


def _rwkv_prep_body(nb, rw_ref, hp_ref, hn_ref, mu_ref, w0_ref, w2_ref, a0_ref, a2_ref, g2_ref, kkp_ref, ka_ref,
                    rk_ref, *refs):
    outs = dict(zip(_RWKV_OUTS, refs[:len(_RWKV_OUTS)]))
    buf_scr = refs[len(_RWKV_OUTS)]
    q = TOK_BLOCK
    j = pl.program_id(1)
    is_ctx = j == 0
    z = rw_ref[0]
    zeros8 = jnp.zeros((8, RWKV_COLS), F32)
    buf_scr[0:8] = zeros8
    buf_scr[8 + q:16 + q] = zeros8
    buf_scr[8:8 + q] = z
    t = _iota((q, 1), 0)
    col = t % GRID_W
    prev = jnp.where(jnp.where(is_ctx, t, col) == 0, 0.0, buf_scr[7:7 + q])
    nxt = jnp.where(jnp.where(is_ctx, t - (q - 1), col - (GRID_W - 1)) == 0, 0.0, buf_scr[9:9 + q])
    up = jnp.concatenate([jnp.where(j == 1, 0.0, hp_ref[0]), z[:q - GRID_W]], axis=0)
    down = jnp.concatenate([z[GRID_W:], jnp.where(j == nb - 1, 0.0, hn_ref[0])], axis=0)
    mu = mu_ref[...]
    vert = mu[2:3] * (up - z) + mu[3:4] * (down - z)
    mixed = z + mu[0:1] * (prev - z) + mu[1:2] * (nxt - z) + jnp.where(is_ctx, 0.0, vert)

    w = RWKV_WIDTH
    r = mixed[:, 0:w]
    k = mixed[:, w:2 * w]
    v = mixed[:, 2 * w:3 * w]
    wd = mixed[:, 3 * w:3 * w + 128]
    ad = mixed[:, 3 * w + 128:4 * w]
    gdr = mixed[:, 4 * w:4 * w + 128]

    lw = w0_ref[...] + _dot(jnp.tanh(wd).astype(BF16), w2_ref[...])
    log_dec = -jnp.exp(-_softplus(-lw) - 0.5)
    a = _sigmoid(a0_ref[...] + _dot(ad.astype(BF16), a2_ref[...]))
    ones64 = _group_ones(w, RWKV_HEAD_DIM)
    kkv = k * kkp_ref[...]
    nrm = jnp.maximum(jnp.sqrt(_dot_x_exact(kkv * kkv, ones64)), 1e-12)
    kkn = kkv / nrm
    ka = ka_ref[...]
    a_f, a_b = a[:, :w], a[:, w:]
    k_f = k * (1.0 + (a_f - 1.0) * ka)
    k_b = k * (1.0 + (a_b - 1.0) * ka)
    outs["r"][0] = r
    outs["v"][0] = v
    outs["kk"][0] = kkn
    outs["lw_f"][0] = log_dec[:, :w]
    outs["lw_b"][0] = log_dec[:, w:]
    outs["k_f"][0] = k_f
    outs["k_b"][0] = k_b
    outs["kka_f"][0] = kkn * a_f
    outs["kka_b"][0] = kkn * a_b
    outs["bonus"][0] = _dot_x_exact(r * (k_f + k_b) * rk_ref[...], ones64) * v
    outs["gate"][0] = _dot(_sigmoid(gdr).astype(BF16), g2_ref[...])


def _rwkv_prep(rw, mu, w0cat, w2bd, a0cat, a2bd, g2, kkp, ka, rk):
    b, t, _ = rw.shape
    nb = t // TOK_BLOCK
    hb = TOK_BLOCK // GRID_W
    w = RWKV_WIDTH
    c2 = lambda i, j: (0, 0)
    return pl.pallas_call(
        functools.partial(_rwkv_prep_body, nb),
        grid=(b, nb),
        in_specs=[pl.BlockSpec((1, TOK_BLOCK, RWKV_COLS), lambda i, j: (i, j, 0)),
                  pl.BlockSpec((1, GRID_W, RWKV_COLS), lambda i, j: (i, jnp.maximum(j * hb - 1, 0), 0)),
                  pl.BlockSpec((1, GRID_W, RWKV_COLS),
                               lambda i, j: (i, jnp.minimum(j * hb + hb, t // GRID_W - 1), 0)),
                  pl.BlockSpec((4, RWKV_COLS), c2),
                  pl.BlockSpec((1, 2 * w), c2),
                  pl.BlockSpec((128, 2 * w), c2),
                  pl.BlockSpec((1, 2 * w), c2),
                  pl.BlockSpec((128, 2 * w), c2),
                  pl.BlockSpec((128, w), c2),
                  pl.BlockSpec((1, w), c2),
                  pl.BlockSpec((1, w), c2),
                  pl.BlockSpec((1, w), c2)],
        out_specs=[pl.BlockSpec((1, TOK_BLOCK, w), lambda i, j: (i, j, 0)) for _ in _RWKV_OUTS],
        out_shape=[jax.ShapeDtypeStruct((b, t, w), F32) for _ in _RWKV_OUTS],
        scratch_shapes=[pltpu.VMEM((TOK_BLOCK + 16, RWKV_COLS), F32)],
        compiler_params=_cparams(("arbitrary", "arbitrary")),
        name="rwkv_prep",
    )(rw, rw, rw, mu, w0cat, w2bd, a0cat, a2bd, g2, kkp, ka, rk)


def _rwkv_scan_body(nbatch, fwd_refs, bwd_refs, yf_ref, yb_ref, s_scr):
    j = pl.program_id(0)
    n = SCAN_BLOCK
    hd = RWKV_HEAD_DIM

    @pl.when(j == 0)
    def _():
        s_scr[...] = jnp.zeros_like(s_scr)

    nh = RWKV_WIDTH // hd
    w = RWKV_WIDTH
    lane_head = _iota((1, w), 1) // hd
    row_h = _iota((nh * n, nh * n), 0) // n
    row_c = _iota((nh * n, nh * n), 0) % n
    col_h = _iota((nh * n, nh * n), 1) // n
    col_c = _iota((nh * n, nh * n), 1) % n
    cat_c = _iota((n, nh * n), 0)
    cat_cc = _iota((n, nh * n), 1) % n
    eye = (_iota((nh * n, nh * n), 0) == _iota((nh * n, nh * n), 1)).astype(F32)

    def stack(x):
        return jnp.concatenate([jnp.where(lane_head == h, x, 0.0) for h in range(nh)], axis=0)

    def chunk(refs, bi, upper, s_prev):
        r, v, kk, lw, k, kka = (ref[bi] for ref in refs)
        tri = _tri(n, upper).astype(BF16)
        before = (col_c > row_c) if upper else (col_c < row_c)
        strict = (row_h == col_h) & before
        incl = (cat_cc >= cat_c) if upper else (cat_cc <= cat_c)
        cs = _dot_exact_x(tri, lw)
        yield
        tot = jnp.sum(lw, axis=0, keepdims=True)
        g_inv = jnp.exp(-cs)
        dec_end = jnp.exp(tot - cs)
        a_t = -kk * jnp.exp(cs - lw)
        r_t = (r * jnp.exp(cs)).astype(BF16)
        a_s = stack(a_t).astype(BF16)
        bk_s = jnp.concatenate([stack(kka * g_inv), stack(k * g_inv)], axis=0).astype(BF16)
        bk_dec = jnp.concatenate([stack(kka * dec_end), stack(k * dec_end)], axis=0).astype(BF16)
        v_s = stack(v).astype(BF16)
        x1 = _dot_nt(a_s, bk_s)
        x2 = _dot_nt(r_t, bk_s)
        yield
        nmat = jnp.where(strict, x1[:, :nh * n], 0.0)
        ak = jnp.where(strict, x1[:, nh * n:], 0.0).astype(BF16)
        rb = jnp.where(incl, x2[:, :nh * n], 0.0).astype(BF16)
        rk = jnp.where(incl, x2[:, nh * n:], 0.0).astype(BF16)
        s_b = s_prev.astype(BF16)
        w0 = _dot_nt(a_s, s_b) + _dot(ak, v_s)
        y0 = _dot_nt(r_t, s_b) + _dot(rk, v_s)
        tinv = eye + nmat
        p = nmat.astype(BF16)
        for i in range(n.bit_length() - 2):
            p2 = _dot(p, p)
            yield
            tinv = tinv + _dot(tinv.astype(BF16), p2.astype(BF16))
            p = p2.astype(BF16)
            yield
        u = _dot(tinv.astype(BF16), w0.astype(BF16)).astype(BF16)
        yield
        y = y0 + _dot(rb, u)
        uv = jnp.concatenate([u, v_s], axis=0)
        s_new = s_prev * jnp.exp(tot) + lax.dot_general(uv, bk_dec, (((0,), (0,)), ((), ())),
                                                        preferred_element_type=F32)
        return y, s_new

    gens = ([chunk(fwd_refs, bi, False, s_scr[bi]) for bi in range(nbatch)]
            + [chunk(bwd_refs, bi, True, s_scr[nbatch + bi]) for bi in range(nbatch)])
    for si, (y, s_new) in enumerate(_run_interleaved(gens)):
        if si < nbatch:
            yf_ref[si] = y
        else:
            yb_ref[si - nbatch] = y
        s_scr[si] = s_new


def _rwkv_scan(pre):
    b, t, w = pre["r"].shape
    nbs = t // SCAN_BLOCK
    ctx_blocks = N_CTX // SCAN_BLOCK

    def fwd_blk(j):
        return j

    def bwd_blk(j):
        return jnp.where(j < ctx_blocks, ctx_blocks - 1 - j, nbs + ctx_blocks - 1 - j)

    in_f = pl.BlockSpec((b, SCAN_BLOCK, w), lambda j: (0, fwd_blk(j), 0))
    in_b = pl.BlockSpec((b, SCAN_BLOCK, w), lambda j: (0, bwd_blk(j), 0))
    oshape = jax.ShapeDtypeStruct((b, t, w), F32)

    def body(*refs):
        _rwkv_scan_body(b, refs[0:6], refs[6:12], refs[12], refs[13], refs[14])

    return pl.pallas_call(
        body,
        grid=(nbs,),
        in_specs=[in_f] * 6 + [in_b] * 6,
        out_specs=[in_f, in_b],
        out_shape=[oshape, oshape],
        scratch_shapes=[pltpu.VMEM((2 * b, w, w), F32)],
        compiler_params=_cparams(("arbitrary",)),
        name="rwkv_scan",
    )(pre["r"], pre["v"], pre["kk"], pre["lw_f"], pre["k_f"], pre["kka_f"],
      pre["r"], pre["v"], pre["kk"], pre["lw_b"], pre["k_b"], pre["kka_b"])


def _outproj_body(x_ref, m_ref, s0_ref, s1_ref, z_ref, rf_ref, rb_ref, bonus_ref, gate_ref, g0_ref, g1_ref, og_ref,
                  snw_ref, lxw_ref, lxb_ref, gnw_ref, wo_ref, l1w_ref, l1b_ref, rtw_ref, rtb_ref,
                  x1_ref, h_ref, g_ref):
    m = m_ref[0, 0]
    ys = (s0_ref[0] + s1_ref[0]) * _silu(z_ref[0])
    gw = SSD_INNER // 2
    parts = []
    for g in range(2):
        part = ys[:, gw * g:gw * (g + 1)]
        parts.append(part * lax.rsqrt(jnp.mean(part * part, axis=-1, keepdims=True) + 1e-5))
    y_ssd = jnp.concatenate(parts, axis=1) * snw_ref[...]
    ones64 = _group_ones(RWKV_WIDTH, RWKV_HEAD_DIM)
    inv = 1.0 / RWKV_HEAD_DIM
    yr = rf_ref[0] + rb_ref[0]
    mu = _dot_x_exact(yr, ones64) * inv
    yc = yr - mu
    var = _dot_x_exact(yc * yc, ones64) * inv
    yn = yc * lax.rsqrt(var + RWKV_LNX_EPS) * lxw_ref[...] + lxb_ref[...]
    y_rwkv = (yn + bonus_ref[0]) * gate_ref[0]
    o = g0_ref[0] + g1_ref[0]
    ms = _dot_x_exact(o * o, ones64) * (1.0 / GLA_VAL_DIM)
    y_gla = o * lax.rsqrt(ms + 1e-5) * gnw_ref[...] * _silu(og_ref[0])

    y_mix = jnp.concatenate([y_ssd, y_rwkv, y_gla], axis=1).astype(BF16)
    mix = _dot(y_mix, wo_ref[...])
    x1 = _layernorm(DN_ALPHA * x_ref[0] + m[2:3] * mix) * l1w_ref[...] + l1b_ref[...]
    x1_ref[0] = x1
    h = _layernorm(x1) * (1.0 + m[4:5]) + m[3:4]
    h_ref[0] = h.astype(BF16)
    g_ref[0] = _route(h, rtw_ref[...], rtb_ref[...])


def _outproj(xs, msel, ssd_f, ssd_b, z, rf, rb, bonus, gate, gla_f, gla_b, og, snw, lxw, lxb, gnw, wo, l1w, l1b, rtw, rtb):
    b, t, d = xs.shape
    nb = t // TOK_BLOCK
    tok = lambda w: pl.BlockSpec((1, TOK_BLOCK, w), lambda i, j: (i, j, 0))
    c2 = lambda i, j: (0, 0)
    row = lambda w: pl.BlockSpec((1, w), c2)
    return pl.pallas_call(
        _outproj_body,
        grid=(b, nb),
        in_specs=[tok(d),
                  pl.BlockSpec((1, 1, N_MOD, d), lambda i, j: (i, jnp.minimum(j, 1), 0, 0)),
                  tok(SSD_INNER), tok(SSD_INNER), tok(SSD_INNER),
                  tok(RWKV_WIDTH), tok(RWKV_WIDTH), tok(RWKV_WIDTH), tok(RWKV_WIDTH),
                  tok(GLA_V_WIDTH), tok(GLA_V_WIDTH), tok(GLA_V_WIDTH),
                  row(SSD_INNER), row(RWKV_WIDTH), row(RWKV_WIDTH), row(GLA_V_WIDTH),
                  pl.BlockSpec((d, d), c2), row(d), row(d),
                  pl.BlockSpec((N_EXPERTS, d), c2), pl.BlockSpec((N_EXPERTS, 128), c2)],
        out_specs=[tok(d), tok(d), tok(128)],
        out_shape=[jax.ShapeDtypeStruct((b, t, d), F32), jax.ShapeDtypeStruct((b, t, d), BF16),
                   jax.ShapeDtypeStruct((b, t, 128), F32)],
        compiler_params=_cparams(("arbitrary", "arbitrary")),
        name="out_proj_router",
    )(xs, msel, ssd_f, ssd_b, z, rf, rb, bonus, gate, gla_f, gla_b, og, snw, lxw, lxb, gnw, wo, l1w, l1b, rtw, rtb)


def _route(h, rw_t, rb_t):
    tm = h.shape[0]
    ne = N_EXPERTS
    per_group = ne // N_GROUPS
    ah, am, _ = _split3(rw_t)
    bh, bm, _ = _split3(h)
    logits = _dot_nt(ah, bh) + (_dot_nt(ah, bm) + _dot_nt(am, bh))
    scores = _sigmoid(logits)
    sel = scores + jnp.concatenate([rb_t] * (tm // 128), axis=1)
    neg = -jnp.inf
    sub8 = _iota((per_group, tm), 0).astype(F32)
    eidx = _iota((ne, tm), 0).astype(F32)

    def first_max(cur, idx, sentinel):
        mx = jnp.max(cur, axis=0, keepdims=True)
        first = jnp.min(jnp.where(cur == mx, idx, sentinel), axis=0, keepdims=True)
        return mx, idx == first

    gs = []
    for g in range(N_GROUPS):
        blk = sel[per_group * g:per_group * (g + 1)]
        m1, hit = first_max(blk, sub8, float(per_group))
        m2 = jnp.max(jnp.where(hit, neg, blk), axis=0, keepdims=True)
        gs.append(m1 + m2)
    cur = jnp.concatenate(gs, axis=0)
    allowed_g = jnp.zeros((N_GROUPS, tm), F32)
    for _ in range(TOPK_GROUPS):
        _, hit = first_max(cur, sub8, float(N_GROUPS))
        cur = jnp.where(hit, neg, cur)
        allowed_g = jnp.where(hit, 1.0, allowed_g)
    allowed = jnp.concatenate([jnp.broadcast_to(allowed_g[g:g + 1], (per_group, tm)) for g in range(N_GROUPS)],
                              axis=0)
    cur = jnp.where(allowed > 0.0, sel, neg)
    wts = jnp.zeros((ne, tm), F32)
    for _ in range(TOP_K):
        _, hit = first_max(cur, eidx, float(ne))
        cur = jnp.where(hit, neg, cur)
        wts = jnp.where(hit, scores, wts)
    gates_t = wts / jnp.sum(wts, axis=0, keepdims=True) * ROUTED_SCALE
    return jnp.concatenate([gates_t, jnp.zeros((128 - ne, tm), F32)], axis=0).T


def _moe_body(h_ref, g_ref, w1_ref, w3_ref, w2_ref, o_ref):
    e0 = pl.program_id(1) * EXPERTS_PER_STEP

    @pl.when(pl.program_id(1) == 0)
    def _():
        o_ref[...] = jnp.zeros_like(o_ref)

    hb = h_ref[...]
    gates = g_ref[...]
    lane = _iota(gates.shape, 1)
    hids = []
    for e in range(EXPERTS_PER_STEP):
        gcol = jnp.sum(jnp.where(lane == e0 + e, gates, 0.0), axis=-1, keepdims=True)
        hid = _silu(_dot(hb, w1_ref[e])) * _dot(hb, w3_ref[e])
        hids.append((hid * gcol).astype(BF16))
    o_ref[...] += _dot(jnp.concatenate(hids, axis=1), w2_ref[0])


def _moe(h2, gates, w1, w3, w2g):
    n, d = h2.shape
    tm = n // 8 if n % (8 * 128) == 0 else 1024
    ne, _, f = w1.shape
    eps = EXPERTS_PER_STEP
    return pl.pallas_call(
        _moe_body,
        grid=(n // tm, ne // eps),
        in_specs=[pl.BlockSpec((tm, d), lambda i, g: (i, 0)),
                  pl.BlockSpec((tm, 128), lambda i, g: (i, 0)),
                  pl.BlockSpec((eps, d, f), lambda i, g: (g, 0, 0)),
                  pl.BlockSpec((eps, d, f), lambda i, g: (g, 0, 0)),
                  pl.BlockSpec((1, eps * f, d), lambda i, g: (g, 0, 0))],
        out_specs=pl.BlockSpec((tm, d), lambda i, g: (i, 0)),
        out_shape=jax.ShapeDtypeStruct((n, d), F32),
        compiler_params=_cparams(("arbitrary", "arbitrary")),
        name="moe_experts",
    )(h2, gates, w1, w3, w2g)


def _ffn_residual(x1_ref, h_ref, rt_ref, m, s13_ref, s2_ref, l2w_ref, l2b_ref):
    a = _dot(h_ref[0], s13_ref[...])
    fs = a.shape[1] // 2
    hid = (_silu(a[:, :fs]) * a[:, fs:]).astype(BF16)
    f = rt_ref[0] + _dot(hid, s2_ref[...])
    return _layernorm(DN_ALPHA * x1_ref[0] + m[5:6] * f) * l2w_ref[...] + l2b_ref[...]


def _final_body(x1_ref, h_ref, rt_ref, m_ref, s13_ref, s2_ref, l2w_ref, l2b_ref, o_ref):
    o_ref[0] = _ffn_residual(x1_ref, h_ref, rt_ref, m_ref[0, 0], s13_ref, s2_ref, l2w_ref, l2b_ref)


def _final_next_body(x1_ref, h_ref, rt_ref, m_ref, s13_ref, s2_ref, l2w_ref, l2b_ref, mn_ref, w_ref, o_ref,
                     *proj_refs):
    x2 = _ffn_residual(x1_ref, h_ref, rt_ref, m_ref[0, 0], s13_ref, s2_ref, l2w_ref, l2b_ref)
    o_ref[0] = x2
    _inproj_store(x2, mn_ref[0, 0], w_ref, proj_refs)


def _final(x1, h, routed, msel, s13, s2, l2w, l2b, nxt):
    b, t, d = x1.shape
    nb = t // TOK_BLOCK
    skip = N_CTX // TOK_BLOCK if nxt is None else 0
    tok = pl.BlockSpec((1, TOK_BLOCK, d), lambda i, j: (i, j + skip, 0))
    mspec = pl.BlockSpec((1, 1, N_MOD, d), lambda i, j: (i, jnp.minimum(j + skip, 1), 0, 0))
    c2 = lambda i, j: (0, 0)
    in_specs = [tok, tok, tok, mspec, pl.BlockSpec(s13.shape, c2), pl.BlockSpec(s2.shape, c2),
                pl.BlockSpec((1, d), c2), pl.BlockSpec((1, d), c2)]
    out_tok = lambda w: pl.BlockSpec((1, TOK_BLOCK, w), lambda i, j: (i, j, 0))
    if nxt is None:
        return pl.pallas_call(
            _final_body,
            grid=(b, nb - skip),
            in_specs=in_specs,
            out_specs=out_tok(d),
            out_shape=jax.ShapeDtypeStruct((b, t - skip * TOK_BLOCK, d), F32),
            compiler_params=_cparams(("arbitrary", "arbitrary")),
            name="shared_ffn_ln",
        )(x1, h, routed, msel, s13, s2, l2w, l2b)
    msel_n, w_in_n = nxt
    outs = pl.pallas_call(
        _final_next_body,
        grid=(b, nb),
        in_specs=in_specs + [mspec, pl.BlockSpec((d, IN_PAD), c2)],
        out_specs=[out_tok(d)] + [out_tok(w) for _, w in _IN_PIECES],
        out_shape=[jax.ShapeDtypeStruct((b, t, d), F32)] + [jax.ShapeDtypeStruct((b, t, w), F32)
                                                            for _, w in _IN_PIECES],
        compiler_params=_cparams(("arbitrary", "arbitrary")),
        name="shared_ffn_ln_in_proj",
    )(x1, h, routed, msel, s13, s2, l2w, l2b, msel_n, w_in_n)
    return outs[0], outs[1:]


def _prep_w_in(w):
    z, xbc, dt, rw, gq, gk, gv, gd, og = jnp.split(
        w, np.cumsum([512, 1024, 16, 1152, 128, 128, 256, 32, 256])[:-1].tolist(), axis=1)
    small = jnp.concatenate([dt, gd, jnp.zeros((w.shape[0], 128 - 48), w.dtype)], axis=1)
    return jnp.concatenate([z, xbc, rw, gq, gk, gv, og, small], axis=1).astype(BF16)


def _block_diag2(m):
    r, c = m.shape[1:]
    zero = jnp.zeros((r, c), m.dtype)
    return jnp.concatenate([jnp.concatenate([m[0], zero], axis=1),
                            jnp.concatenate([zero, m[1]], axis=1)], axis=0)


def _pad_lanes(v, n=128):
    v = v.reshape(1, -1)
    return jnp.pad(v, ((0, 0), (0, n - v.shape[1])))


def _token_mixer(proj, l, p):
    z, xbc, rw, gq, gk, gv, og, small = proj
    ssd_f, ssd_b = _ssd(xbc, small, p["ssd_conv_w"][l], p["ssd_conv_b"][l].reshape(1, -1),
                 _pad_lanes(p["ssd_dt_bias"][l]), _pad_lanes(p["ssd_a_log"][l]),
                 jnp.repeat(p["ssd_d"][l], SSD_HEAD_DIM).reshape(1, -1))
    gu_pad = jnp.pad(p["gla_gu"][l], ((0, 0), (0, 128 - GLA_GATE_LORA), (0, 0)))
    gla_f, gla_b = _gla(gq, gk, gv, small, gu_pad, p["gla_gb"][l])
    pre = dict(zip(_RWKV_OUTS, _rwkv_prep(
        rw, p["rwkv_mu"][l], p["rwkv_w0"][l].reshape(1, -1), _block_diag2(p["rwkv_w2"][l]).astype(BF16),
        p["rwkv_a0"][l].reshape(1, -1), _block_diag2(p["rwkv_a2"][l]).astype(BF16),
        p["rwkv_g2"][l].astype(BF16), p["rwkv_kk"][l].reshape(1, -1), p["rwkv_ka"][l].reshape(1, -1),
        p["rwkv_rk"][l].reshape(1, -1))))
    rf, rb = _rwkv_scan(pre)
    return z, og, ssd_f, ssd_b, gla_f, gla_b, rf, rb, pre["bonus"], pre["gate"]


def kernel(x, c, ctx, c_ctx, ada_w, ada_b, w_in, ssd_conv_w, ssd_conv_b, ssd_dt_bias, ssd_a_log, ssd_d, ssd_norm_w, rwkv_mu, rwkv_w0, rwkv_w2, rwkv_a0, rwkv_a2, rwkv_g2, rwkv_kk, rwkv_ka, rwkv_rk, rwkv_lnx_w, rwkv_lnx_b, gla_gu, gla_gb, gla_norm_w, w_out, ln1_w, ln1_b, ln2_w, ln2_b, router_w, router_b, exp_w1, exp_w3, exp_w2, sh_w1, sh_w3, sh_w2):
    p = dict(ssd_conv_w=ssd_conv_w, ssd_conv_b=ssd_conv_b, ssd_dt_bias=ssd_dt_bias, ssd_a_log=ssd_a_log,
             ssd_d=ssd_d, rwkv_mu=rwkv_mu, rwkv_w0=rwkv_w0, rwkv_w2=rwkv_w2, rwkv_a0=rwkv_a0, rwkv_a2=rwkv_a2,
             rwkv_g2=rwkv_g2, rwkv_kk=rwkv_kk, rwkv_ka=rwkv_ka, rwkv_rk=rwkv_rk, gla_gu=gla_gu, gla_gb=gla_gb)
    bsz, seq, d = x.shape
    n_ctx = ctx.shape[1]
    assert n_ctx == N_CTX and d == D_MODEL and seq % TOK_BLOCK == 0 and seq // GRID_W >= 8
    depth = ada_w.shape[0]
    row = lambda a: a.reshape(1, -1)

    cond = jnp.concatenate([c, c_ctx[None], jnp.zeros((8 - bsz - 1, d), F32)], axis=0)
    mods = _ada_all(cond, ada_w, ada_b).reshape(depth, 8, N_MOD, d)
    xs = jnp.concatenate([ctx, x], axis=1)
    t = xs.shape[1]
    msels = [jnp.stack([jnp.broadcast_to(mods[l, bsz], (bsz, N_MOD, d)), mods[l, :bsz]], axis=1)
             for l in range(depth)]
    proj = _inproj(xs, msels[0], _prep_w_in(w_in[0]))
    for l in range(depth):
        msel = msels[l]
        z, og, ssd_f, ssd_b, gla_f, gla_b, rf, rb, bonus, gate = _token_mixer(proj, l, p)
        x1, h, gates = _outproj(xs, msel, ssd_f, ssd_b, z, rf, rb, bonus, gate, gla_f, gla_b, og,
                                row(ssd_norm_w[l]), row(rwkv_lnx_w[l]), row(rwkv_lnx_b[l]), row(gla_norm_w[l]),
                                w_out[l].astype(BF16), row(ln1_w[l]), row(ln1_b[l]),
                                router_w[l].T, jnp.broadcast_to(router_b[l][:, None], (N_EXPERTS, 128)))
        routed = _moe(h.reshape(bsz * t, d), gates.reshape(bsz * t, 128),
                      exp_w1[l].astype(BF16), exp_w3[l].astype(BF16),
                      exp_w2[l].astype(BF16).reshape(N_EXPERTS // EXPERTS_PER_STEP, -1, d))
        s13 = jnp.concatenate([sh_w1[l], sh_w3[l]], axis=1).astype(BF16)
        args = (x1, h, routed.reshape(bsz, t, d), msel, s13, sh_w2[l].astype(BF16), row(ln2_w[l]), row(ln2_b[l]))
        if l == depth - 1:
            return _final(*args, None)
        xs, proj = _final(*args, (msels[l + 1], _prep_w_in(w_in[l + 1])))
```

```python
import functools

import jax
import jax.numpy as jnp
import numpy as np
from jax import lax
from jax.experimental import pallas as pl
from jax.experimental.pallas import tpu as pltpu

F32 = jnp.float32
BF16 = jnp.bfloat16

D_MODEL = 1024
N_CTX = 256
GRID_W = 64
N_MOD = 6
LN_EPS = 1e-5
DEPTH = 4
DN_ALPHA = (2 * DEPTH) ** 0.25

SSD_HEADS = 8
SSD_HEAD_DIM = 64
SSD_INNER = 512
SSD_STATE = 128
SSD_XBC = 1024

RWKV_WIDTH = 256
RWKV_HEAD_DIM = 64
RWKV_COLS = 1152
RWKV_LNX_EPS = 64e-5

GLA_K_WIDTH = 128
GLA_V_WIDTH = 256
GLA_KEY_DIM = 32
GLA_VAL_DIM = 64
GLA_GATE_LORA = 16
GLA_GATE_NORMALIZER = 16.0
GLA_SUB = 64
GLA_BLK = 16

N_EXPERTS = 64
TOP_K = 8
N_GROUPS = 8
TOPK_GROUPS = 4
EXPERT_DIM = 256
ROUTED_SCALE = 2.5
EXPERTS_PER_STEP = 4

TOK_BLOCK = 256
SCAN_BLOCK = 64
V7X_VMEM_LIMIT = 56 * 1024 * 1024

_IN_PIECES = (("z", 512), ("xbc", 1024), ("rw", 1152), ("gq", 128), ("gk", 128), ("gv", 256), ("og", 256),
              ("small", 128))
IN_PAD = sum(w for _, w in _IN_PIECES)


def _cparams(sem):
    return pltpu.CompilerParams(dimension_semantics=sem, vmem_limit_bytes=V7X_VMEM_LIMIT)


def _split3(a):
    hi = a.astype(BF16)
    r1 = a - hi.astype(F32)
    mid = r1.astype(BF16)
    lo = (r1 - mid.astype(F32)).astype(BF16)
    return hi, mid, lo


def _dot(a, b):
    return jnp.dot(a, b, preferred_element_type=F32)


def _dot_nt(a, b):
    return lax.dot_general(a, b, (((1,), (1,)), ((), ())), preferred_element_type=F32)


def _dot_x_exact(a, e):
    hi, mid, lo = _split3(a)
    return _dot(hi, e) + (_dot(mid, e) + _dot(lo, e))


def _dot_exact_x(e, a):
    hi, mid, lo = _split3(a)
    return _dot(e, hi) + (_dot(e, mid) + _dot(e, lo))


def _dot_hp(a, b):
    ah, am, _ = _split3(a)
    bh, bm, _ = _split3(b)
    return _dot(ah, bh) + (_dot(ah, bm) + _dot(am, bh))


def _sigmoid(x):
    return 1.0 / (1.0 + jnp.exp(-x))


def _silu(x):
    return x * _sigmoid(x)


def _softplus(x):
    return jnp.maximum(x, 0.0) + jnp.log(1.0 + jnp.exp(-jnp.abs(x)))


def _layernorm(x):
    mu = jnp.mean(x, axis=-1, keepdims=True)
    xc = x - mu
    var = jnp.mean(xc * xc, axis=-1, keepdims=True)
    return xc * lax.rsqrt(var + LN_EPS)


def _iota(shape, dim):
    return lax.broadcasted_iota(jnp.int32, shape, dim)


def _tri(n, upper):
    r = _iota((n, n), 0)
    c = _iota((n, n), 1)
    return jnp.where(upper, c - r, r - c) >= 0


def _group_ones(n, g):
    r = _iota((n, n), 0) // g
    c = _iota((n, n), 1) // g
    return (r == c).astype(BF16)


def _run_interleaved(gens):
    results = [None] * len(gens)
    active = list(range(len(gens)))
    while active:
        for i in list(active):
            try:
                next(gens[i])
            except StopIteration as stop:
                results[i] = stop.value
                active.remove(i)
    return results


def _ada_body(c_ref, w_ref, b_ref, o_ref):
    a = _silu(c_ref[...]).astype(BF16)
    o_ref[0] = _dot(a, w_ref[0].astype(BF16)) + b_ref[0]


def _ada_all(cond, ada_w, ada_b):
    nl, d, n = ada_w.shape
    tn = 1536
    rows = cond.shape[0]
    return pl.pallas_call(
        _ada_body,
        grid=(nl, n // tn),
        in_specs=[pl.BlockSpec((rows, d), lambda l, k: (0, 0)),
                  pl.BlockSpec((1, d, tn), lambda l, k: (l, 0, k)),
                  pl.BlockSpec((1, 1, tn), lambda l, k: (l, 0, k))],
        out_specs=pl.BlockSpec((1, rows, tn), lambda l, k: (l, 0, k)),
        out_shape=jax.ShapeDtypeStruct((nl, rows, n), F32),
        compiler_params=_cparams(("arbitrary", "arbitrary")),
        name="ada_mod",
    )(cond, ada_w, ada_b.reshape(nl, 1, n))


def _inproj_store(x, m, w_ref, out_refs):
    u = _layernorm(x) * (1.0 + m[1:2]) + m[0:1]
    y = _dot(u.astype(BF16), w_ref[...])
    off = 0
    for (_, wdt), o in zip(_IN_PIECES, out_refs):
        o[0] = y[:, off:off + wdt]
        off += wdt


def _inproj_body(x_ref, m_ref, w_ref, *out_refs):
    _inproj_store(x_ref[0], m_ref[0, 0], w_ref, out_refs)


def _inproj(xs, msel, w_in_p):
    b, t, d = xs.shape
    nb = t // TOK_BLOCK
    out_shape = [jax.ShapeDtypeStruct((b, t, w), F32) for _, w in _IN_PIECES]
    out_specs = [pl.BlockSpec((1, TOK_BLOCK, w), lambda i, j: (i, j, 0)) for _, w in _IN_PIECES]
    return pl.pallas_call(
        _inproj_body,
        grid=(b, nb),
        in_specs=[pl.BlockSpec((1, TOK_BLOCK, d), lambda i, j: (i, j, 0)),
                  pl.BlockSpec((1, 1, N_MOD, d), lambda i, j: (i, jnp.minimum(j, 1), 0, 0)),
                  pl.BlockSpec((d, IN_PAD), lambda i, j: (0, 0))],
        out_specs=out_specs,
        out_shape=out_shape,
        compiler_params=_cparams(("arbitrary", "arbitrary")),
        name="in_proj",
    )(xs, msel, w_in_p)


def _ssd_body(nbatch, nb, fwd_refs, bwd_refs, cw_ref, cb_ref, dtb_ref, alog_ref, dsk_ref, yf_ref, yb_ref,
              h_scr, buf_scr):
    q = TOK_BLOCK
    j = pl.program_id(0)

    @pl.when(j == 0)
    def _():
        h_scr[...] = jnp.zeros_like(h_scr)

    lane = _iota((1, 128), 1)
    neg_a = jnp.where(lane < 2 * SSD_HEADS, -jnp.exp(alog_ref[...]), 0.0)
    cw = cw_ref[0]
    er = _iota((128, SSD_INNER), 0)
    ec = _iota((128, SSD_INNER), 1)
    er8 = _iota((128, 128), 0)
    ec8 = _iota((128, 128), 1)
    lane_q = _iota((q, 128), 1)
    hpg = SSD_HEADS // 2

    def stream(refs, bi, upper, si, blk):
        xbc_ref, hp_ref, hn_ref, small_ref = refs
        d = 1 if upper else 0
        seg_first = (blk == 0) | (blk == 1)
        seg_last = (blk == 0) | (blk == nb - 1)
        cur = xbc_ref[bi]
        buf_scr[8:8 + q] = cur
        buf_scr[0:8] = jnp.where(seg_first, 0.0, hp_ref[bi])
        buf_scr[8 + q:16 + q] = jnp.where(seg_last, 0.0, hn_ref[bi])
        prev = buf_scr[7:7 + q]
        nxt = buf_scr[9:9 + q]
        act = _silu(prev * cw[0:1] + cur * cw[1:2] + nxt * cw[2:3] + cb_ref[...])
        xs = act[:, :SSD_INNER]
        b_grp = [act[:, 512 + 128 * g:640 + 128 * g] for g in range(2)]
        c_grp = [act[:, 768 + 128 * g:896 + 128 * g].astype(BF16) for g in range(2)]

        tri_b = _tri(q, upper)
        tri = tri_b.astype(BF16)
        tri_t = _tri(q, not upper).astype(BF16)
        dt_all = _softplus(small_ref[bi] + dtb_ref[...])
        la_all = dt_all * neg_a
        e_exp = (er == SSD_HEADS * d + ec // SSD_HEAD_DIM).astype(BF16)
        e8 = ((er8 == SSD_HEADS * d + ec8) & (ec8 < SSD_HEADS)).astype(BF16)
        dt_exp = _dot_x_exact(dt_all, e_exp)
        la_exp = _dot_x_exact(la_all, e_exp)
        la8 = _dot_x_exact(la_all, e8)
        cs_exp = _dot_exact_x(tri, la_exp)
        cs8 = _dot_exact_x(tri, la8)
        cs_t = _dot_x_exact(la8.T, tri_t)
        tot_exp = jnp.sum(la_exp, axis=0, keepdims=True)

        xdt = xs * dt_exp
        gmats = [_dot_nt(c_grp[g], b_grp[g].astype(BF16)) for g in range(2)]
        ys = []
        for p in range(SSD_HEADS // 2):
            xdt_p = xdt[:, 128 * p:128 * (p + 1)].astype(BF16)
            halves = []
            for h in (2 * p, 2 * p + 1):
                lmat = jnp.where(tri_b, jnp.exp(cs8[:, h:h + 1] - cs_t[h:h + 1, :]), 0.0)
                halves.append(_dot((gmats[h // hpg] * lmat).astype(BF16), xdt_p))
            ys.append(jnp.where(lane_q < SSD_HEAD_DIM, halves[0], halves[1]))
        y = jnp.concatenate(ys, axis=1)

        xd = (xdt * jnp.exp(tot_exp - cs_exp)).astype(BF16)
        offs = []
        for g in range(2):
            hg = h_scr[si, g]
            offs.append(_dot(c_grp[g], hg.astype(BF16)))
            hn = _dot(b_grp[g].T.astype(BF16), xd[:, 256 * g:256 * (g + 1)])
            h_scr[si, g] = hg * jnp.exp(tot_exp[:, 256 * g:256 * (g + 1)]) + hn
        y = y + jnp.concatenate(offs, axis=1) * jnp.exp(cs_exp)
        if not upper:
            y = y + dsk_ref[...] * xs
        return y

    blk_b = jnp.where(j == 0, 0, nb - j)
    for bi in range(nbatch):
        yf_ref[bi] = stream(fwd_refs, bi, False, bi, j)
        yb_ref[bi] = stream(bwd_refs, bi, True, nbatch + bi, blk_b)


def _ssd(xbc, small, conv_w, conv_b, dtb_pad, alog_pad, dskip_exp):
    b, t, _ = xbc.shape
    nb = t // TOK_BLOCK
    r8 = TOK_BLOCK // 8

    def bwd_blk(j):
        return jnp.where(j == 0, 0, nb - j)

    def specs(blk):
        return [pl.BlockSpec((b, TOK_BLOCK, SSD_XBC), lambda j: (0, blk(j), 0)),
                pl.BlockSpec((b, 8, SSD_XBC), lambda j: (0, jnp.maximum(blk(j) * r8 - 1, 0), 0)),
                pl.BlockSpec((b, 8, SSD_XBC), lambda j: (0, jnp.minimum(blk(j) * r8 + r8, t // 8 - 1), 0)),
                pl.BlockSpec((b, TOK_BLOCK, 128), lambda j: (0, blk(j), 0))]

    def body(*refs):
        _ssd_body(b, nb, refs[0:4], refs[4:8], *refs[8:])

    const2 = lambda j: (0, 0)
    oshape = jax.ShapeDtypeStruct((b, t, SSD_INNER), F32)
    return pl.pallas_call(
        body,
        grid=(nb,),
        in_specs=specs(lambda j: j) + specs(bwd_blk) + [
            pl.BlockSpec((1, 3, SSD_XBC), lambda j: (0, 0, 0)),
            pl.BlockSpec((1, SSD_XBC), const2),
            pl.BlockSpec((1, 128), const2),
            pl.BlockSpec((1, 128), const2),
            pl.BlockSpec((1, SSD_INNER), const2)],
        out_specs=[pl.BlockSpec((b, TOK_BLOCK, SSD_INNER), lambda j: (0, j, 0)),
                   pl.BlockSpec((b, TOK_BLOCK, SSD_INNER), lambda j: (0, bwd_blk(j), 0))],
        out_shape=[oshape, oshape],
        scratch_shapes=[pltpu.VMEM((2 * b, 2, SSD_STATE, 256), F32),
                        pltpu.VMEM((TOK_BLOCK + 16, SSD_XBC), F32)],
        compiler_params=_cparams(("arbitrary",)),
        name="ssd",
    )(xbc, xbc, xbc, small, xbc, xbc, xbc, small, conv_w[None], conv_b, dtb_pad, alog_pad, dskip_exp)


def _gla_body(nbatch, fwd_refs, bwd_refs, gu_ref, gb_ref, of_ref, ob_ref, s_scr):
    j = pl.program_id(0)
    c = GLA_SUB
    nsub = TOK_BLOCK // c

    @pl.when(j == 0)
    def _():
        s_scr[...] = jnp.zeros_like(s_scr)

    lr = _iota((128, 128), 0)
    lc = _iota((128, 128), 1)
    lane_k = _iota((c, GLA_K_WIDTH), 1) // GLA_KEY_DIM
    lane_kx = _iota((GLA_BLK * (c // GLA_BLK) * (c // GLA_BLK - 1) // 2, GLA_K_WIDTH), 1) // GLA_KEY_DIM
    lane_v = _iota((c, GLA_V_WIDTH), 1) // GLA_VAL_DIM
    st_mask = (_iota((GLA_V_WIDTH, GLA_K_WIDTH), 0) // GLA_VAL_DIM
               == _iota((GLA_V_WIDTH, GLA_K_WIDTH), 1) // GLA_KEY_DIM)

    nblk = c // GLA_BLK
    tok_r = _iota((c, c), 0)
    tok_c = _iota((c, c), 1)
    col_blk = _iota((GLA_BLK, c), 1) // GLA_BLK

    def direction_consts(upper):
        first = (tok_r // GLA_BLK) * GLA_BLK + (GLA_BLK - 1 if upper else 0)
        last = (tok_r // GLA_BLK) * GLA_BLK + (0 if upper else GLA_BLK - 1)
        pairs = [(ti, tj) for ti in range(nblk) for tj in range(nblk) if (tj > ti if upper else tj < ti)]
        return dict(tri_b=_tri(c, upper), sel_first=(tok_c == first).astype(BF16),
                    sel_last=(tok_c == last).astype(BF16), pairs=pairs)

    consts = {False: direction_consts(False), True: direction_consts(True)}

    def stream(refs, bi, upper, st):
        q_ref, k_ref, v_ref, small_ref = refs
        d = 1 if upper else 0
        cst = consts[upper]
        tri_b = cst["tri_b"]
        tri = tri_b.astype(BF16)
        gsel = ((lr == 2 * SSD_HEADS + GLA_GATE_LORA * d + lc) & (lc < GLA_GATE_LORA)).astype(BF16)
        gd = _dot_x_exact(small_ref[bi], gsel)
        yield
        pre = _dot_hp(gd, gu_ref[d]) + gb_ref[d:d + 1]
        la_all = -_softplus(-pre) * (1.0 / GLA_GATE_NORMALIZER)
        yield
        outs = [None] * nsub
        for si in range(nsub):
            sub = nsub - 1 - si if upper else si
            lo = sub * c
            la = la_all[lo:lo + c]
            qq = q_ref[bi, lo:lo + c, :] * (GLA_KEY_DIM ** -0.5)
            kk = k_ref[bi, lo:lo + c, :]
            vv = v_ref[bi, lo:lo + c, :]
            cs = _dot_exact_x(tri, la)
            yield
            rs = _dot_exact_x(cst["sel_first"], cs - la)
            re = _dot_exact_x(cst["sel_last"], cs)
            yield
            tot = jnp.sum(la, axis=0, keepdims=True)
            qt = qq * jnp.exp(cs)
            kdec = (kk * jnp.exp(tot - cs)).astype(BF16)
            q_in = qq * jnp.exp(cs - rs)
            k_in = (kk * jnp.exp(jnp.minimum(rs - cs, 80.0))).astype(BF16)
            k_out = (kk * jnp.exp(re - cs)).astype(BF16)
            q_cross = jnp.concatenate(
                [q_in[GLA_BLK * ti:GLA_BLK * (ti + 1)]
                 * jnp.exp(rs[GLA_BLK * ti:GLA_BLK * ti + 1] - re[GLA_BLK * tj:GLA_BLK * tj + 1])
                 for ti, tj in cst["pairs"]], axis=0)
            vb = vv.astype(BF16)
            o = _dot_nt(qt.astype(BF16), st.astype(BF16))
            amats = []
            for h in range(GLA_K_WIDTH // GLA_KEY_DIM):
                r_in = _dot_nt(jnp.where(lane_k == h, q_in, 0.0).astype(BF16), k_in)
                r_cross = _dot_nt(jnp.where(lane_kx == h, q_cross, 0.0).astype(BF16), k_out)
                rows = []
                for ti in range(nblk):
                    blk_rows = slice(GLA_BLK * ti, GLA_BLK * (ti + 1))
                    part = jnp.where(tri_b[blk_rows] & (col_blk == ti), r_in[blk_rows], 0.0)
                    for pi, (pti, ptj) in enumerate(cst["pairs"]):
                        if pti == ti:
                            part = part + jnp.where(col_blk == ptj, r_cross[GLA_BLK * pi:GLA_BLK * (pi + 1)], 0.0)
                    rows.append(part)
                amats.append(jnp.concatenate(rows, axis=0).astype(BF16))
            upd = _dot(vv.T.astype(BF16), kdec)
            yield
            for h, a in enumerate(amats):
                o = o + jnp.where(lane_v == h, _dot(a, vb), 0.0)
            st = st * jnp.exp(tot) + jnp.where(st_mask, upd, 0.0)
            outs[sub] = o
            yield
        return jnp.concatenate(outs, axis=0), st

    gens = ([stream(fwd_refs, bi, False, s_scr[bi]) for bi in range(nbatch)]
            + [stream(bwd_refs, bi, True, s_scr[nbatch + bi]) for bi in range(nbatch)])
    for si, (o, st) in enumerate(_run_interleaved(gens)):
        if si < nbatch:
            of_ref[si] = o
        else:
            ob_ref[si - nbatch] = o
        s_scr[si] = st


def _gla(gq, gk, gv, small, gu_pad, gb):
    b, t, _ = gq.shape
    nb = t // TOK_BLOCK

    def bwd_blk(j):
        return jnp.where(j == 0, 0, nb - j)

    def specs(blk):
        return [pl.BlockSpec((b, TOK_BLOCK, GLA_K_WIDTH), lambda j: (0, blk(j), 0)),
                pl.BlockSpec((b, TOK_BLOCK, GLA_K_WIDTH), lambda j: (0, blk(j), 0)),
                pl.BlockSpec((b, TOK_BLOCK, GLA_V_WIDTH), lambda j: (0, blk(j), 0)),
                pl.BlockSpec((b, TOK_BLOCK, 128), lambda j: (0, blk(j), 0))]

    def body(*refs):
        _gla_body(b, refs[0:4], refs[4:8], *refs[8:])

    oshape = jax.ShapeDtypeStruct((b, t, GLA_V_WIDTH), F32)
    return pl.pallas_call(
        body,
        grid=(nb,),
        in_specs=specs(lambda j: j) + specs(bwd_blk) + [
            pl.BlockSpec((2, 128, GLA_K_WIDTH), lambda j: (0, 0, 0)),
            pl.BlockSpec((2, GLA_K_WIDTH), lambda j: (0, 0))],
        out_specs=[pl.BlockSpec((b, TOK_BLOCK, GLA_V_WIDTH), lambda j: (0, j, 0)),
                   pl.BlockSpec((b, TOK_BLOCK, GLA_V_WIDTH), lambda j: (0, bwd_blk(j), 0))],
        out_shape=[oshape, oshape],
        scratch_shapes=[pltpu.VMEM((2 * b, GLA_V_WIDTH, GLA_K_WIDTH), F32)],
        compiler_params=_cparams(("arbitrary",)),
        name="gla",
    )(gq, gk, gv, small, gq, gk, gv, small, gu_pad, gb)


_RWKV_OUTS = ("r", "v", "kk", "lw_f", "k_f", "kka_f", "lw_b", "k_b", "kka_b", "bonus", "gate")


def _rwkv_prep_body(nb, rw_ref, hp_ref, hn_ref, mu_ref, w0_ref, w2_ref, a0_ref, a2_ref, g2_ref, kkp_ref, ka_ref,
                    rk_ref, *refs):
    outs = dict(zip(_RWKV_OUTS, refs[:len(_RWKV_OUTS)]))
    buf_scr = refs[len(_RWKV_OUTS)]
    q = TOK_BLOCK
    j = pl.program_id(1)
    is_ctx = j == 0
    z = rw_ref[0]
    zeros8 = jnp.zeros((8, RWKV_COLS), F32)
    buf_scr[0:8] = zeros8
    buf_scr[8 + q:16 + q] = zeros8
    buf_scr[8:8 + q] = z
    t = _iota((q, 1), 0)
    col = t % GRID_W
    prev = jnp.where(jnp.where(is_ctx, t, col) == 0, 0.0, buf_scr[7:7 + q])
    nxt = jnp.where(jnp.where(is_ctx, t - (q - 1), col - (GRID_W - 1)) == 0, 0.0, buf_scr[9:9 + q])
    up = jnp.concatenate([jnp.where(j == 1, 0.0, hp_ref[0]), z[:q - GRID_W]], axis=0)
    down = jnp.concatenate([z[GRID_W:], jnp.where(j == nb - 1, 0.0, hn_ref[0])], axis=0)
    mu = mu_ref[...]
    vert = mu[2:3] * (up - z) + mu[3:4] * (down - z)
    mixed = z + mu[0:1] * (prev - z) + mu[1:2] * (nxt - z) + jnp.where(is_ctx, 0.0, vert)

    w = RWKV_WIDTH
    r = mixed[:, 0:w]
    k = mixed[:, w:2 * w]
    v = mixed[:, 2 * w:3 * w]
    wd = mixed[:, 3 * w:3 * w + 128]
    ad = mixed[:, 3 * w + 128:4 * w]
    gdr = mixed[:, 4 * w:4 * w + 128]

    lw = w0_ref[...] + _dot(jnp.tanh(wd).astype(BF16), w2_ref[...])
    log_dec = -jnp.exp(-_softplus(-lw) - 0.5)
    a = _sigmoid(a0_ref[...] + _dot(ad.astype(BF16), a2_ref[...]))
    ones64 = _group_ones(w, RWKV_HEAD_DIM)
    kkv = k * kkp_ref[...]
    nrm = jnp.maximum(jnp.sqrt(_dot_x_exact(kkv * kkv, ones64)), 1e-12)
    kkn = kkv / nrm
    ka = ka_ref[...]
    a_f, a_b = a[:, :w], a[:, w:]
    k_f = k * (1.0 + (a_f - 1.0) * ka)
    k_b = k * (1.0 + (a_b - 1.0) * ka)
    outs["r"][0] = r
    outs["v"][0] = v
    outs["kk"][0] = kkn
    outs["lw_f"][0] = log_dec[:, :w]
    outs["lw_b"][0] = log_dec[:, w:]
    outs["k_f"][0] = k_f
    outs["k_b"][0] = k_b
    outs["kka_f"][0] = kkn * a_f
    outs["kka_b"][0] = kkn * a_b
    outs["bonus"][0] = _dot_x_exact(r * (k_f + k_b) * rk_ref[...], ones64) * v
    outs["gate"][0] = _dot(_sigmoid(gdr).astype(BF16), g2_ref[...])


def _rwkv_prep(rw, mu, w0cat, w2bd, a0cat, a2bd, g2, kkp, ka, rk):
    b, t, _ = rw.shape
    nb = t // TOK_BLOCK
    hb = TOK_BLOCK // GRID_W
    w = RWKV_WIDTH
    c2 = lambda i, j: (0, 0)
    return pl.pallas_call(
        functools.partial(_rwkv_prep_body, nb),
        grid=(b, nb),
        in_specs=[pl.BlockSpec((1, TOK_BLOCK, RWKV_COLS), lambda i, j: (i, j, 0)),
                  pl.BlockSpec((1, GRID_W, RWKV_COLS), lambda i, j: (i, jnp.maximum(j * hb - 1, 0), 0)),
                  pl.BlockSpec((1, GRID_W, RWKV_COLS),
                               lambda i, j: (i, jnp.minimum(j * hb + hb, t // GRID_W - 1), 0)),
                  pl.BlockSpec((4, RWKV_COLS), c2),
                  pl.BlockSpec((1, 2 * w), c2),
                  pl.BlockSpec((128, 2 * w), c2),
                  pl.BlockSpec((1, 2 * w), c2),
                  pl.BlockSpec((128, 2 * w), c2),
                  pl.BlockSpec((128, w), c2),
                  pl.BlockSpec((1, w), c2),
                  pl.BlockSpec((1, w), c2),
                  pl.BlockSpec((1, w), c2)],
        out_specs=[pl.BlockSpec((1, TOK_BLOCK, w), lambda i, j: (i, j, 0)) for _ in _RWKV_OUTS],
        out_shape=[jax.ShapeDtypeStruct((b, t, w), F32) for _ in _RWKV_OUTS],
        scratch_shapes=[pltpu.VMEM((TOK_BLOCK + 16, RWKV_COLS), F32)],
        compiler_params=_cparams(("arbitrary", "arbitrary")),
        name="rwkv_prep",
    )(rw, rw, rw, mu, w0cat, w2bd, a0cat, a2bd, g2, kkp, ka, rk)


def _rwkv_scan_body(nbatch, fwd_refs, bwd_refs, yf_ref, yb_ref, s_scr):
    j = pl.program_id(0)
    n = SCAN_BLOCK
    hd = RWKV_HEAD_DIM

    @pl.when(j == 0)
    def _():
        s_scr[...] = jnp.zeros_like(s_scr)

    nh = RWKV_WIDTH // hd
    w = RWKV_WIDTH
    assert n == hd
    lane_head = _iota((1, w), 1) // hd
    cat_c = _iota((n, nh * n), 0)
    cat_cc = _iota((n, nh * n), 1) % n
    eye_cat = (cat_c == cat_cc).astype(F32)
    bd_mask = (_iota((w, w), 0) // hd) == (_iota((w, w), 1) // hd)

    def stack(x):
        return jnp.concatenate([jnp.where(lane_head == h, x, 0.0) for h in range(nh)], axis=0)

    def chunk(refs, bi, upper, s_prev):
        r, v, kk, lw, k, kka = (ref[bi] for ref in refs)
        tri = _tri(n, upper).astype(BF16)
        strict = (cat_cc > cat_c) if upper else (cat_cc < cat_c)
        incl = (cat_cc >= cat_c) if upper else (cat_cc <= cat_c)
        cs = _dot_exact_x(tri, lw)
        yield
        tot = jnp.sum(lw, axis=0, keepdims=True)
        g_inv = jnp.exp(-cs)
        dec_end = jnp.exp(tot - cs)
        a_t = (-kk * jnp.exp(cs - lw)).astype(BF16)
        r_t = (r * jnp.exp(cs)).astype(BF16)
        bk_s = jnp.concatenate([stack(kka * g_inv), stack(k * g_inv)], axis=0).astype(BF16)
        bk_dec = jnp.concatenate([kka * dec_end, k * dec_end], axis=0).astype(BF16)
        v_s = stack(v).astype(BF16)
        x = _dot_nt(jnp.concatenate([a_t, r_t], axis=0), bk_s)
        yield
        nmat = jnp.where(strict, x[:n, :nh * n], 0.0)
        ak = jnp.where(strict, x[:n, nh * n:], 0.0).astype(BF16)
        rb = jnp.where(incl, x[n:, :nh * n], 0.0).astype(BF16)
        rk = jnp.where(incl, x[n:, nh * n:], 0.0).astype(BF16)
        s_b = s_prev.astype(BF16)
        w0 = _dot_nt(a_t, s_b) + _dot(ak, v_s)
        y0 = _dot_nt(r_t, s_b) + _dot(rk, v_s)
        tinv = eye_cat + nmat
        p = nmat.astype(BF16)
        p_s = stack(nmat).astype(BF16)
        for i in range(n.bit_length() - 2):
            p2 = _dot(p, p_s)
            yield
            p_s = stack(p2).astype(BF16)
            tinv = tinv + _dot(tinv.astype(BF16), p_s)
            p = p2.astype(BF16)
            yield
        u = _dot(tinv.astype(BF16), stack(w0).astype(BF16))
        yield
        y = y0 + _dot(rb, stack(u).astype(BF16))
        uv = jnp.concatenate([u, v], axis=0).astype(BF16)
        upd = lax.dot_general(uv, bk_dec, (((0,), (0,)), ((), ())), preferred_element_type=F32)
        s_new = s_prev * jnp.exp(tot) + jnp.where(bd_mask, upd, 0.0)
        return y, s_new

    gens = ([chunk(fwd_refs, bi, False, s_scr[bi]) for bi in range(nbatch)]
            + [chunk(bwd_refs, bi, True, s_scr[nbatch + bi]) for bi in range(nbatch)])
    for si, (y, s_new) in enumerate(_run_interleaved(gens)):
        if si < nbatch:
            yf_ref[si] = y
        else:
            yb_ref[si - nbatch] = y
        s_scr[si] = s_new


def _rwkv_scan(pre):
    b, t, w = pre["r"].shape
    nbs = t // SCAN_BLOCK
    ctx_blocks = N_CTX // SCAN_BLOCK

    def fwd_blk(j):
        return j

    def bwd_blk(j):
        return jnp.where(j < ctx_blocks, ctx_blocks - 1 - j, nbs + ctx_blocks - 1 - j)

    in_f = pl.BlockSpec((b, SCAN_BLOCK, w), lambda j: (0, fwd_blk(j), 0))
    in_b = pl.BlockSpec((b, SCAN_BLOCK, w), lambda j: (0, bwd_blk(j), 0))
    oshape = jax.ShapeDtypeStruct((b, t, w), F32)

    def body(*refs):
        _rwkv_scan_body(b, refs[0:6], refs[6:12], refs[12], refs[13], refs[14])

    return pl.pallas_call(
        body,
        grid=(nbs,),
        in_specs=[in_f] * 6 + [in_b] * 6,
        out_specs=[in_f, in_b],
        out_shape=[oshape, oshape],
        scratch_shapes=[pltpu.VMEM((2 * b, w, w), F32)],
        compiler_params=_cparams(("arbitrary",)),
        name="rwkv_scan",
    )(pre["r"], pre["v"], pre["kk"], pre["lw_f"], pre["k_f"], pre["kka_f"],
      pre["r"], pre["v"], pre["kk"], pre["lw_b"], pre["k_b"], pre["kka_b"])


def _outproj_body(x_ref, m_ref, s0_ref, s1_ref, z_ref, rf_ref, rb_ref, bonus_ref, gate_ref, g0_ref, g1_ref, og_ref,
                  snw_ref, lxw_ref, lxb_ref, gnw_ref, wo_ref, l1w_ref, l1b_ref, rtw_ref, rtb_ref,
                  x1_ref, h_ref, g_ref):
    m = m_ref[0, 0]
    ys = (s0_ref[0] + s1_ref[0]) * _silu(z_ref[0])
    gw = SSD_INNER // 2
    parts = []
    for g in range(2):
        part = ys[:, gw * g:gw * (g + 1)]
        parts.append(part * lax.rsqrt(jnp.mean(part * part, axis=-1, keepdims=True) + 1e-5))
    y_ssd = jnp.concatenate(parts, axis=1) * snw_ref[...]
    ones64 = _group_ones(RWKV_WIDTH, RWKV_HEAD_DIM)
    inv = 1.0 / RWKV_HEAD_DIM
    yr = rf_ref[0] + rb_ref[0]
    mu = _dot_x_exact(yr, ones64) * inv
    yc = yr - mu
    var = _dot_x_exact(yc * yc, ones64) * inv
    yn = yc * lax.rsqrt(var + RWKV_LNX_EPS) * lxw_ref[...] + lxb_ref[...]
    y_rwkv = (yn + bonus_ref[0]) * gate_ref[0]
    o = g0_ref[0] + g1_ref[0]
    ms = _dot_x_exact(o * o, ones64) * (1.0 / GLA_VAL_DIM)
    y_gla = o * lax.rsqrt(ms + 1e-5) * gnw_ref[...] * _silu(og_ref[0])

    y_mix = jnp.concatenate([y_ssd, y_rwkv, y_gla], axis=1).astype(BF16)
    mix = _dot(y_mix, wo_ref[...])
    x1 = _layernorm(DN_ALPHA * x_ref[0] + m[2:3] * mix) * l1w_ref[...] + l1b_ref[...]
    x1_ref[0] = x1
    h = _layernorm(x1) * (1.0 + m[4:5]) + m[3:4]
    h_ref[0] = h.astype(BF16)
    g_ref[0] = _route(h, rtw_ref[...], rtb_ref[...])


def _outproj(xs, msel, ssd_f, ssd_b, z, rf, rb, bonus, gate, gla_f, gla_b, og, snw, lxw, lxb, gnw, wo, l1w, l1b, rtw, rtb):
    b, t, d = xs.shape
    nb = t // TOK_BLOCK
    tok = lambda w: pl.BlockSpec((1, TOK_BLOCK, w), lambda i, j: (i, j, 0))
    c2 = lambda i, j: (0, 0)
    row = lambda w: pl.BlockSpec((1, w), c2)
    return pl.pallas_call(
        _outproj_body,
        grid=(b, nb),
        in_specs=[tok(d),
                  pl.BlockSpec((1, 1, N_MOD, d), lambda i, j: (i, jnp.minimum(j, 1), 0, 0)),
                  tok(SSD_INNER), tok(SSD_INNER), tok(SSD_INNER),
                  tok(RWKV_WIDTH), tok(RWKV_WIDTH), tok(RWKV_WIDTH), tok(RWKV_WIDTH),
                  tok(GLA_V_WIDTH), tok(GLA_V_WIDTH), tok(GLA_V_WIDTH),
                  row(SSD_INNER), row(RWKV_WIDTH), row(RWKV_WIDTH), row(GLA_V_WIDTH),
                  pl.BlockSpec((d, d), c2), row(d), row(d),
                  pl.BlockSpec((N_EXPERTS, d), c2), pl.BlockSpec((N_EXPERTS, 128), c2)],
        out_specs=[tok(d), tok(d), tok(128)],
        out_shape=[jax.ShapeDtypeStruct((b, t, d), F32), jax.ShapeDtypeStruct((b, t, d), BF16),
                   jax.ShapeDtypeStruct((b, t, 128), F32)],
        compiler_params=_cparams(("arbitrary", "arbitrary")),
        name="out_proj_router",
    )(xs, msel, ssd_f, ssd_b, z, rf, rb, bonus, gate, gla_f, gla_b, og, snw, lxw, lxb, gnw, wo, l1w, l1b, rtw, rtb)


def _route(h, rw_t, rb_t):
    tm = h.shape[0]
    ne = N_EXPERTS
    per_group = ne // N_GROUPS
    ah, am, _ = _split3(rw_t)
    bh, bm, _ = _split3(h)
    logits = _dot_nt(ah, bh) + (_dot_nt(ah, bm) + _dot_nt(am, bh))
    scores = _sigmoid(logits)
    sel = scores + jnp.concatenate([rb_t] * (tm // 128), axis=1)
    neg = -jnp.inf
    sub8 = _iota((per_group, tm), 0).astype(F32)
    eidx = _iota((ne, tm), 0).astype(F32)

    def first_max(cur, idx, sentinel):
        mx = jnp.max(cur, axis=0, keepdims=True)
        first = jnp.min(jnp.where(cur == mx, idx, sentinel), axis=0, keepdims=True)
        return mx, idx == first

    gs = []
    for g in range(N_GROUPS):
        blk = sel[per_group * g:per_group * (g + 1)]
        m1, hit = first_max(blk, sub8, float(per_group))
        m2 = jnp.max(jnp.where(hit, neg, blk), axis=0, keepdims=True)
        gs.append(m1 + m2)
    cur = jnp.concatenate(gs, axis=0)
    allowed_g = jnp.zeros((N_GROUPS, tm), F32)
    for _ in range(TOPK_GROUPS):
        _, hit = first_max(cur, sub8, float(N_GROUPS))
        cur = jnp.where(hit, neg, cur)
        allowed_g = jnp.where(hit, 1.0, allowed_g)
    allowed = jnp.concatenate([jnp.broadcast_to(allowed_g[g:g + 1], (per_group, tm)) for g in range(N_GROUPS)],
                              axis=0)
    cur = jnp.where(allowed > 0.0, sel, neg)
    wts = jnp.zeros((ne, tm), F32)
    for _ in range(TOP_K):
        _, hit = first_max(cur, eidx, float(ne))
        cur = jnp.where(hit, neg, cur)
        wts = jnp.where(hit, scores, wts)
    gates_t = wts / jnp.sum(wts, axis=0, keepdims=True) * ROUTED_SCALE
    return jnp.concatenate([gates_t, jnp.zeros((128 - ne, tm), F32)], axis=0).T


def _moe_body(h_ref, g_ref, w1_ref, w3_ref, w2_ref, o_ref):
    e0 = pl.program_id(1) * EXPERTS_PER_STEP

    @pl.when(pl.program_id(1) == 0)
    def _():
        o_ref[...] = jnp.zeros_like(o_ref)

    hb = h_ref[...]
    gates = g_ref[...]
    lane = _iota(gates.shape, 1)
    hids = []
    for e in range(EXPERTS_PER_STEP):
        gcol = jnp.sum(jnp.where(lane == e0 + e, gates, 0.0), axis=-1, keepdims=True)
        hid = _silu(_dot(hb, w1_ref[e])) * _dot(hb, w3_ref[e])
        hids.append((hid * gcol).astype(BF16))
    o_ref[...] += _dot(jnp.concatenate(hids, axis=1), w2_ref[0])


def _moe(h2, gates, w1, w3, w2g, layer):
    n, d = h2.shape
    tm = n // 8 if n % (8 * 128) == 0 else 1024
    _, ne, _, f = w1.shape
    eps = EXPERTS_PER_STEP
    return pl.pallas_call(
        _moe_body,
        grid=(n // tm, ne // eps),
        in_specs=[pl.BlockSpec((tm, d), lambda i, g: (i, 0)),
                  pl.BlockSpec((tm, 128), lambda i, g: (i, 0)),
                  pl.BlockSpec((None, eps, d, f), lambda i, g: (layer, g, 0, 0)),
                  pl.BlockSpec((None, eps, d, f), lambda i, g: (layer, g, 0, 0)),
                  pl.BlockSpec((None, 1, eps * f, d), lambda i, g: (layer, g, 0, 0))],
        out_specs=pl.BlockSpec((tm, d), lambda i, g: (i, 0)),
        out_shape=jax.ShapeDtypeStruct((n, d), F32),
        compiler_params=_cparams(("arbitrary", "arbitrary")),
        name="moe_experts",
    )(h2, gates, w1, w3, w2g)


def _ffn_residual(x1_ref, h_ref, rt_ref, m, s13_ref, s2_ref, l2w_ref, l2b_ref):
    a = _dot(h_ref[0], s13_ref[...])
    fs = a.shape[1] // 2
    hid = (_silu(a[:, :fs]) * a[:, fs:]).astype(BF16)
    f = rt_ref[0] + _dot(hid, s2_ref[...])
    return _layernorm(DN_ALPHA * x1_ref[0] + m[5:6] * f) * l2w_ref[...] + l2b_ref[...]


def _final_body(x1_ref, h_ref, rt_ref, m_ref, s13_ref, s2_ref, l2w_ref, l2b_ref, o_ref):
    o_ref[0] = _ffn_residual(x1_ref, h_ref, rt_ref, m_ref[0, 0], s13_ref, s2_ref, l2w_ref, l2b_ref)


def _final_next_body(x1_ref, h_ref, rt_ref, m_ref, s13_ref, s2_ref, l2w_ref, l2b_ref, mn_ref, w_ref, o_ref,
                     *proj_refs):
    x2 = _ffn_residual(x1_ref, h_ref, rt_ref, m_ref[0, 0], s13_ref, s2_ref, l2w_ref, l2b_ref)
    o_ref[0] = x2
    _inproj_store(x2, mn_ref[0, 0], w_ref, proj_refs)


def _final(x1, h, routed, msel, s13, s2, l2w, l2b, nxt):
    b, t, d = x1.shape
    nb = t // TOK_BLOCK
    skip = N_CTX // TOK_BLOCK if nxt is None else 0
    tok = pl.BlockSpec((1, TOK_BLOCK, d), lambda i, j: (i, j + skip, 0))
    mspec = pl.BlockSpec((1, 1, N_MOD, d), lambda i, j: (i, jnp.minimum(j + skip, 1), 0, 0))
    c2 = lambda i, j: (0, 0)
    in_specs = [tok, tok, tok, mspec, pl.BlockSpec(s13.shape, c2), pl.BlockSpec(s2.shape, c2),
                pl.BlockSpec((1, d), c2), pl.BlockSpec((1, d), c2)]
    out_tok = lambda w: pl.BlockSpec((1, TOK_BLOCK, w), lambda i, j: (i, j, 0))
    if nxt is None:
        return pl.pallas_call(
            _final_body,
            grid=(b, nb - skip),
            in_specs=in_specs,
            out_specs=out_tok(d),
            out_shape=jax.ShapeDtypeStruct((b, t - skip * TOK_BLOCK, d), F32),
            compiler_params=_cparams(("arbitrary", "arbitrary")),
            name="shared_ffn_ln",
        )(x1, h, routed, msel, s13, s2, l2w, l2b)
    msel_n, w_in_n = nxt
    outs = pl.pallas_call(
        _final_next_body,
        grid=(b, nb),
        in_specs=in_specs + [mspec, pl.BlockSpec((d, IN_PAD), c2)],
        out_specs=[out_tok(d)] + [out_tok(w) for _, w in _IN_PIECES],
        out_shape=[jax.ShapeDtypeStruct((b, t, d), F32)] + [jax.ShapeDtypeStruct((b, t, w), F32)
                                                            for _, w in _IN_PIECES],
        compiler_params=_cparams(("arbitrary", "arbitrary")),
        name="shared_ffn_ln_in_proj",
    )(x1, h, routed, msel, s13, s2, l2w, l2b, msel_n, w_in_n)
    return outs[0], outs[1:]


def _prep_w_in(w):
    z, xbc, dt, rw, gq, gk, gv, gd, og = jnp.split(
        w, np.cumsum([512, 1024, 16, 1152, 128, 128, 256, 32, 256])[:-1].tolist(), axis=1)
    small = jnp.concatenate([dt, gd, jnp.zeros((w.shape[0], 128 - 48), w.dtype)], axis=1)
    return jnp.concatenate([z, xbc, rw, gq, gk, gv, og, small], axis=1).astype(BF16)


def _block_diag2(m):
    r, c = m.shape[1:]
    zero = jnp.zeros((r, c), m.dtype)
    return jnp.concatenate([jnp.concatenate([m[0], zero], axis=1),
                            jnp.concatenate([zero, m[1]], axis=1)], axis=0)


def _pad_lanes(v, n=128):
    v = v.reshape(1, -1)
    return jnp.pad(v, ((0, 0), (0, n - v.shape[1])))


def _token_mixer(proj, l, p):
    z, xbc, rw, gq, gk, gv, og, small = proj
    ssd_f, ssd_b = _ssd(xbc, small, p["ssd_conv_w"][l], p["ssd_conv_b"][l].reshape(1, -1),
                 _pad_lanes(p["ssd_dt_bias"][l]), _pad_lanes(p["ssd_a_log"][l]),
                 jnp.repeat(p["ssd_d"][l], SSD_HEAD_DIM).reshape(1, -1))
    gu_pad = jnp.pad(p["gla_gu"][l], ((0, 0), (0, 128 - GLA_GATE_LORA), (0, 0)))
    gla_f, gla_b = _gla(gq, gk, gv, small, gu_pad, p["gla_gb"][l])
    pre = dict(zip(_RWKV_OUTS, _rwkv_prep(
        rw, p["rwkv_mu"][l], p["rwkv_w0"][l].reshape(1, -1), _block_diag2(p["rwkv_w2"][l]).astype(BF16),
        p["rwkv_a0"][l].reshape(1, -1), _block_diag2(p["rwkv_a2"][l]).astype(BF16),
        p["rwkv_g2"][l].astype(BF16), p["rwkv_kk"][l].reshape(1, -1), p["rwkv_ka"][l].reshape(1, -1),
        p["rwkv_rk"][l].reshape(1, -1))))
    rf, rb = _rwkv_scan(pre)
    return z, og, ssd_f, ssd_b, gla_f, gla_b, rf, rb, pre["bonus"], pre["gate"]


def kernel(x, c, ctx, c_ctx, ada_w, ada_b, w_in, ssd_conv_w, ssd_conv_b, ssd_dt_bias, ssd_a_log, ssd_d, ssd_norm_w, rwkv_mu, rwkv_w0, rwkv_w2, rwkv_a0, rwkv_a2, rwkv_g2, rwkv_kk, rwkv_ka, rwkv_rk, rwkv_lnx_w, rwkv_lnx_b, gla_gu, gla_gb, gla_norm_w, w_out, ln1_w, ln1_b, ln2_w, ln2_b, router_w, router_b, exp_w1, exp_w3, exp_w2, sh_w1, sh_w3, sh_w2):
    p = dict(ssd_conv_w=ssd_conv_w, ssd_conv_b=ssd_conv_b, ssd_dt_bias=ssd_dt_bias, ssd_a_log=ssd_a_log,
             ssd_d=ssd_d, rwkv_mu=rwkv_mu, rwkv_w0=rwkv_w0, rwkv_w2=rwkv_w2, rwkv_a0=rwkv_a0, rwkv_a2=rwkv_a2,
             rwkv_g2=rwkv_g2, rwkv_kk=rwkv_kk, rwkv_ka=rwkv_ka, rwkv_rk=rwkv_rk, gla_gu=gla_gu, gla_gb=gla_gb)
    bsz, seq, d = x.shape
    n_ctx = ctx.shape[1]
    assert n_ctx == N_CTX and d == D_MODEL and seq % TOK_BLOCK == 0 and seq // GRID_W >= 8
    depth = ada_w.shape[0]
    row = lambda a: a.reshape(1, -1)

    cond = jnp.concatenate([c, c_ctx[None], jnp.zeros((8 - bsz - 1, d), F32)], axis=0)
    mods = _ada_all(cond, ada_w, ada_b).reshape(depth, 8, N_MOD, d)
    xs = jnp.concatenate([ctx, x], axis=1)
    t = xs.shape[1]
    msels = [jnp.stack([jnp.broadcast_to(mods[l, bsz], (bsz, N_MOD, d)), mods[l, :bsz]], axis=1)
             for l in range(depth)]
    proj = _inproj(xs, msels[0], _prep_w_in(w_in[0]))
    w1_all = exp_w1.astype(BF16)
    w3_all = exp_w3.astype(BF16)
    w2_all = exp_w2.astype(BF16).reshape(depth, N_EXPERTS // EXPERTS_PER_STEP, -1, d)
    for l in range(depth):
        msel = msels[l]
        z, og, ssd_f, ssd_b, gla_f, gla_b, rf, rb, bonus, gate = _token_mixer(proj, l, p)
        x1, h, gates = _outproj(xs, msel, ssd_f, ssd_b, z, rf, rb, bonus, gate, gla_f, gla_b, og,
                                row(ssd_norm_w[l]), row(rwkv_lnx_w[l]), row(rwkv_lnx_b[l]), row(gla_norm_w[l]),
                                w_out[l].astype(BF16), row(ln1_w[l]), row(ln1_b[l]),
                                router_w[l].T, jnp.broadcast_to(router_b[l][:, None], (N_EXPERTS, 128)))
        routed = _moe(h.reshape(bsz * t, d), gates.reshape(bsz * t, 128), w1_all, w3_all, w2_all, l)
        s13 = jnp.concatenate([sh_w1[l], sh_w3[l]], axis=1).astype(BF16)
        args = (x1, h, routed.reshape(bsz, t, d), msel, s13, sh_w2[l].astype(BF16), row(ln2_w[l]), row(ln2_b[l]))
        if l == depth - 1:
            return _final(*args, None)
        xs, proj = _final(*args, (msels[l + 1], _prep_w_in(w_in[l + 1])))
```

```python
import functools

import jax
import jax.numpy as jnp
import numpy as np
from jax import lax
from jax.experimental import pallas as pl
from jax.experimental.pallas import tpu as pltpu

F32 = jnp.float32
BF16 = jnp.bfloat16

D_MODEL = 1024
N_CTX = 256
GRID_W = 64
N_MOD = 6
LN_EPS = 1e-5
DEPTH = 4
DN_ALPHA = (2 * DEPTH) ** 0.25

SSD_HEADS = 8
SSD_HEAD_DIM = 64
SSD_INNER = 512
SSD_STATE = 128
SSD_XBC = 1024

RWKV_WIDTH = 256
RWKV_HEAD_DIM = 64
RWKV_COLS = 1152
RWKV_LNX_EPS = 64e-5

GLA_K_WIDTH = 128
GLA_V_WIDTH = 256
GLA_KEY_DIM = 32
GLA_VAL_DIM = 64
GLA_GATE_LORA = 16
GLA_GATE_NORMALIZER = 16.0
GLA_SUB = 64
GLA_BLK = 16

N_EXPERTS = 64
TOP_K = 8
N_GROUPS = 8
TOPK_GROUPS = 4
EXPERT_DIM = 256
ROUTED_SCALE = 2.5
EXPERTS_PER_STEP = 4

TOK_BLOCK = 256
SCAN_BLOCK = 64
V7X_VMEM_LIMIT = 56 * 1024 * 1024

_IN_PIECES = (("z", 512), ("xbc", 1024), ("rw", 1152), ("gq", 128), ("gk", 128), ("gv", 256), ("og", 256),
              ("small", 128))
IN_PAD = sum(w for _, w in _IN_PIECES)


def _cparams(sem):
    return pltpu.CompilerParams(dimension_semantics=sem, vmem_limit_bytes=V7X_VMEM_LIMIT)


def _split3(a):
    hi = a.astype(BF16)
    r1 = a - hi.astype(F32)
    mid = r1.astype(BF16)
    lo = (r1 - mid.astype(F32)).astype(BF16)
    return hi, mid, lo


def _dot(a, b):
    return jnp.dot(a, b, preferred_element_type=F32)


def _dot_nt(a, b):
    return lax.dot_general(a, b, (((1,), (1,)), ((), ())), preferred_element_type=F32)


def _dot_x_exact(a, e):
    hi, mid, lo = _split3(a)
    return _dot(hi, e) + (_dot(mid, e) + _dot(lo, e))


def _dot_exact_x(e, a):
    hi, mid, lo = _split3(a)
    return _dot(e, hi) + (_dot(e, mid) + _dot(e, lo))


def _dot_hp(a, b):
    ah, am, _ = _split3(a)
    bh, bm, _ = _split3(b)
    return _dot(ah, bh) + (_dot(ah, bm) + _dot(am, bh))


def _sigmoid(x):
    return 1.0 / (1.0 + jnp.exp(-x))


def _silu(x):
    return x * _sigmoid(x)


def _softplus(x):
    return jnp.maximum(x, 0.0) + jnp.log(1.0 + jnp.exp(-jnp.abs(x)))


def _layernorm(x):
    mu = jnp.mean(x, axis=-1, keepdims=True)
    xc = x - mu
    var = jnp.mean(xc * xc, axis=-1, keepdims=True)
    return xc * lax.rsqrt(var + LN_EPS)


def _iota(shape, dim):
    return lax.broadcasted_iota(jnp.int32, shape, dim)


def _tri(n, upper):
    r = _iota((n, n), 0)
    c = _iota((n, n), 1)
    return jnp.where(upper, c - r, r - c) >= 0


def _group_ones(n, g):
    r = _iota((n, n), 0) // g
    c = _iota((n, n), 1) // g
    return (r == c).astype(BF16)


def _run_interleaved(gens):
    results = [None] * len(gens)
    active = list(range(len(gens)))
    while active:
        for i in list(active):
            try:
                next(gens[i])
            except StopIteration as stop:
                results[i] = stop.value
                active.remove(i)
    return results


def _ada_body(c_ref, w_ref, b_ref, o_ref):
    a = _silu(c_ref[...]).astype(BF16)
    o_ref[0] = _dot(a, w_ref[0].astype(BF16)) + b_ref[0]


def _ada_all(cond, ada_w, ada_b):
    nl, d, n = ada_w.shape
    tn = 1536
    rows = cond.shape[0]
    return pl.pallas_call(
        _ada_body,
        grid=(nl, n // tn),
        in_specs=[pl.BlockSpec((rows, d), lambda l, k: (0, 0)),
                  pl.BlockSpec((1, d, tn), lambda l, k: (l, 0, k)),
                  pl.BlockSpec((1, 1, tn), lambda l, k: (l, 0, k))],
        out_specs=pl.BlockSpec((1, rows, tn), lambda l, k: (l, 0, k)),
        out_shape=jax.ShapeDtypeStruct((nl, rows, n), F32),
        compiler_params=_cparams(("arbitrary", "arbitrary")),
        name="ada_mod",
    )(cond, ada_w, ada_b.reshape(nl, 1, n))


def _inproj_store(x, m, w_ref, out_refs):
    u = _layernorm(x) * (1.0 + m[1:2]) + m[0:1]
    y = _dot(u.astype(BF16), w_ref[...])
    off = 0
    for (_, wdt), o in zip(_IN_PIECES, out_refs):
        o[0] = y[:, off:off + wdt]
        off += wdt


def _inproj_body(ctx_ref, x_ref, m_ref, w_ref, xs_ref, *out_refs):
    x = jnp.where(pl.program_id(1) == 0, ctx_ref[0], x_ref[0])
    xs_ref[0] = x
    _inproj_store(x, m_ref[0, 0], w_ref, out_refs)


def _inproj(ctx, x, msel, w_in_p):
    b, seq, d = x.shape
    t = seq + N_CTX
    nb = t // TOK_BLOCK
    tok = lambda w: pl.BlockSpec((1, TOK_BLOCK, w), lambda i, j: (i, j, 0))
    outs = pl.pallas_call(
        _inproj_body,
        grid=(b, nb),
        in_specs=[pl.BlockSpec((1, N_CTX, d), lambda i, j: (i, 0, 0)),
                  pl.BlockSpec((1, TOK_BLOCK, d), lambda i, j: (i, jnp.maximum(j - N_CTX // TOK_BLOCK, 0), 0)),
                  pl.BlockSpec((1, 1, N_MOD, d), lambda i, j: (i, jnp.minimum(j, 1), 0, 0)),
                  pl.BlockSpec((d, IN_PAD), lambda i, j: (0, 0))],
        out_specs=[tok(d)] + [tok(w) for _, w in _IN_PIECES],
        out_shape=[jax.ShapeDtypeStruct((b, t, d), F32)] + [jax.ShapeDtypeStruct((b, t, w), F32)
                                                            for _, w in _IN_PIECES],
        compiler_params=_cparams(("arbitrary", "arbitrary")),
        name="in_proj",
    )(ctx, x, msel, w_in_p)
    return outs[0], outs[1:]


def _ssd_body(nbatch, nb, fwd_refs, bwd_refs, cw_ref, cb_ref, dtb_ref, alog_ref, dsk_ref, yf_ref, yb_ref,
              h_scr, buf_scr):
    q = TOK_BLOCK
    j = pl.program_id(0)

    @pl.when(j == 0)
    def _():
        h_scr[...] = jnp.zeros_like(h_scr)

    lane = _iota((1, 128), 1)
    neg_a = jnp.where(lane < 2 * SSD_HEADS, -jnp.exp(alog_ref[...]), 0.0)
    cw = cw_ref[0]
    er = _iota((128, SSD_INNER), 0)
    ec = _iota((128, SSD_INNER), 1)
    er8 = _iota((128, 128), 0)
    ec8 = _iota((128, 128), 1)
    lane_q = _iota((q, 128), 1)
    hpg = SSD_HEADS // 2

    def stream(refs, bi, upper, si, blk):
        xbc_ref, hp_ref, hn_ref, small_ref = refs
        d = 1 if upper else 0
        seg_first = (blk == 0) | (blk == 1)
        seg_last = (blk == 0) | (blk == nb - 1)
        cur = xbc_ref[bi]
        buf_scr[8:8 + q] = cur
        buf_scr[0:8] = jnp.where(seg_first, 0.0, hp_ref[bi])
        buf_scr[8 + q:16 + q] = jnp.where(seg_last, 0.0, hn_ref[bi])
        prev = buf_scr[7:7 + q]
        nxt = buf_scr[9:9 + q]
        act = _silu(prev * cw[0:1] + cur * cw[1:2] + nxt * cw[2:3] + cb_ref[...])
        xs = act[:, :SSD_INNER]
        b_grp = [act[:, 512 + 128 * g:640 + 128 * g] for g in range(2)]
        c_grp = [act[:, 768 + 128 * g:896 + 128 * g].astype(BF16) for g in range(2)]

        tri_b = _tri(q, upper)
        tri = tri_b.astype(BF16)
        tri_t = _tri(q, not upper).astype(BF16)
        dt_all = _softplus(small_ref[bi] + dtb_ref[...])
        la_all = dt_all * neg_a
        e_exp = (er == SSD_HEADS * d + ec // SSD_HEAD_DIM).astype(BF16)
        e8 = ((er8 == SSD_HEADS * d + ec8) & (ec8 < SSD_HEADS)).astype(BF16)
        e_head = (er == ec // SSD_HEAD_DIM).astype(BF16)
        dt_exp = _dot_x_exact(dt_all, e_exp)
        la8 = _dot_x_exact(la_all, e8)
        cs8 = _dot_exact_x(tri, la8)
        cs_t = _dot_x_exact(la8.T, tri_t)
        cs_exp = _dot_x_exact(cs8, e_head)
        tot_exp = cs_exp[0:1] if upper else cs_exp[q - 1:q]

        xdt = xs * dt_exp
        gmats = [_dot_nt(c_grp[g], b_grp[g].astype(BF16)) for g in range(2)]
        ys = []
        for p in range(SSD_HEADS // 2):
            xdt_p = xdt[:, 128 * p:128 * (p + 1)].astype(BF16)
            halves = []
            for h in (2 * p, 2 * p + 1):
                lmat = jnp.where(tri_b, jnp.exp(cs8[:, h:h + 1] - cs_t[h:h + 1, :]), 0.0)
                halves.append(_dot((gmats[h // hpg] * lmat).astype(BF16), xdt_p))
            ys.append(jnp.where(lane_q < SSD_HEAD_DIM, halves[0], halves[1]))
        y = jnp.concatenate(ys, axis=1)

        xd = (xdt * jnp.exp(tot_exp - cs_exp)).astype(BF16)
        offs = []
        for g in range(2):
            hg = h_scr[si, g]
            offs.append(_dot(c_grp[g], hg.astype(BF16)))
            hn = _dot(b_grp[g].T.astype(BF16), xd[:, 256 * g:256 * (g + 1)])
            h_scr[si, g] = hg * jnp.exp(tot_exp[:, 256 * g:256 * (g + 1)]) + hn
        y = y + jnp.concatenate(offs, axis=1) * jnp.exp(cs_exp)
        if not upper:
            y = y + dsk_ref[...] * xs
        return y

    blk_b = jnp.where(j == 0, 0, nb - j)
    for bi in range(nbatch):
        yf_ref[bi] = stream(fwd_refs, bi, False, bi, j)
        yb_ref[bi] = stream(bwd_refs, bi, True, nbatch + bi, blk_b)


def _ssd(xbc, small, conv_w, conv_b, dtb_pad, alog_pad, dskip_exp):
    b, t, _ = xbc.shape
    nb = t // TOK_BLOCK
    r8 = TOK_BLOCK // 8

    def bwd_blk(j):
        return jnp.where(j == 0, 0, nb - j)

    def specs(blk):
        return [pl.BlockSpec((b, TOK_BLOCK, SSD_XBC), lambda j: (0, blk(j), 0)),
                pl.BlockSpec((b, 8, SSD_XBC), lambda j: (0, jnp.maximum(blk(j) * r8 - 1, 0), 0)),
                pl.BlockSpec((b, 8, SSD_XBC), lambda j: (0, jnp.minimum(blk(j) * r8 + r8, t // 8 - 1), 0)),
                pl.BlockSpec((b, TOK_BLOCK, 128), lambda j: (0, blk(j), 0))]

    def body(*refs):
        _ssd_body(b, nb, refs[0:4], refs[4:8], *refs[8:])

    const2 = lambda j: (0, 0)
    oshape = jax.ShapeDtypeStruct((b, t, SSD_INNER), F32)
    return pl.pallas_call(
        body,
        grid=(nb,),
        in_specs=specs(lambda j: j) + specs(bwd_blk) + [
            pl.BlockSpec((1, 3, SSD_XBC), lambda j: (0, 0, 0)),
            pl.BlockSpec((1, SSD_XBC), const2),
            pl.BlockSpec((1, 128), const2),
            pl.BlockSpec((1, 128), const2),
            pl.BlockSpec((1, SSD_INNER), const2)],
        out_specs=[pl.BlockSpec((b, TOK_BLOCK, SSD_INNER), lambda j: (0, j, 0)),
                   pl.BlockSpec((b, TOK_BLOCK, SSD_INNER), lambda j: (0, bwd_blk(j), 0))],
        out_shape=[oshape, oshape],
        scratch_shapes=[pltpu.VMEM((2 * b, 2, SSD_STATE, 256), F32),
                        pltpu.VMEM((TOK_BLOCK + 16, SSD_XBC), F32)],
        compiler_params=_cparams(("arbitrary",)),
        name="ssd",
    )(xbc, xbc, xbc, small, xbc, xbc, xbc, small, conv_w[None], conv_b, dtb_pad, alog_pad, dskip_exp)


def _gla_body(nbatch, fwd_refs, bwd_refs, gu_ref, gb_ref, of_ref, ob_ref, s_scr):
    j = pl.program_id(0)
    c = GLA_SUB
    nsub = TOK_BLOCK // c

    @pl.when(j == 0)
    def _():
        s_scr[...] = jnp.zeros_like(s_scr)

    lr = _iota((128, 128), 0)
    lc = _iota((128, 128), 1)
    lane_k = _iota((c, GLA_K_WIDTH), 1) // GLA_KEY_DIM
    lane_kx = _iota((GLA_BLK * (c // GLA_BLK) * (c // GLA_BLK - 1) // 2, GLA_K_WIDTH), 1) // GLA_KEY_DIM
    lane_v = _iota((c, GLA_V_WIDTH), 1) // GLA_VAL_DIM
    st_mask = (_iota((GLA_V_WIDTH, GLA_K_WIDTH), 0) // GLA_VAL_DIM
               == _iota((GLA_V_WIDTH, GLA_K_WIDTH), 1) // GLA_KEY_DIM)

    nblk = c // GLA_BLK
    tok_r = _iota((c, c), 0)
    tok_c = _iota((c, c), 1)
    col_blk = _iota((GLA_BLK, c), 1) // GLA_BLK

    def direction_consts(upper):
        first = (tok_r // GLA_BLK) * GLA_BLK + (GLA_BLK - 1 if upper else 0)
        last = (tok_r // GLA_BLK) * GLA_BLK + (0 if upper else GLA_BLK - 1)
        pairs = [(ti, tj) for ti in range(nblk) for tj in range(nblk) if (tj > ti if upper else tj < ti)]
        return dict(tri_b=_tri(c, upper), sel_first=(tok_c == first).astype(BF16),
                    sel_last=(tok_c == last).astype(BF16), pairs=pairs)

    consts = {False: direction_consts(False), True: direction_consts(True)}

    def stream(refs, bi, upper, st):
        q_ref, k_ref, v_ref, small_ref = refs
        d = 1 if upper else 0
        cst = consts[upper]
        tri_b = cst["tri_b"]
        tri = tri_b.astype(BF16)
        gsel = ((lr == 2 * SSD_HEADS + GLA_GATE_LORA * d + lc) & (lc < GLA_GATE_LORA)).astype(BF16)
        gd = _dot_x_exact(small_ref[bi], gsel)
        yield
        pre = _dot_hp(gd, gu_ref[d]) + gb_ref[d:d + 1]
        la_all = -_softplus(-pre) * (1.0 / GLA_GATE_NORMALIZER)
        yield
        outs = [None] * nsub
        for si in range(nsub):
            sub = nsub - 1 - si if upper else si
            lo = sub * c
            la = la_all[lo:lo + c]
            qq = q_ref[bi, lo:lo + c, :] * (GLA_KEY_DIM ** -0.5)
            kk = k_ref[bi, lo:lo + c, :]
            vv = v_ref[bi, lo:lo + c, :]
            cs = _dot_exact_x(tri, la)
            yield
            rs = _dot_exact_x(cst["sel_first"], cs - la)
            re = _dot_exact_x(cst["sel_last"], cs)
            yield
            tot = jnp.sum(la, axis=0, keepdims=True)
            qt = qq * jnp.exp(cs)
            kdec = (kk * jnp.exp(tot - cs)).astype(BF16)
            q_in = qq * jnp.exp(cs - rs)
            k_in = (kk * jnp.exp(jnp.minimum(rs - cs, 80.0))).astype(BF16)
            k_out = (kk * jnp.exp(re - cs)).astype(BF16)
            q_cross = jnp.concatenate(
                [q_in[GLA_BLK * ti:GLA_BLK * (ti + 1)]
                 * jnp.exp(rs[GLA_BLK * ti:GLA_BLK * ti + 1] - re[GLA_BLK * tj:GLA_BLK * tj + 1])
                 for ti, tj in cst["pairs"]], axis=0)
            vb = vv.astype(BF16)
            o = _dot_nt(qt.astype(BF16), st.astype(BF16))
            amats = []
            for h in range(GLA_K_WIDTH // GLA_KEY_DIM):
                r_in = _dot_nt(jnp.where(lane_k == h, q_in, 0.0).astype(BF16), k_in)
                r_cross = _dot_nt(jnp.where(lane_kx == h, q_cross, 0.0).astype(BF16), k_out)
                rows = []
                for ti in range(nblk):
                    blk_rows = slice(GLA_BLK * ti, GLA_BLK * (ti + 1))
                    part = jnp.where(tri_b[blk_rows] & (col_blk == ti), r_in[blk_rows], 0.0)
                    for pi, (pti, ptj) in enumerate(cst["pairs"]):
                        if pti == ti:
                            part = part + jnp.where(col_blk == ptj, r_cross[GLA_BLK * pi:GLA_BLK * (pi + 1)], 0.0)
                    rows.append(part)
                amats.append(jnp.concatenate(rows, axis=0).astype(BF16))
            upd = _dot(vv.T.astype(BF16), kdec)
            yield
            for h, a in enumerate(amats):
                o = o + jnp.where(lane_v == h, _dot(a, vb), 0.0)
            st = st * jnp.exp(tot) + jnp.where(st_mask, upd, 0.0)
            outs[sub] = o
            yield
        return jnp.concatenate(outs, axis=0), st

    gens = ([stream(fwd_refs, bi, False, s_scr[bi]) for bi in range(nbatch)]
            + [stream(bwd_refs, bi, True, s_scr[nbatch + bi]) for bi in range(nbatch)])
    for si, (o, st) in enumerate(_run_interleaved(gens)):
        if si < nbatch:
            of_ref[si] = o
        else:
            ob_ref[si - nbatch] = o
        s_scr[si] = st


def _gla(gq, gk, gv, small, gu_pad, gb):
    b, t, _ = gq.shape
    nb = t // TOK_BLOCK

    def bwd_blk(j):
        return jnp.where(j == 0, 0, nb - j)

    def specs(blk):
        return [pl.BlockSpec((b, TOK_BLOCK, GLA_K_WIDTH), lambda j: (0, blk(j), 0)),
                pl.BlockSpec((b, TOK_BLOCK, GLA_K_WIDTH), lambda j: (0, blk(j), 0)),
                pl.BlockSpec((b, TOK_BLOCK, GLA_V_WIDTH), lambda j: (0, blk(j), 0)),
                pl.BlockSpec((b, TOK_BLOCK, 128), lambda j: (0, blk(j), 0))]

    def body(*refs):
        _gla_body(b, refs[0:4], refs[4:8], *refs[8:])

    oshape = jax.ShapeDtypeStruct((b, t, GLA_V_WIDTH), F32)
    return pl.pallas_call(
        body,
        grid=(nb,),
        in_specs=specs(lambda j: j) + specs(bwd_blk) + [
            pl.BlockSpec((2, 128, GLA_K_WIDTH), lambda j: (0, 0, 0)),
            pl.BlockSpec((2, GLA_K_WIDTH), lambda j: (0, 0))],
        out_specs=[pl.BlockSpec((b, TOK_BLOCK, GLA_V_WIDTH), lambda j: (0, j, 0)),
                   pl.BlockSpec((b, TOK_BLOCK, GLA_V_WIDTH), lambda j: (0, bwd_blk(j), 0))],
        out_shape=[oshape, oshape],
        scratch_shapes=[pltpu.VMEM((2 * b, GLA_V_WIDTH, GLA_K_WIDTH), F32)],
        compiler_params=_cparams(("arbitrary",)),
        name="gla",
    )(gq, gk, gv, small, gq, gk, gv, small, gu_pad, gb)


_RWKV_OUTS = ("r", "v", "kk", "lw_f", "k_f", "kka_f", "lw_b", "k_b", "kka_b", "bonus", "gate")


def _rwkv_prep_body(nb, rw_ref, hp_ref, hn_ref, mu_ref, w0_ref, w2_ref, a0_ref, a2_ref, g2_ref, kkp_ref, ka_ref,
                    rk_ref, *refs):
    outs = dict(zip(_RWKV_OUTS, refs[:len(_RWKV_OUTS)]))
    buf_scr = refs[len(_RWKV_OUTS)]
    q = TOK_BLOCK
    j = pl.program_id(1)
    is_ctx = j == 0
    z = rw_ref[0]
    zeros8 = jnp.zeros((8, RWKV_COLS), F32)
    buf_scr[0:8] = zeros8
    buf_scr[8 + q:16 + q] = zeros8
    buf_scr[8:8 + q] = z
    t = _iota((q, 1), 0)
    col = t % GRID_W
    prev = jnp.where(jnp.where(is_ctx, t, col) == 0, 0.0, buf_scr[7:7 + q])
    nxt = jnp.where(jnp.where(is_ctx, t - (q - 1), col - (GRID_W - 1)) == 0, 0.0, buf_scr[9:9 + q])
    up = jnp.concatenate([jnp.where(j == 1, 0.0, hp_ref[0]), z[:q - GRID_W]], axis=0)
    down = jnp.concatenate([z[GRID_W:], jnp.where(j == nb - 1, 0.0, hn_ref[0])], axis=0)
    mu = mu_ref[...]
    vert = mu[2:3] * (up - z) + mu[3:4] * (down - z)
    mixed = z + mu[0:1] * (prev - z) + mu[1:2] * (nxt - z) + jnp.where(is_ctx, 0.0, vert)

    w = RWKV_WIDTH
    r = mixed[:, 0:w]
    k = mixed[:, w:2 * w]
    v = mixed[:, 2 * w:3 * w]
    wd = mixed[:, 3 * w:3 * w + 128]
    ad = mixed[:, 3 * w + 128:4 * w]
    gdr = mixed[:, 4 * w:4 * w + 128]

    lw = w0_ref[...] + _dot(jnp.tanh(wd).astype(BF16), w2_ref[...])
    log_dec = -jnp.exp(-_softplus(-lw) - 0.5)
    a = _sigmoid(a0_ref[...] + _dot(ad.astype(BF16), a2_ref[...]))
    ones64 = _group_ones(w, RWKV_HEAD_DIM)
    kkv = k * kkp_ref[...]
    nrm = jnp.maximum(jnp.sqrt(_dot_x_exact(kkv * kkv, ones64)), 1e-12)
    kkn = kkv / nrm
    ka = ka_ref[...]
    a_f, a_b = a[:, :w], a[:, w:]
    k_f = k * (1.0 + (a_f - 1.0) * ka)
    k_b = k * (1.0 + (a_b - 1.0) * ka)
    outs["r"][0] = r
    outs["v"][0] = v
    outs["kk"][0] = kkn
    outs["lw_f"][0] = log_dec[:, :w]
    outs["lw_b"][0] = log_dec[:, w:]
    outs["k_f"][0] = k_f
    outs["k_b"][0] = k_b
    outs["kka_f"][0] = kkn * a_f
    outs["kka_b"][0] = kkn * a_b
    outs["bonus"][0] = _dot_x_exact(r * (k_f + k_b) * rk_ref[...], ones64) * v
    outs["gate"][0] = _dot(_sigmoid(gdr).astype(BF16), g2_ref[...])


def _rwkv_prep(rw, mu, w0cat, w2bd, a0cat, a2bd, g2, kkp, ka, rk):
    b, t, _ = rw.shape
    nb = t // TOK_BLOCK
    hb = TOK_BLOCK // GRID_W
    w = RWKV_WIDTH
    c2 = lambda i, j: (0, 0)
    return pl.pallas_call(
        functools.partial(_rwkv_prep_body, nb),
        grid=(b, nb),
        in_specs=[pl.BlockSpec((1, TOK_BLOCK, RWKV_COLS), lambda i, j: (i, j, 0)),
                  pl.BlockSpec((1, GRID_W, RWKV_COLS), lambda i, j: (i, jnp.maximum(j * hb - 1, 0), 0)),
                  pl.BlockSpec((1, GRID_W, RWKV_COLS),
                               lambda i, j: (i, jnp.minimum(j * hb + hb, t // GRID_W - 1), 0)),
                  pl.BlockSpec((4, RWKV_COLS), c2),
                  pl.BlockSpec((1, 2 * w), c2),
                  pl.BlockSpec((128, 2 * w), c2),
                  pl.BlockSpec((1, 2 * w), c2),
                  pl.BlockSpec((128, 2 * w), c2),
                  pl.BlockSpec((128, w), c2),
                  pl.BlockSpec((1, w), c2),
                  pl.BlockSpec((1, w), c2),
                  pl.BlockSpec((1, w), c2)],
        out_specs=[pl.BlockSpec((1, TOK_BLOCK, w), lambda i, j: (i, j, 0)) for _ in _RWKV_OUTS],
        out_shape=[jax.ShapeDtypeStruct((b, t, w), F32) for _ in _RWKV_OUTS],
        scratch_shapes=[pltpu.VMEM((TOK_BLOCK + 16, RWKV_COLS), F32)],
        compiler_params=_cparams(("arbitrary", "arbitrary")),
        name="rwkv_prep",
    )(rw, rw, rw, mu, w0cat, w2bd, a0cat, a2bd, g2, kkp, ka, rk)


def _rwkv_scan_body(nbatch, fwd_refs, bwd_refs, yf_ref, yb_ref, s_scr):
    j = pl.program_id(0)
    n = SCAN_BLOCK
    hd = RWKV_HEAD_DIM

    @pl.when(j == 0)
    def _():
        s_scr[...] = jnp.zeros_like(s_scr)

    nh = RWKV_WIDTH // hd
    w = RWKV_WIDTH
    assert n == hd
    lane_head = _iota((1, w), 1) // hd
    cat_c = _iota((n, nh * n), 0)
    cat_cc = _iota((n, nh * n), 1) % n
    eye_cat = (cat_c == cat_cc).astype(F32)
    bd_mask = (_iota((w, w), 0) // hd) == (_iota((w, w), 1) // hd)

    def stack(x):
        return jnp.concatenate([jnp.where(lane_head == h, x, 0.0) for h in range(nh)], axis=0)

    def chunk(refs, bi, upper, s_prev):
        r, v, kk, lw, k, kka = (ref[bi] for ref in refs)
        tri = _tri(n, upper).astype(BF16)
        strict = (cat_cc > cat_c) if upper else (cat_cc < cat_c)
        incl = (cat_cc >= cat_c) if upper else (cat_cc <= cat_c)
        cs = _dot_exact_x(tri, lw)
        yield
        tot = jnp.sum(lw, axis=0, keepdims=True)
        g_inv = jnp.exp(-cs)
        dec_end = jnp.exp(tot - cs)
        a_t = (-kk * jnp.exp(cs - lw)).astype(BF16)
        r_t = (r * jnp.exp(cs)).astype(BF16)
        bk_s = jnp.concatenate([stack(kka * g_inv), stack(k * g_inv)], axis=0).astype(BF16)
        bk_dec = jnp.concatenate([kka * dec_end, k * dec_end], axis=0).astype(BF16)
        v_s = stack(v).astype(BF16)
        x = _dot_nt(jnp.concatenate([a_t, r_t], axis=0), bk_s)
        yield
        nmat = jnp.where(strict, x[:n, :nh * n], 0.0)
        ak = jnp.where(strict, x[:n, nh * n:], 0.0).astype(BF16)
        rb = jnp.where(incl, x[n:, :nh * n], 0.0).astype(BF16)
        rk = jnp.where(incl, x[n:, nh * n:], 0.0).astype(BF16)
        s_b = s_prev.astype(BF16)
        w0 = _dot_nt(a_t, s_b) + _dot(ak, v_s)
        y0 = _dot_nt(r_t, s_b) + _dot(rk, v_s)
        tinv = eye_cat + nmat
        p = nmat.astype(BF16)
        p_s = stack(nmat).astype(BF16)
        for i in range(n.bit_length() - 2):
            p2 = _dot(p, p_s)
            yield
            p_s = stack(p2).astype(BF16)
            tinv = tinv + _dot(tinv.astype(BF16), p_s)
            p = p2.astype(BF16)
            yield
        u = _dot(tinv.astype(BF16), stack(w0).astype(BF16))
        yield
        y = y0 + _dot(rb, stack(u).astype(BF16))
        uv = jnp.concatenate([u, v], axis=0).astype(BF16)
        upd = lax.dot_general(uv, bk_dec, (((0,), (0,)), ((), ())), preferred_element_type=F32)
        s_new = s_prev * jnp.exp(tot) + jnp.where(bd_mask, upd, 0.0)
        return y, s_new

    gens = ([chunk(fwd_refs, bi, False, s_scr[bi]) for bi in range(nbatch)]
            + [chunk(bwd_refs, bi, True, s_scr[nbatch + bi]) for bi in range(nbatch)])
    for si, (y, s_new) in enumerate(_run_interleaved(gens)):
        if si < nbatch:
            yf_ref[si] = y
        else:
            yb_ref[si - nbatch] = y
        s_scr[si] = s_new


def _rwkv_scan(pre):
    b, t, w = pre["r"].shape
    nbs = t // SCAN_BLOCK
    ctx_blocks = N_CTX // SCAN_BLOCK

    def fwd_blk(j):
        return j

    def bwd_blk(j):
        return jnp.where(j < ctx_blocks, ctx_blocks - 1 - j, nbs + ctx_blocks - 1 - j)

    in_f = pl.BlockSpec((b, SCAN_BLOCK, w), lambda j: (0, fwd_blk(j), 0))
    in_b = pl.BlockSpec((b, SCAN_BLOCK, w), lambda j: (0, bwd_blk(j), 0))
    oshape = jax.ShapeDtypeStruct((b, t, w), F32)

    def body(*refs):
        _rwkv_scan_body(b, refs[0:6], refs[6:12], refs[12], refs[13], refs[14])

    return pl.pallas_call(
        body,
        grid=(nbs,),
        in_specs=[in_f] * 6 + [in_b] * 6,
        out_specs=[in_f, in_b],
        out_shape=[oshape, oshape],
        scratch_shapes=[pltpu.VMEM((2 * b, w, w), F32)],
        compiler_params=_cparams(("arbitrary",)),
        name="rwkv_scan",
    )(pre["r"], pre["v"], pre["kk"], pre["lw_f"], pre["k_f"], pre["kka_f"],
      pre["r"], pre["v"], pre["kk"], pre["lw_b"], pre["k_b"], pre["kka_b"])


def _outproj_body(x_ref, m_ref, s0_ref, s1_ref, z_ref, rf_ref, rb_ref, bonus_ref, gate_ref, g0_ref, g1_ref, og_ref,
                  snw_ref, lxw_ref, lxb_ref, gnw_ref, wo_ref, l1w_ref, l1b_ref, rtw_ref, rtb_ref,
                  x1_ref, h_ref, g_ref):
    m = m_ref[0, 0]
    ys = (s0_ref[0] + s1_ref[0]) * _silu(z_ref[0])
    gw = SSD_INNER // 2
    parts = []
    for g in range(2):
        part = ys[:, gw * g:gw * (g + 1)]
        parts.append(part * lax.rsqrt(jnp.mean(part * part, axis=-1, keepdims=True) + 1e-5))
    y_ssd = jnp.concatenate(parts, axis=1) * snw_ref[...]
    ones64 = _group_ones(RWKV_WIDTH, RWKV_HEAD_DIM)
    inv = 1.0 / RWKV_HEAD_DIM
    yr = rf_ref[0] + rb_ref[0]
    mu = _dot_x_exact(yr, ones64) * inv
    yc = yr - mu
    var = _dot_x_exact(yc * yc, ones64) * inv
    yn = yc * lax.rsqrt(var + RWKV_LNX_EPS) * lxw_ref[...] + lxb_ref[...]
    y_rwkv = (yn + bonus_ref[0]) * gate_ref[0]
    o = g0_ref[0] + g1_ref[0]
    ms = _dot_x_exact(o * o, ones64) * (1.0 / GLA_VAL_DIM)
    y_gla = o * lax.rsqrt(ms + 1e-5) * gnw_ref[...] * _silu(og_ref[0])

    y_mix = jnp.concatenate([y_ssd, y_rwkv, y_gla], axis=1).astype(BF16)
    mix = _dot(y_mix, wo_ref[...])
    x1 = _layernorm(DN_ALPHA * x_ref[0] + m[2:3] * mix) * l1w_ref[...] + l1b_ref[...]
    x1_ref[0] = x1
    h = _layernorm(x1) * (1.0 + m[4:5]) + m[3:4]
    h_ref[0] = h.astype(BF16)
    g_ref[0] = _route(h, rtw_ref[...], rtb_ref[...])


def _outproj(xs, msel, ssd_f, ssd_b, z, rf, rb, bonus, gate, gla_f, gla_b, og, snw, lxw, lxb, gnw, wo, l1w, l1b, rtw, rtb):
    b, t, d = xs.shape
    nb = t // TOK_BLOCK
    tok = lambda w: pl.BlockSpec((1, TOK_BLOCK, w), lambda i, j: (i, j, 0))
    c2 = lambda i, j: (0, 0)
    row = lambda w: pl.BlockSpec((1, w), c2)
    return pl.pallas_call(
        _outproj_body,
        grid=(b, nb),
        in_specs=[tok(d),
                  pl.BlockSpec((1, 1, N_MOD, d), lambda i, j: (i, jnp.minimum(j, 1), 0, 0)),
                  tok(SSD_INNER), tok(SSD_INNER), tok(SSD_INNER),
                  tok(RWKV_WIDTH), tok(RWKV_WIDTH), tok(RWKV_WIDTH), tok(RWKV_WIDTH),
                  tok(GLA_V_WIDTH), tok(GLA_V_WIDTH), tok(GLA_V_WIDTH),
                  row(SSD_INNER), row(RWKV_WIDTH), row(RWKV_WIDTH), row(GLA_V_WIDTH),
                  pl.BlockSpec((d, d), c2), row(d), row(d),
                  pl.BlockSpec((N_EXPERTS, d), c2), pl.BlockSpec((N_EXPERTS, 128), c2)],
        out_specs=[tok(d), tok(d), tok(128)],
        out_shape=[jax.ShapeDtypeStruct((b, t, d), F32), jax.ShapeDtypeStruct((b, t, d), BF16),
                   jax.ShapeDtypeStruct((b, t, 128), F32)],
        compiler_params=_cparams(("arbitrary", "arbitrary")),
        name="out_proj_router",
    )(xs, msel, ssd_f, ssd_b, z, rf, rb, bonus, gate, gla_f, gla_b, og, snw, lxw, lxb, gnw, wo, l1w, l1b, rtw, rtb)


def _route(h, rw_t, rb_t):
    tm = h.shape[0]
    ne = N_EXPERTS
    per_group = ne // N_GROUPS
    ah, am, _ = _split3(rw_t)
    bh, bm, _ = _split3(h)
    logits = _dot_nt(ah, bh) + (_dot_nt(ah, bm) + _dot_nt(am, bh))
    scores = _sigmoid(logits)
    sel = scores + jnp.concatenate([rb_t] * (tm // 128), axis=1)
    neg = -jnp.inf
    sub8 = _iota((per_group, tm), 0).astype(F32)
    eidx = _iota((ne, tm), 0).astype(F32)

    def first_max(cur, idx, sentinel):
        mx = jnp.max(cur, axis=0, keepdims=True)
        first = jnp.min(jnp.where(cur == mx, idx, sentinel), axis=0, keepdims=True)
        return mx, idx == first

    gs = []
    for g in range(N_GROUPS):
        blk = sel[per_group * g:per_group * (g + 1)]
        m1, hit = first_max(blk, sub8, float(per_group))
        m2 = jnp.max(jnp.where(hit, neg, blk), axis=0, keepdims=True)
        gs.append(m1 + m2)
    cur = jnp.concatenate(gs, axis=0)
    allowed_g = jnp.zeros((N_GROUPS, tm), F32)
    for _ in range(TOPK_GROUPS):
        _, hit = first_max(cur, sub8, float(N_GROUPS))
        cur = jnp.where(hit, neg, cur)
        allowed_g = jnp.where(hit, 1.0, allowed_g)
    allowed = jnp.concatenate([jnp.broadcast_to(allowed_g[g:g + 1], (per_group, tm)) for g in range(N_GROUPS)],
                              axis=0)
    cur = jnp.where(allowed > 0.0, sel, neg)
    wts = jnp.zeros((ne, tm), F32)
    for _ in range(TOP_K):
        _, hit = first_max(cur, eidx, float(ne))
        cur = jnp.where(hit, neg, cur)
        wts = jnp.where(hit, scores, wts)
    gates_t = wts / jnp.sum(wts, axis=0, keepdims=True) * ROUTED_SCALE
    return jnp.concatenate([gates_t, jnp.zeros((128 - ne, tm), F32)], axis=0).T


def _moe_body(h_ref, g_ref, w1_ref, w3_ref, w2_ref, o_ref):
    e0 = pl.program_id(1) * EXPERTS_PER_STEP

    @pl.when(pl.program_id(1) == 0)
    def _():
        o_ref[...] = jnp.zeros_like(o_ref)

    hb = h_ref[...]
    gates = g_ref[...]
    lane = _iota(gates.shape, 1)
    hids = []
    for e in range(EXPERTS_PER_STEP):
        gcol = jnp.sum(jnp.where(lane == e0 + e, gates, 0.0), axis=-1, keepdims=True)
        hid = _silu(_dot(hb, w1_ref[e])) * _dot(hb, w3_ref[e])
        hids.append((hid * gcol).astype(BF16))
    o_ref[...] += _dot(jnp.concatenate(hids, axis=1), w2_ref[0])


def _moe(h2, gates, w1, w3, w2g, layer):
    n, d = h2.shape
    tm = n // 8 if n % (8 * 128) == 0 else 1024
    _, ne, _, f = w1.shape
    eps = EXPERTS_PER_STEP
    return pl.pallas_call(
        _moe_body,
        grid=(n // tm, ne // eps),
        in_specs=[pl.BlockSpec((tm, d), lambda i, g: (i, 0)),
                  pl.BlockSpec((tm, 128), lambda i, g: (i, 0)),
                  pl.BlockSpec((None, eps, d, f), lambda i, g: (layer, g, 0, 0)),
                  pl.BlockSpec((None, eps, d, f), lambda i, g: (layer, g, 0, 0)),
                  pl.BlockSpec((None, 1, eps * f, d), lambda i, g: (layer, g, 0, 0))],
        out_specs=pl.BlockSpec((tm, d), lambda i, g: (i, 0)),
        out_shape=jax.ShapeDtypeStruct((n, d), F32),
        compiler_params=_cparams(("arbitrary", "arbitrary")),
        name="moe_experts",
    )(h2, gates, w1, w3, w2g)


def _ffn_residual(x1_ref, h_ref, rt_ref, m, s13_ref, s2_ref, l2w_ref, l2b_ref):
    a = _dot(h_ref[0], s13_ref[...])
    fs = a.shape[1] // 2
    hid = (_silu(a[:, :fs]) * a[:, fs:]).astype(BF16)
    f = rt_ref[0] + _dot(hid, s2_ref[...])
    return _layernorm(DN_ALPHA * x1_ref[0] + m[5:6] * f) * l2w_ref[...] + l2b_ref[...]


def _final_body(x1_ref, h_ref, rt_ref, m_ref, s13_ref, s2_ref, l2w_ref, l2b_ref, o_ref):
    o_ref[0] = _ffn_residual(x1_ref, h_ref, rt_ref, m_ref[0, 0], s13_ref, s2_ref, l2w_ref, l2b_ref)


def _final_next_body(x1_ref, h_ref, rt_ref, m_ref, s13_ref, s2_ref, l2w_ref, l2b_ref, mn_ref, w_ref, o_ref,
                     *proj_refs):
    x2 = _ffn_residual(x1_ref, h_ref, rt_ref, m_ref[0, 0], s13_ref, s2_ref, l2w_ref, l2b_ref)
    o_ref[0] = x2
    _inproj_store(x2, mn_ref[0, 0], w_ref, proj_refs)


def _final(x1, h, routed, msel, s13, s2, l2w, l2b, nxt):
    b, t, d = x1.shape
    nb = t // TOK_BLOCK
    skip = N_CTX // TOK_BLOCK if nxt is None else 0
    tok = pl.BlockSpec((1, TOK_BLOCK, d), lambda i, j: (i, j + skip, 0))
    mspec = pl.BlockSpec((1, 1, N_MOD, d), lambda i, j: (i, jnp.minimum(j + skip, 1), 0, 0))
    c2 = lambda i, j: (0, 0)
    out_tok = lambda w: pl.BlockSpec((1, TOK_BLOCK, w), lambda i, j: (i, j, 0))
    in_specs = [tok, tok, out_tok(d) if nxt is None else tok, mspec, pl.BlockSpec(s13.shape, c2),
                pl.BlockSpec(s2.shape, c2), pl.BlockSpec((1, d), c2), pl.BlockSpec((1, d), c2)]
    if nxt is None:
        return pl.pallas_call(
            _final_body,
            grid=(b, nb - skip),
            in_specs=in_specs,
            out_specs=out_tok(d),
            out_shape=jax.ShapeDtypeStruct((b, t - skip * TOK_BLOCK, d), F32),
            compiler_params=_cparams(("arbitrary", "arbitrary")),
            name="shared_ffn_ln",
        )(x1, h, routed, msel, s13, s2, l2w, l2b)
    msel_n, w_in_n = nxt
    outs = pl.pallas_call(
        _final_next_body,
        grid=(b, nb),
        in_specs=in_specs + [mspec, pl.BlockSpec((d, IN_PAD), c2)],
        out_specs=[out_tok(d)] + [out_tok(w) for _, w in _IN_PIECES],
        out_shape=[jax.ShapeDtypeStruct((b, t, d), F32)] + [jax.ShapeDtypeStruct((b, t, w), F32)
                                                            for _, w in _IN_PIECES],
        compiler_params=_cparams(("arbitrary", "arbitrary")),
        name="shared_ffn_ln_in_proj",
    )(x1, h, routed, msel, s13, s2, l2w, l2b, msel_n, w_in_n)
    return outs[0], outs[1:]


def _prep_w_in(w):
    z, xbc, dt, rw, gq, gk, gv, gd, og = jnp.split(
        w, np.cumsum([512, 1024, 16, 1152, 128, 128, 256, 32, 256])[:-1].tolist(), axis=1)
    small = jnp.concatenate([dt, gd, jnp.zeros((w.shape[0], 128 - 48), w.dtype)], axis=1)
    return jnp.concatenate([z, xbc, rw, gq, gk, gv, og, small], axis=1).astype(BF16)


def _block_diag2(m):
    r, c = m.shape[1:]
    zero = jnp.zeros((r, c), m.dtype)
    return jnp.concatenate([jnp.concatenate([m[0], zero], axis=1),
                            jnp.concatenate([zero, m[1]], axis=1)], axis=0)


def _pad_lanes(v, n=128):
    v = v.reshape(1, -1)
    return jnp.pad(v, ((0, 0), (0, n - v.shape[1])))


def _token_mixer(proj, l, p):
    z, xbc, rw, gq, gk, gv, og, small = proj
    ssd_f, ssd_b = _ssd(xbc, small, p["ssd_conv_w"][l], p["ssd_conv_b"][l].reshape(1, -1),
                 _pad_lanes(p["ssd_dt_bias"][l]), _pad_lanes(p["ssd_a_log"][l]),
                 jnp.repeat(p["ssd_d"][l], SSD_HEAD_DIM).reshape(1, -1))
    gu_pad = jnp.pad(p["gla_gu"][l], ((0, 0), (0, 128 - GLA_GATE_LORA), (0, 0)))
    gla_f, gla_b = _gla(gq, gk, gv, small, gu_pad, p["gla_gb"][l])
    pre = dict(zip(_RWKV_OUTS, _rwkv_prep(
        rw, p["rwkv_mu"][l], p["rwkv_w0"][l].reshape(1, -1), _block_diag2(p["rwkv_w2"][l]).astype(BF16),
        p["rwkv_a0"][l].reshape(1, -1), _block_diag2(p["rwkv_a2"][l]).astype(BF16),
        p["rwkv_g2"][l].astype(BF16), p["rwkv_kk"][l].reshape(1, -1), p["rwkv_ka"][l].reshape(1, -1),
        p["rwkv_rk"][l].reshape(1, -1))))
    rf, rb = _rwkv_scan(pre)
    return z, og, ssd_f, ssd_b, gla_f, gla_b, rf, rb, pre["bonus"], pre["gate"]


def kernel(x, c, ctx, c_ctx, ada_w, ada_b, w_in, ssd_conv_w, ssd_conv_b, ssd_dt_bias, ssd_a_log, ssd_d, ssd_norm_w, rwkv_mu, rwkv_w0, rwkv_w2, rwkv_a0, rwkv_a2, rwkv_g2, rwkv_kk, rwkv_ka, rwkv_rk, rwkv_lnx_w, rwkv_lnx_b, gla_gu, gla_gb, gla_norm_w, w_out, ln1_w, ln1_b, ln2_w, ln2_b, router_w, router_b, exp_w1, exp_w3, exp_w2, sh_w1, sh_w3, sh_w2):
    p = dict(ssd_conv_w=ssd_conv_w, ssd_conv_b=ssd_conv_b, ssd_dt_bias=ssd_dt_bias, ssd_a_log=ssd_a_log,
             ssd_d=ssd_d, rwkv_mu=rwkv_mu, rwkv_w0=rwkv_w0, rwkv_w2=rwkv_w2, rwkv_a0=rwkv_a0, rwkv_a2=rwkv_a2,
             rwkv_g2=rwkv_g2, rwkv_kk=rwkv_kk, rwkv_ka=rwkv_ka, rwkv_rk=rwkv_rk, gla_gu=gla_gu, gla_gb=gla_gb)
    bsz, seq, d = x.shape
    n_ctx = ctx.shape[1]
    assert n_ctx == N_CTX == TOK_BLOCK and d == D_MODEL and seq % TOK_BLOCK == 0 and seq // GRID_W >= 8
    depth = ada_w.shape[0]
    row = lambda a: a.reshape(1, -1)

    cond = jnp.concatenate([c, c_ctx[None], jnp.zeros((8 - bsz - 1, d), F32)], axis=0)
    mods = _ada_all(cond, ada_w, ada_b).reshape(depth, 8, N_MOD, d)
    t = n_ctx + seq
    msels = [jnp.stack([jnp.broadcast_to(mods[l, bsz], (bsz, N_MOD, d)), mods[l, :bsz]], axis=1)
             for l in range(depth)]
    xs, proj = _inproj(ctx, x, msels[0], _prep_w_in(w_in[0]))
    w1_all = exp_w1.astype(BF16)
    w3_all = exp_w3.astype(BF16)
    w2_all = exp_w2.astype(BF16).reshape(depth, N_EXPERTS // EXPERTS_PER_STEP, -1, d)
    for l in range(depth):
        msel = msels[l]
        z, og, ssd_f, ssd_b, gla_f, gla_b, rf, rb, bonus, gate = _token_mixer(proj, l, p)
        x1, h, gates = _outproj(xs, msel, ssd_f, ssd_b, z, rf, rb, bonus, gate, gla_f, gla_b, og,
                                row(ssd_norm_w[l]), row(rwkv_lnx_w[l]), row(rwkv_lnx_b[l]), row(gla_norm_w[l]),
                                w_out[l].astype(BF16), row(ln1_w[l]), row(ln1_b[l]),
                                router_w[l].T, jnp.broadcast_to(router_b[l][:, None], (N_EXPERTS, 128)))
        last = l == depth - 1
        h_moe, g_moe = (h[:, n_ctx:], gates[:, n_ctx:]) if last else (h, gates)
        routed = _moe(h_moe.reshape(-1, d), g_moe.reshape(-1, 128), w1_all, w3_all, w2_all, l)
        s13 = jnp.concatenate([sh_w1[l], sh_w3[l]], axis=1).astype(BF16)
        args = (x1, h, routed.reshape(bsz, -1, d), msel, s13, sh_w2[l].astype(BF16), row(ln2_w[l]), row(ln2_b[l]))
        if last:
            return _final(*args, None)
        xs, proj = _final(*args, (msels[l + 1], _prep_w_in(w_in[l + 1])))
```

```python
import functools

import jax
import jax.numpy as jnp
import numpy as np
from jax import lax
from jax.experimental import pallas as pl
from jax.experimental.pallas import tpu as pltpu

F32 = jnp.float32
BF16 = jnp.bfloat16

D_MODEL = 1024
N_CTX = 256
GRID_W = 64
N_MOD = 6
LN_EPS = 1e-5
DEPTH = 4
DN_ALPHA = (2 * DEPTH) ** 0.25

SSD_HEADS = 8
SSD_HEAD_DIM = 64
SSD_INNER = 512
SSD_STATE = 128
SSD_XBC = 1024

RWKV_WIDTH = 256
RWKV_HEAD_DIM = 64
RWKV_COLS = 1152
RWKV_LNX_EPS = 64e-5

GLA_K_WIDTH = 128
GLA_V_WIDTH = 256
GLA_KEY_DIM = 32
GLA_VAL_DIM = 64
GLA_GATE_LORA = 16
GLA_GATE_NORMALIZER = 16.0
GLA_SUB = 64
GLA_BLK = 16

N_EXPERTS = 64
TOP_K = 8
N_GROUPS = 8
TOPK_GROUPS = 4
EXPERT_DIM = 256
ROUTED_SCALE = 2.5
EXPERTS_PER_STEP = 4

TOK_BLOCK = 256
SCAN_BLOCK = 64
V7X_VMEM_LIMIT = 56 * 1024 * 1024

_IN_PIECES = (("z", 512), ("xbc", 1024), ("rw", 1152), ("gq", 128), ("gk", 128), ("gv", 256), ("og", 256),
              ("small", 128))
IN_PAD = sum(w for _, w in _IN_PIECES)


def _cparams(sem):
    return pltpu.CompilerParams(dimension_semantics=sem, vmem_limit_bytes=V7X_VMEM_LIMIT)


def _split3(a):
    hi = a.astype(BF16)
    r1 = a - hi.astype(F32)
    mid = r1.astype(BF16)
    lo = (r1 - mid.astype(F32)).astype(BF16)
    return hi, mid, lo


def _dot(a, b):
    return jnp.dot(a, b, preferred_element_type=F32)


def _dot_nt(a, b):
    return lax.dot_general(a, b, (((1,), (1,)), ((), ())), preferred_element_type=F32)


def _dot_x_exact(a, e):
    hi, mid, lo = _split3(a)
    return _dot(hi, e) + (_dot(mid, e) + _dot(lo, e))


def _dot_exact_x(e, a):
    hi, mid, lo = _split3(a)
    return _dot(e, hi) + (_dot(e, mid) + _dot(e, lo))


def _dot_hp(a, b):
    ah, am, _ = _split3(a)
    bh, bm, _ = _split3(b)
    return _dot(ah, bh) + (_dot(ah, bm) + _dot(am, bh))


def _sigmoid(x):
    return 1.0 / (1.0 + jnp.exp(-x))


def _silu(x):
    return x * _sigmoid(x)


def _softplus(x):
    return jnp.maximum(x, 0.0) + jnp.log(1.0 + jnp.exp(-jnp.abs(x)))


def _layernorm(x):
    mu = jnp.mean(x, axis=-1, keepdims=True)
    xc = x - mu
    var = jnp.mean(xc * xc, axis=-1, keepdims=True)
    return xc * lax.rsqrt(var + LN_EPS)


def _iota(shape, dim):
    return lax.broadcasted_iota(jnp.int32, shape, dim)


def _tri(n, upper):
    r = _iota((n, n), 0)
    c = _iota((n, n), 1)
    return jnp.where(upper, c - r, r - c) >= 0


def _group_ones(n, g):
    r = _iota((n, n), 0) // g
    c = _iota((n, n), 1) // g
    return (r == c).astype(BF16)


def _run_interleaved(gens):
    results = [None] * len(gens)
    active = list(range(len(gens)))
    while active:
        for i in list(active):
            try:
                next(gens[i])
            except StopIteration as stop:
                results[i] = stop.value
                active.remove(i)
    return results


def _ada_body(c_ref, w_ref, b_ref, o_ref):
    a = _silu(c_ref[...]).astype(BF16)
    o_ref[0] = _dot(a, w_ref[0].astype(BF16)) + b_ref[0]


def _ada_all(cond, ada_w, ada_b):
    nl, d, n = ada_w.shape
    tn = 1536
    rows = cond.shape[0]
    return pl.pallas_call(
        _ada_body,
        grid=(nl, n // tn),
        in_specs=[pl.BlockSpec((rows, d), lambda l, k: (0, 0)),
                  pl.BlockSpec((1, d, tn), lambda l, k: (l, 0, k)),
                  pl.BlockSpec((1, 1, tn), lambda l, k: (l, 0, k))],
        out_specs=pl.BlockSpec((1, rows, tn), lambda l, k: (l, 0, k)),
        out_shape=jax.ShapeDtypeStruct((nl, rows, n), F32),
        compiler_params=_cparams(("arbitrary", "arbitrary")),
        name="ada_mod",
    )(cond, ada_w, ada_b.reshape(nl, 1, n))


def _inproj_store(x, m, w_ref, out_refs):
    u = _layernorm(x) * (1.0 + m[1:2]) + m[0:1]
    y = _dot(u.astype(BF16), w_ref[...])
    off = 0
    for (_, wdt), o in zip(_IN_PIECES, out_refs):
        o[0] = y[:, off:off + wdt]
        off += wdt


def _inproj_body(ctx_ref, x_ref, m_ref, w_ref, xs_ref, *out_refs):
    x = jnp.where(pl.program_id(1) == 0, ctx_ref[0], x_ref[0])
    xs_ref[0] = x
    _inproj_store(x, m_ref[0, 0], w_ref, out_refs)


def _inproj(ctx, x, msel, w_in_p):
    b, seq, d = x.shape
    t = seq + N_CTX
    nb = t // TOK_BLOCK
    tok = lambda w: pl.BlockSpec((1, TOK_BLOCK, w), lambda i, j: (i, j, 0))
    outs = pl.pallas_call(
        _inproj_body,
        grid=(b, nb),
        in_specs=[pl.BlockSpec((1, N_CTX, d), lambda i, j: (i, 0, 0)),
                  pl.BlockSpec((1, TOK_BLOCK, d), lambda i, j: (i, jnp.maximum(j - N_CTX // TOK_BLOCK, 0), 0)),
                  pl.BlockSpec((1, 1, N_MOD, d), lambda i, j: (i, jnp.minimum(j, 1), 0, 0)),
                  pl.BlockSpec((d, IN_PAD), lambda i, j: (0, 0))],
        out_specs=[tok(d)] + [tok(w) for _, w in _IN_PIECES],
        out_shape=[jax.ShapeDtypeStruct((b, t, d), F32)] + [jax.ShapeDtypeStruct((b, t, w), F32)
                                                            for _, w in _IN_PIECES],
        compiler_params=_cparams(("arbitrary", "arbitrary")),
        name="in_proj",
    )(ctx, x, msel, w_in_p)
    return outs[0], outs[1:]


def _ssd_body(nbatch, nb, fwd_refs, bwd_refs, cw_ref, cb_ref, dtb_ref, alog_ref, dsk_ref, yf_ref, yb_ref, h_scr):
    q = TOK_BLOCK
    j = pl.program_id(0)

    @pl.when(j == 0)
    def _():
        h_scr[...] = jnp.zeros_like(h_scr)

    lane = _iota((1, 128), 1)
    neg_a = jnp.where(lane < 2 * SSD_HEADS, -jnp.exp(alog_ref[...]), 0.0)
    cw = cw_ref[0]
    er = _iota((128, SSD_INNER), 0)
    ec = _iota((128, SSD_INNER), 1)
    er8 = _iota((128, 128), 0)
    ec8 = _iota((128, 128), 1)
    lane_q = _iota((q, 128), 1)
    hpg = SSD_HEADS // 2

    def stream(refs, bi, upper, si, blk):
        xbc_ref, hp_ref, hn_ref, small_ref = refs
        d = 1 if upper else 0
        seg_first = (blk == 0) | (blk == 1)
        seg_last = (blk == 0) | (blk == nb - 1)
        cur = xbc_ref[bi]
        row_id = _iota((q, 1), 0)
        before = jnp.where(seg_first, 0.0, hp_ref[bi][7:8])
        after = jnp.where(seg_last, 0.0, hn_ref[bi][0:1])
        prev = jnp.where(row_id == 0, before, pltpu.roll(cur, 1, axis=0))
        nxt = jnp.where(row_id == q - 1, after, pltpu.roll(cur, q - 1, axis=0))
        act = _silu(prev * cw[0:1] + cur * cw[1:2] + nxt * cw[2:3] + cb_ref[...])
        xs = act[:, :SSD_INNER]
        b_grp = [act[:, 512 + 128 * g:640 + 128 * g] for g in range(2)]
        c_grp = [act[:, 768 + 128 * g:896 + 128 * g].astype(BF16) for g in range(2)]

        tri_b = _tri(q, upper)
        tri = tri_b.astype(BF16)
        tri_t = _tri(q, not upper).astype(BF16)
        dt_all = _softplus(small_ref[bi] + dtb_ref[...])
        la_all = dt_all * neg_a
        e_exp = (er == SSD_HEADS * d + ec // SSD_HEAD_DIM).astype(BF16)
        e8 = ((er8 == SSD_HEADS * d + ec8) & (ec8 < SSD_HEADS)).astype(BF16)
        e_head = (er == ec // SSD_HEAD_DIM).astype(BF16)
        dt_exp = _dot_x_exact(dt_all, e_exp)
        la8 = _dot_x_exact(la_all, e8)
        cs8 = _dot_exact_x(tri, la8)
        cs_t = _dot_x_exact(la8.T, tri_t)
        cs_exp = _dot_x_exact(cs8, e_head)
        tot_exp = cs_exp[0:1] if upper else cs_exp[q - 1:q]

        xdt = xs * dt_exp
        gmats = [_dot_nt(c_grp[g], b_grp[g].astype(BF16)) for g in range(2)]
        ys = []
        for p in range(SSD_HEADS // 2):
            xdt_p = xdt[:, 128 * p:128 * (p + 1)].astype(BF16)
            halves = []
            for h in (2 * p, 2 * p + 1):
                lmat = jnp.where(tri_b, jnp.exp(cs8[:, h:h + 1] - cs_t[h:h + 1, :]), 0.0)
                halves.append(_dot((gmats[h // hpg] * lmat).astype(BF16), xdt_p))
            ys.append(jnp.where(lane_q < SSD_HEAD_DIM, halves[0], halves[1]))
        y = jnp.concatenate(ys, axis=1)

        xd = (xdt * jnp.exp(tot_exp - cs_exp)).astype(BF16)
        offs = []
        for g in range(2):
            hg = h_scr[si, g]
            offs.append(_dot(c_grp[g], hg.astype(BF16)))
            hn = _dot(b_grp[g].T.astype(BF16), xd[:, 256 * g:256 * (g + 1)])
            h_scr[si, g] = hg * jnp.exp(tot_exp[:, 256 * g:256 * (g + 1)]) + hn
        y = y + jnp.concatenate(offs, axis=1) * jnp.exp(cs_exp)
        if not upper:
            y = y + dsk_ref[...] * xs
        return y

    blk_b = jnp.where(j == 0, 0, nb - j)
    for bi in range(nbatch):
        yf_ref[bi] = stream(fwd_refs, bi, False, bi, j)
        yb_ref[bi] = stream(bwd_refs, bi, True, nbatch + bi, blk_b)


def _ssd(xbc, small, conv_w, conv_b, dtb_pad, alog_pad, dskip_exp):
    b, t, _ = xbc.shape
    nb = t // TOK_BLOCK
    r8 = TOK_BLOCK // 8

    def bwd_blk(j):
        return jnp.where(j == 0, 0, nb - j)

    def specs(blk):
        return [pl.BlockSpec((b, TOK_BLOCK, SSD_XBC), lambda j: (0, blk(j), 0)),
                pl.BlockSpec((b, 8, SSD_XBC), lambda j: (0, jnp.maximum(blk(j) * r8 - 1, 0), 0)),
                pl.BlockSpec((b, 8, SSD_XBC), lambda j: (0, jnp.minimum(blk(j) * r8 + r8, t // 8 - 1), 0)),
                pl.BlockSpec((b, TOK_BLOCK, 128), lambda j: (0, blk(j), 0))]

    def body(*refs):
        _ssd_body(b, nb, refs[0:4], refs[4:8], *refs[8:])

    const2 = lambda j: (0, 0)
    oshape = jax.ShapeDtypeStruct((b, t, SSD_INNER), F32)
    return pl.pallas_call(
        body,
        grid=(nb,),
        in_specs=specs(lambda j: j) + specs(bwd_blk) + [
            pl.BlockSpec((1, 3, SSD_XBC), lambda j: (0, 0, 0)),
            pl.BlockSpec((1, SSD_XBC), const2),
            pl.BlockSpec((1, 128), const2),
            pl.BlockSpec((1, 128), const2),
            pl.BlockSpec((1, SSD_INNER), const2)],
        out_specs=[pl.BlockSpec((b, TOK_BLOCK, SSD_INNER), lambda j: (0, j, 0)),
                   pl.BlockSpec((b, TOK_BLOCK, SSD_INNER), lambda j: (0, bwd_blk(j), 0))],
        out_shape=[oshape, oshape],
        scratch_shapes=[pltpu.VMEM((2 * b, 2, SSD_STATE, 256), F32)],
        compiler_params=_cparams(("arbitrary",)),
        name="ssd",
    )(xbc, xbc, xbc, small, xbc, xbc, xbc, small, conv_w[None], conv_b, dtb_pad, alog_pad, dskip_exp)


def _gla_body(nbatch, fwd_refs, bwd_refs, gu_ref, gb_ref, of_ref, ob_ref, s_scr):
    j = pl.program_id(0)
    c = GLA_SUB
    nsub = TOK_BLOCK // c

    @pl.when(j == 0)
    def _():
        s_scr[...] = jnp.zeros_like(s_scr)

    lr = _iota((128, 128), 0)
    lc = _iota((128, 128), 1)
    lane_k = _iota((c, GLA_K_WIDTH), 1) // GLA_KEY_DIM
    lane_kx = _iota((GLA_BLK * (c // GLA_BLK) * (c // GLA_BLK - 1) // 2, GLA_K_WIDTH), 1) // GLA_KEY_DIM
    lane_v = _iota((c, GLA_V_WIDTH), 1) // GLA_VAL_DIM
    st_mask = (_iota((GLA_V_WIDTH, GLA_K_WIDTH), 0) // GLA_VAL_DIM
               == _iota((GLA_V_WIDTH, GLA_K_WIDTH), 1) // GLA_KEY_DIM)

    nblk = c // GLA_BLK
    tok_r = _iota((c, c), 0)
    tok_c = _iota((c, c), 1)
    col_blk = _iota((GLA_BLK, c), 1) // GLA_BLK

    def direction_consts(upper):
        first = (tok_r // GLA_BLK) * GLA_BLK + (GLA_BLK - 1 if upper else 0)
        last = (tok_r // GLA_BLK) * GLA_BLK + (0 if upper else GLA_BLK - 1)
        pairs = [(ti, tj) for ti in range(nblk) for tj in range(nblk) if (tj > ti if upper else tj < ti)]
        return dict(tri_b=_tri(c, upper), sel_first=(tok_c == first).astype(BF16),
                    sel_last=(tok_c == last).astype(BF16), pairs=pairs)

    consts = {False: direction_consts(False), True: direction_consts(True)}

    def stream(refs, bi, upper, st):
        q_ref, k_ref, v_ref, small_ref = refs
        d = 1 if upper else 0
        cst = consts[upper]
        tri_b = cst["tri_b"]
        tri = tri_b.astype(BF16)
        gsel = ((lr == 2 * SSD_HEADS + GLA_GATE_LORA * d + lc) & (lc < GLA_GATE_LORA)).astype(BF16)
        gd = _dot_x_exact(small_ref[bi], gsel)
        yield
        pre = _dot_hp(gd, gu_ref[d]) + gb_ref[d:d + 1]
        la_all = -_softplus(-pre) * (1.0 / GLA_GATE_NORMALIZER)
        yield
        outs = [None] * nsub
        for si in range(nsub):
            sub = nsub - 1 - si if upper else si
            lo = sub * c
            la = la_all[lo:lo + c]
            qq = q_ref[bi, lo:lo + c, :] * (GLA_KEY_DIM ** -0.5)
            kk = k_ref[bi, lo:lo + c, :]
            vv = v_ref[bi, lo:lo + c, :]
            cs = _dot_exact_x(tri, la)
            yield
            rs = _dot_exact_x(cst["sel_first"], cs - la)
            re = _dot_exact_x(cst["sel_last"], cs)
            yield
            tot = jnp.sum(la, axis=0, keepdims=True)
            qt = qq * jnp.exp(cs)
            kdec = (kk * jnp.exp(tot - cs)).astype(BF16)
            q_in = qq * jnp.exp(cs - rs)
            k_in = (kk * jnp.exp(jnp.minimum(rs - cs, 80.0))).astype(BF16)
            k_out = (kk * jnp.exp(re - cs)).astype(BF16)
            q_cross = jnp.concatenate(
                [q_in[GLA_BLK * ti:GLA_BLK * (ti + 1)]
                 * jnp.exp(rs[GLA_BLK * ti:GLA_BLK * ti + 1] - re[GLA_BLK * tj:GLA_BLK * tj + 1])
                 for ti, tj in cst["pairs"]], axis=0)
            vb = vv.astype(BF16)
            o = _dot_nt(qt.astype(BF16), st.astype(BF16))
            amats = []
            for h in range(GLA_K_WIDTH // GLA_KEY_DIM):
                r_in = _dot_nt(jnp.where(lane_k == h, q_in, 0.0).astype(BF16), k_in)
                r_cross = _dot_nt(jnp.where(lane_kx == h, q_cross, 0.0).astype(BF16), k_out)
                rows = []
                for ti in range(nblk):
                    blk_rows = slice(GLA_BLK * ti, GLA_BLK * (ti + 1))
                    part = jnp.where(tri_b[blk_rows] & (col_blk == ti), r_in[blk_rows], 0.0)
                    for pi, (pti, ptj) in enumerate(cst["pairs"]):
                        if pti == ti:
                            part = part + jnp.where(col_blk == ptj, r_cross[GLA_BLK * pi:GLA_BLK * (pi + 1)], 0.0)
                    rows.append(part)
                amats.append(jnp.concatenate(rows, axis=0).astype(BF16))
            upd = _dot(vv.T.astype(BF16), kdec)
            yield
            for h, a in enumerate(amats):
                o = o + jnp.where(lane_v == h, _dot(a, vb), 0.0)
            st = st * jnp.exp(tot) + jnp.where(st_mask, upd, 0.0)
            outs[sub] = o
            yield
        return jnp.concatenate(outs, axis=0), st

    gens = ([stream(fwd_refs, bi, False, s_scr[bi]) for bi in range(nbatch)]
            + [stream(bwd_refs, bi, True, s_scr[nbatch + bi]) for bi in range(nbatch)])
    for si, (o, st) in enumerate(_run_interleaved(gens)):
        if si < nbatch:
            of_ref[si] = o
        else:
            ob_ref[si - nbatch] = o
        s_scr[si] = st


def _gla(gq, gk, gv, small, gu_pad, gb):
    b, t, _ = gq.shape
    nb = t // TOK_BLOCK

    def bwd_blk(j):
        return jnp.where(j == 0, 0, nb - j)

    def specs(blk):
        return [pl.BlockSpec((b, TOK_BLOCK, GLA_K_WIDTH), lambda j: (0, blk(j), 0)),
                pl.BlockSpec((b, TOK_BLOCK, GLA_K_WIDTH), lambda j: (0, blk(j), 0)),
                pl.BlockSpec((b, TOK_BLOCK, GLA_V_WIDTH), lambda j: (0, blk(j), 0)),
                pl.BlockSpec((b, TOK_BLOCK, 128), lambda j: (0, blk(j), 0))]

    def body(*refs):
        _gla_body(b, refs[0:4], refs[4:8], *refs[8:])

    oshape = jax.ShapeDtypeStruct((b, t, GLA_V_WIDTH), F32)
    return pl.pallas_call(
        body,
        grid=(nb,),
        in_specs=specs(lambda j: j) + specs(bwd_blk) + [
            pl.BlockSpec((2, 128, GLA_K_WIDTH), lambda j: (0, 0, 0)),
            pl.BlockSpec((2, GLA_K_WIDTH), lambda j: (0, 0))],
        out_specs=[pl.BlockSpec((b, TOK_BLOCK, GLA_V_WIDTH), lambda j: (0, j, 0)),
                   pl.BlockSpec((b, TOK_BLOCK, GLA_V_WIDTH), lambda j: (0, bwd_blk(j), 0))],
        out_shape=[oshape, oshape],
        scratch_shapes=[pltpu.VMEM((2 * b, GLA_V_WIDTH, GLA_K_WIDTH), F32)],
        compiler_params=_cparams(("arbitrary",)),
        name="gla",
    )(gq, gk, gv, small, gq, gk, gv, small, gu_pad, gb)


_RWKV_OUTS = ("r", "v", "kk", "lw_f", "k_f", "kka_f", "lw_b", "k_b", "kka_b", "bonus", "gate")


def _rwkv_prep_body(nb, rw_ref, hp_ref, hn_ref, mu_ref, w0_ref, w2_ref, a0_ref, a2_ref, g2_ref, kkp_ref, ka_ref,
                    rk_ref, *refs):
    outs = dict(zip(_RWKV_OUTS, refs))
    q = TOK_BLOCK
    j = pl.program_id(1)
    is_ctx = j == 0
    z = rw_ref[0]
    t = _iota((q, 1), 0)
    col = t % GRID_W
    prev = jnp.where(jnp.where(is_ctx, t, col) == 0, 0.0, pltpu.roll(z, 1, axis=0))
    nxt = jnp.where(jnp.where(is_ctx, t - (q - 1), col - (GRID_W - 1)) == 0, 0.0, pltpu.roll(z, q - 1, axis=0))
    up = jnp.concatenate([jnp.where(j == 1, 0.0, hp_ref[0]), z[:q - GRID_W]], axis=0)
    down = jnp.concatenate([z[GRID_W:], jnp.where(j == nb - 1, 0.0, hn_ref[0])], axis=0)
    mu = mu_ref[...]
    vert = mu[2:3] * (up - z) + mu[3:4] * (down - z)
    mixed = z + mu[0:1] * (prev - z) + mu[1:2] * (nxt - z) + jnp.where(is_ctx, 0.0, vert)

    w = RWKV_WIDTH
    r = mixed[:, 0:w]
    k = mixed[:, w:2 * w]
    v = mixed[:, 2 * w:3 * w]
    wd = mixed[:, 3 * w:3 * w + 128]
    ad = mixed[:, 3 * w + 128:4 * w]
    gdr = mixed[:, 4 * w:4 * w + 128]

    lw = w0_ref[...] + _dot(jnp.tanh(wd).astype(BF16), w2_ref[...])
    log_dec = -jnp.exp(-_softplus(-lw) - 0.5)
    a = _sigmoid(a0_ref[...] + _dot(ad.astype(BF16), a2_ref[...]))
    ones64 = _group_ones(w, RWKV_HEAD_DIM)
    kkv = k * kkp_ref[...]
    nrm = jnp.maximum(jnp.sqrt(_dot_x_exact(kkv * kkv, ones64)), 1e-12)
    kkn = kkv / nrm
    ka = ka_ref[...]
    a_f, a_b = a[:, :w], a[:, w:]
    k_f = k * (1.0 + (a_f - 1.0) * ka)
    k_b = k * (1.0 + (a_b - 1.0) * ka)
    outs["r"][0] = r
    outs["v"][0] = v
    outs["kk"][0] = kkn
    outs["lw_f"][0] = log_dec[:, :w]
    outs["lw_b"][0] = log_dec[:, w:]
    outs["k_f"][0] = k_f
    outs["k_b"][0] = k_b
    outs["kka_f"][0] = kkn * a_f
    outs["kka_b"][0] = kkn * a_b
    outs["bonus"][0] = _dot_x_exact(r * (k_f + k_b) * rk_ref[...], ones64) * v
    outs["gate"][0] = _dot(_sigmoid(gdr).astype(BF16), g2_ref[...])


def _rwkv_prep(rw, mu, w0cat, w2bd, a0cat, a2bd, g2, kkp, ka, rk):
    b, t, _ = rw.shape
    nb = t // TOK_BLOCK
    hb = TOK_BLOCK // GRID_W
    w = RWKV_WIDTH
    c2 = lambda i, j: (0, 0)
    return pl.pallas_call(
        functools.partial(_rwkv_prep_body, nb),
        grid=(b, nb),
        in_specs=[pl.BlockSpec((1, TOK_BLOCK, RWKV_COLS), lambda i, j: (i, j, 0)),
                  pl.BlockSpec((1, GRID_W, RWKV_COLS), lambda i, j: (i, jnp.maximum(j * hb - 1, 0), 0)),
                  pl.BlockSpec((1, GRID_W, RWKV_COLS),
                               lambda i, j: (i, jnp.minimum(j * hb + hb, t // GRID_W - 1), 0)),
                  pl.BlockSpec((4, RWKV_COLS), c2),
                  pl.BlockSpec((1, 2 * w), c2),
                  pl.BlockSpec((128, 2 * w), c2),
                  pl.BlockSpec((1, 2 * w), c2),
                  pl.BlockSpec((128, 2 * w), c2),
                  pl.BlockSpec((128, w), c2),
                  pl.BlockSpec((1, w), c2),
                  pl.BlockSpec((1, w), c2),
                  pl.BlockSpec((1, w), c2)],
        out_specs=[pl.BlockSpec((1, TOK_BLOCK, w), lambda i, j: (i, j, 0)) for _ in _RWKV_OUTS],
        out_shape=[jax.ShapeDtypeStruct((b, t, w), F32) for _ in _RWKV_OUTS],
        compiler_params=_cparams(("arbitrary", "arbitrary")),
        name="rwkv_prep",
    )(rw, rw, rw, mu, w0cat, w2bd, a0cat, a2bd, g2, kkp, ka, rk)


def _rwkv_scan_body(nbatch, fwd_refs, bwd_refs, yf_ref, yb_ref, s_scr):
    j = pl.program_id(0)
    n = SCAN_BLOCK
    hd = RWKV_HEAD_DIM

    @pl.when(j == 0)
    def _():
        s_scr[...] = jnp.zeros_like(s_scr)

    nh = RWKV_WIDTH // hd
    w = RWKV_WIDTH
    assert n == hd
    lane_head = _iota((1, w), 1) // hd
    cat_c = _iota((n, nh * n), 0)
    cat_cc = _iota((n, nh * n), 1) % n
    eye_cat = (cat_c == cat_cc).astype(F32)
    bd_mask = (_iota((w, w), 0) // hd) == (_iota((w, w), 1) // hd)

    def stack(x):
        return jnp.concatenate([jnp.where(lane_head == h, x, 0.0) for h in range(nh)], axis=0)

    def chunk(refs, bi, upper, s_prev):
        r, v, kk, lw, k, kka = (ref[bi] for ref in refs)
        tri = _tri(n, upper).astype(BF16)
        strict = (cat_cc > cat_c) if upper else (cat_cc < cat_c)
        incl = (cat_cc >= cat_c) if upper else (cat_cc <= cat_c)
        cs = _dot_exact_x(tri, lw)
        yield
        tot = jnp.sum(lw, axis=0, keepdims=True)
        g_inv = jnp.exp(-cs)
        dec_end = jnp.exp(tot - cs)
        a_t = (-kk * jnp.exp(cs - lw)).astype(BF16)
        r_t = (r * jnp.exp(cs)).astype(BF16)
        bk_s = jnp.concatenate([stack(kka * g_inv), stack(k * g_inv)], axis=0).astype(BF16)
        bk_dec = jnp.concatenate([kka * dec_end, k * dec_end], axis=0).astype(BF16)
        v_s = stack(v).astype(BF16)
        x = _dot_nt(jnp.concatenate([a_t, r_t], axis=0), bk_s)
        yield
        nmat = jnp.where(strict, x[:n, :nh * n], 0.0)
        ak = jnp.where(strict, x[:n, nh * n:], 0.0).astype(BF16)
        rb = jnp.where(incl, x[n:, :nh * n], 0.0).astype(BF16)
        rk = jnp.where(incl, x[n:, nh * n:], 0.0).astype(BF16)
        s_b = s_prev.astype(BF16)
        w0 = _dot_nt(a_t, s_b) + _dot(ak, v_s)
        y0 = _dot_nt(r_t, s_b) + _dot(rk, v_s)
        tinv = eye_cat + nmat
        p = nmat.astype(BF16)
        p_s = stack(nmat).astype(BF16)
        for i in range(n.bit_length() - 2):
            p2 = _dot(p, p_s)
            yield
            p_s = stack(p2).astype(BF16)
            tinv = tinv + _dot(tinv.astype(BF16), p_s)
            p = p2.astype(BF16)
            yield
        u = _dot(tinv.astype(BF16), stack(w0).astype(BF16))
        yield
        y = y0 + _dot(rb, stack(u).astype(BF16))
        uv = jnp.concatenate([u, v], axis=0).astype(BF16)
        upd = lax.dot_general(uv, bk_dec, (((0,), (0,)), ((), ())), preferred_element_type=F32)
        s_new = s_prev * jnp.exp(tot) + jnp.where(bd_mask, upd, 0.0)
        return y, s_new

    gens = ([chunk(fwd_refs, bi, False, s_scr[bi]) for bi in range(nbatch)]
            + [chunk(bwd_refs, bi, True, s_scr[nbatch + bi]) for bi in range(nbatch)])
    for si, (y, s_new) in enumerate(_run_interleaved(gens)):
        if si < nbatch:
            yf_ref[si] = y
        else:
            yb_ref[si - nbatch] = y
        s_scr[si] = s_new


def _rwkv_scan(pre):
    b, t, w = pre["r"].shape
    nbs = t // SCAN_BLOCK
    ctx_blocks = N_CTX // SCAN_BLOCK

    def fwd_blk(j):
        return j

    def bwd_blk(j):
        return jnp.where(j < ctx_blocks, ctx_blocks - 1 - j, nbs + ctx_blocks - 1 - j)

    in_f = pl.BlockSpec((b, SCAN_BLOCK, w), lambda j: (0, fwd_blk(j), 0))
    in_b = pl.BlockSpec((b, SCAN_BLOCK, w), lambda j: (0, bwd_blk(j), 0))
    oshape = jax.ShapeDtypeStruct((b, t, w), F32)

    def body(*refs):
        _rwkv_scan_body(b, refs[0:6], refs[6:12], refs[12], refs[13], refs[14])

    return pl.pallas_call(
        body,
        grid=(nbs,),
        in_specs=[in_f] * 6 + [in_b] * 6,
        out_specs=[in_f, in_b],
        out_shape=[oshape, oshape],
        scratch_shapes=[pltpu.VMEM((2 * b, w, w), F32)],
        compiler_params=_cparams(("arbitrary",)),
        name="rwkv_scan",
    )(pre["r"], pre["v"], pre["kk"], pre["lw_f"], pre["k_f"], pre["kka_f"],
      pre["r"], pre["v"], pre["kk"], pre["lw_b"], pre["k_b"], pre["kka_b"])


def _outproj_body(x_ref, m_ref, s0_ref, s1_ref, z_ref, rf_ref, rb_ref, bonus_ref, gate_ref, g0_ref, g1_ref, og_ref,
                  snw_ref, lxw_ref, lxb_ref, gnw_ref, wo_ref, l1w_ref, l1b_ref, rtw_ref, rtb_ref,
                  x1_ref, h_ref, g_ref):
    m = m_ref[0, 0]
    ys = (s0_ref[0] + s1_ref[0]) * _silu(z_ref[0])
    gw = SSD_INNER // 2
    parts = []
    for g in range(2):
        part = ys[:, gw * g:gw * (g + 1)]
        parts.append(part * lax.rsqrt(jnp.mean(part * part, axis=-1, keepdims=True) + 1e-5))
    y_ssd = jnp.concatenate(parts, axis=1) * snw_ref[...]
    ones64 = _group_ones(RWKV_WIDTH, RWKV_HEAD_DIM)
    inv = 1.0 / RWKV_HEAD_DIM
    yr = rf_ref[0] + rb_ref[0]
    mu = _dot_x_exact(yr, ones64) * inv
    yc = yr - mu
    var = _dot_x_exact(yc * yc, ones64) * inv
    yn = yc * lax.rsqrt(var + RWKV_LNX_EPS) * lxw_ref[...] + lxb_ref[...]
    y_rwkv = (yn + bonus_ref[0]) * gate_ref[0]
    o = g0_ref[0] + g1_ref[0]
    ms = _dot_x_exact(o * o, ones64) * (1.0 / GLA_VAL_DIM)
    y_gla = o * lax.rsqrt(ms + 1e-5) * gnw_ref[...] * _silu(og_ref[0])

    y_mix = jnp.concatenate([y_ssd, y_rwkv, y_gla], axis=1).astype(BF16)
    mix = _dot(y_mix, wo_ref[...])
    x1 = _layernorm(DN_ALPHA * x_ref[0] + m[2:3] * mix) * l1w_ref[...] + l1b_ref[...]
    x1_ref[0] = x1
    h = _layernorm(x1) * (1.0 + m[4:5]) + m[3:4]
    h_ref[0] = h.astype(BF16)
    g_ref[0] = _route(h, rtw_ref[...], rtb_ref[...])


def _outproj(xs, msel, ssd_f, ssd_b, z, rf, rb, bonus, gate, gla_f, gla_b, og, snw, lxw, lxb, gnw, wo, l1w, l1b, rtw, rtb):
    b, t, d = xs.shape
    nb = t // TOK_BLOCK
    tok = lambda w: pl.BlockSpec((1, TOK_BLOCK, w), lambda i, j: (i, j, 0))
    c2 = lambda i, j: (0, 0)
    row = lambda w: pl.BlockSpec((1, w), c2)
    return pl.pallas_call(
        _outproj_body,
        grid=(b, nb),
        in_specs=[tok(d),
                  pl.BlockSpec((1, 1, N_MOD, d), lambda i, j: (i, jnp.minimum(j, 1), 0, 0)),
                  tok(SSD_INNER), tok(SSD_INNER), tok(SSD_INNER),
                  tok(RWKV_WIDTH), tok(RWKV_WIDTH), tok(RWKV_WIDTH), tok(RWKV_WIDTH),
                  tok(GLA_V_WIDTH), tok(GLA_V_WIDTH), tok(GLA_V_WIDTH),
                  row(SSD_INNER), row(RWKV_WIDTH), row(RWKV_WIDTH), row(GLA_V_WIDTH),
                  pl.BlockSpec((d, d), c2), row(d), row(d),
                  pl.BlockSpec((N_EXPERTS, d), c2), pl.BlockSpec((N_EXPERTS, 128), c2)],
        out_specs=[tok(d), tok(d), tok(128)],
        out_shape=[jax.ShapeDtypeStruct((b, t, d), F32), jax.ShapeDtypeStruct((b, t, d), BF16),
                   jax.ShapeDtypeStruct((b, t, 128), F32)],
        compiler_params=_cparams(("arbitrary", "arbitrary")),
        name="out_proj_router",
    )(xs, msel, ssd_f, ssd_b, z, rf, rb, bonus, gate, gla_f, gla_b, og, snw, lxw, lxb, gnw, wo, l1w, l1b, rtw, rtb)


def _route(h, rw_t, rb_t):
    tm = h.shape[0]
    ne = N_EXPERTS
    per_group = ne // N_GROUPS
    ah, am, _ = _split3(rw_t)
    bh, bm, _ = _split3(h)
    logits = _dot_nt(ah, bh) + (_dot_nt(ah, bm) + _dot_nt(am, bh))
    scores = _sigmoid(logits)
    sel = scores + jnp.concatenate([rb_t] * (tm // 128), axis=1)
    neg = -jnp.inf
    sub8 = _iota((per_group, tm), 0).astype(F32)
    eidx = _iota((ne, tm), 0).astype(F32)

    def first_max(cur, idx, sentinel):
        mx = jnp.max(cur, axis=0, keepdims=True)
        first = jnp.min(jnp.where(cur == mx, idx, sentinel), axis=0, keepdims=True)
        return mx, idx == first

    gs = []
    for g in range(N_GROUPS):
        blk = sel[per_group * g:per_group * (g + 1)]
        m1, hit = first_max(blk, sub8, float(per_group))
        m2 = jnp.max(jnp.where(hit, neg, blk), axis=0, keepdims=True)
        gs.append(m1 + m2)
    cur = jnp.concatenate(gs, axis=0)
    allowed_g = jnp.zeros((N_GROUPS, tm), F32)
    for _ in range(TOPK_GROUPS):
        _, hit = first_max(cur, sub8, float(N_GROUPS))
        cur = jnp.where(hit, neg, cur)
        allowed_g = jnp.where(hit, 1.0, allowed_g)
    allowed = jnp.concatenate([jnp.broadcast_to(allowed_g[g:g + 1], (per_group, tm)) for g in range(N_GROUPS)],
                              axis=0)
    cur = jnp.where(allowed > 0.0, sel, neg)
    wts = jnp.zeros((ne, tm), F32)
    for _ in range(TOP_K):
        _, hit = first_max(cur, eidx, float(ne))
        cur = jnp.where(hit, neg, cur)
        wts = jnp.where(hit, scores, wts)
    gates_t = wts / jnp.sum(wts, axis=0, keepdims=True) * ROUTED_SCALE
    return jnp.concatenate([gates_t, jnp.zeros((128 - ne, tm), F32)], axis=0).T


def _moe_body(h_ref, g_ref, w1_ref, w3_ref, w2_ref, o_ref):
    e0 = pl.program_id(1) * EXPERTS_PER_STEP

    @pl.when(pl.program_id(1) == 0)
    def _():
        o_ref[...] = jnp.zeros_like(o_ref)

    hb = h_ref[...]
    gates = g_ref[...]
    lane = _iota(gates.shape, 1)
    hids = []
    for e in range(EXPERTS_PER_STEP):
        gcol = jnp.sum(jnp.where(lane == e0 + e, gates, 0.0), axis=-1, keepdims=True)
        hid = _silu(_dot(hb, w1_ref[e])) * _dot(hb, w3_ref[e])
        hids.append((hid * gcol).astype(BF16))
    o_ref[...] += _dot(jnp.concatenate(hids, axis=1), w2_ref[0])


def _moe(h2, gates, w1, w3, w2g, layer):
    n, d = h2.shape
    tm = n // 8 if n % (8 * 128) == 0 else 1024
    _, ne, _, f = w1.shape
    eps = EXPERTS_PER_STEP
    return pl.pallas_call(
        _moe_body,
        grid=(n // tm, ne // eps),
        in_specs=[pl.BlockSpec((tm, d), lambda i, g: (i, 0)),
                  pl.BlockSpec((tm, 128), lambda i, g: (i, 0)),
                  pl.BlockSpec((None, eps, d, f), lambda i, g: (layer, g, 0, 0)),
                  pl.BlockSpec((None, eps, d, f), lambda i, g: (layer, g, 0, 0)),
                  pl.BlockSpec((None, 1, eps * f, d), lambda i, g: (layer, g, 0, 0))],
        out_specs=pl.BlockSpec((tm, d), lambda i, g: (i, 0)),
        out_shape=jax.ShapeDtypeStruct((n, d), F32),
        compiler_params=_cparams(("arbitrary", "arbitrary")),
        name="moe_experts",
    )(h2, gates, w1, w3, w2g)


def _ffn_residual(x1_ref, h_ref, rt_ref, m, s13_ref, s2_ref, l2w_ref, l2b_ref):
    a = _dot(h_ref[0], s13_ref[...])
    fs = a.shape[1] // 2
    hid = (_silu(a[:, :fs]) * a[:, fs:]).astype(BF16)
    f = rt_ref[0] + _dot(hid, s2_ref[...])
    return _layernorm(DN_ALPHA * x1_ref[0] + m[5:6] * f) * l2w_ref[...] + l2b_ref[...]


def _final_body(x1_ref, h_ref, rt_ref, m_ref, s13_ref, s2_ref, l2w_ref, l2b_ref, o_ref):
    o_ref[0] = _ffn_residual(x1_ref, h_ref, rt_ref, m_ref[0, 0], s13_ref, s2_ref, l2w_ref, l2b_ref)


def _final_next_body(x1_ref, h_ref, rt_ref, m_ref, s13_ref, s2_ref, l2w_ref, l2b_ref, mn_ref, w_ref, o_ref,
                     *proj_refs):
    x2 = _ffn_residual(x1_ref, h_ref, rt_ref, m_ref[0, 0], s13_ref, s2_ref, l2w_ref, l2b_ref)
    o_ref[0] = x2
    _inproj_store(x2, mn_ref[0, 0], w_ref, proj_refs)


def _final(x1, h, routed, msel, s13, s2, l2w, l2b, nxt):
    b, t, d = x1.shape
    nb = t // TOK_BLOCK
    skip = N_CTX // TOK_BLOCK if nxt is None else 0
    tok = pl.BlockSpec((1, TOK_BLOCK, d), lambda i, j: (i, j + skip, 0))
    mspec = pl.BlockSpec((1, 1, N_MOD, d), lambda i, j: (i, jnp.minimum(j + skip, 1), 0, 0))
    c2 = lambda i, j: (0, 0)
    out_tok = lambda w: pl.BlockSpec((1, TOK_BLOCK, w), lambda i, j: (i, j, 0))
    in_specs = [tok, tok, out_tok(d) if nxt is None else tok, mspec, pl.BlockSpec(s13.shape, c2),
                pl.BlockSpec(s2.shape, c2), pl.BlockSpec((1, d), c2), pl.BlockSpec((1, d), c2)]
    if nxt is None:
        return pl.pallas_call(
            _final_body,
            grid=(b, nb - skip),
            in_specs=in_specs,
            out_specs=out_tok(d),
            out_shape=jax.ShapeDtypeStruct((b, t - skip * TOK_BLOCK, d), F32),
            compiler_params=_cparams(("arbitrary", "arbitrary")),
            name="shared_ffn_ln",
        )(x1, h, routed, msel, s13, s2, l2w, l2b)
    msel_n, w_in_n = nxt
    outs = pl.pallas_call(
        _final_next_body,
        grid=(b, nb),
        in_specs=in_specs + [mspec, pl.BlockSpec((d, IN_PAD), c2)],
        out_specs=[out_tok(d)] + [out_tok(w) for _, w in _IN_PIECES],
        out_shape=[jax.ShapeDtypeStruct((b, t, d), F32)] + [jax.ShapeDtypeStruct((b, t, w), F32)
                                                            for _, w in _IN_PIECES],
        compiler_params=_cparams(("arbitrary", "arbitrary")),
        name="shared_ffn_ln_in_proj",
    )(x1, h, routed, msel, s13, s2, l2w, l2b, msel_n, w_in_n)
    return outs[0], outs[1:]


def _prep_w_in(w):
    z, xbc, dt, rw, gq, gk, gv, gd, og = jnp.split(
        w, np.cumsum([512, 1024, 16, 1152, 128, 128, 256, 32, 256])[:-1].tolist(), axis=1)
    small = jnp.concatenate([dt, gd, jnp.zeros((w.shape[0], 128 - 48), w.dtype)], axis=1)
    return jnp.concatenate([z, xbc, rw, gq, gk, gv, og, small], axis=1).astype(BF16)


def _block_diag2(m):
    r, c = m.shape[1:]
    zero = jnp.zeros((r, c), m.dtype)
    return jnp.concatenate([jnp.concatenate([m[0], zero], axis=1),
                            jnp.concatenate([zero, m[1]], axis=1)], axis=0)


def _pad_lanes(v, n=128):
    v = v.reshape(1, -1)
    return jnp.pad(v, ((0, 0), (0, n - v.shape[1])))


def _token_mixer(proj, l, p):
    z, xbc, rw, gq, gk, gv, og, small = proj
    ssd_f, ssd_b = _ssd(xbc, small, p["ssd_conv_w"][l], p["ssd_conv_b"][l].reshape(1, -1),
                 _pad_lanes(p["ssd_dt_bias"][l]), _pad_lanes(p["ssd_a_log"][l]),
                 jnp.repeat(p["ssd_d"][l], SSD_HEAD_DIM).reshape(1, -1))
    gu_pad = jnp.pad(p["gla_gu"][l], ((0, 0), (0, 128 - GLA_GATE_LORA), (0, 0)))
    gla_f, gla_b = _gla(gq, gk, gv, small, gu_pad, p["gla_gb"][l])
    pre = dict(zip(_RWKV_OUTS, _rwkv_prep(
        rw, p["rwkv_mu"][l], p["rwkv_w0"][l].reshape(1, -1), _block_diag2(p["rwkv_w2"][l]).astype(BF16),
        p["rwkv_a0"][l].reshape(1, -1), _block_diag2(p["rwkv_a2"][l]).astype(BF16),
        p["rwkv_g2"][l].astype(BF16), p["rwkv_kk"][l].reshape(1, -1), p["rwkv_ka"][l].reshape(1, -1),
        p["rwkv_rk"][l].reshape(1, -1))))
    rf, rb = _rwkv_scan(pre)
    return z, og, ssd_f, ssd_b, gla_f, gla_b, rf, rb, pre["bonus"], pre["gate"]


def kernel(x, c, ctx, c_ctx, ada_w, ada_b, w_in, ssd_conv_w, ssd_conv_b, ssd_dt_bias, ssd_a_log, ssd_d, ssd_norm_w, rwkv_mu, rwkv_w0, rwkv_w2, rwkv_a0, rwkv_a2, rwkv_g2, rwkv_kk, rwkv_ka, rwkv_rk, rwkv_lnx_w, rwkv_lnx_b, gla_gu, gla_gb, gla_norm_w, w_out, ln1_w, ln1_b, ln2_w, ln2_b, router_w, router_b, exp_w1, exp_w3, exp_w2, sh_w1, sh_w3, sh_w2):
    p = dict(ssd_conv_w=ssd_conv_w, ssd_conv_b=ssd_conv_b, ssd_dt_bias=ssd_dt_bias, ssd_a_log=ssd_a_log,
             ssd_d=ssd_d, rwkv_mu=rwkv_mu, rwkv_w0=rwkv_w0, rwkv_w2=rwkv_w2, rwkv_a0=rwkv_a0, rwkv_a2=rwkv_a2,
             rwkv_g2=rwkv_g2, rwkv_kk=rwkv_kk, rwkv_ka=rwkv_ka, rwkv_rk=rwkv_rk, gla_gu=gla_gu, gla_gb=gla_gb)
    bsz, seq, d = x.shape
    n_ctx = ctx.shape[1]
    assert n_ctx == N_CTX == TOK_BLOCK and d == D_MODEL and seq % TOK_BLOCK == 0 and seq // GRID_W >= 8
    depth = ada_w.shape[0]
    row = lambda a: a.reshape(1, -1)

    cond = jnp.concatenate([c, c_ctx[None], jnp.zeros((8 - bsz - 1, d), F32)], axis=0)
    mods = _ada_all(cond, ada_w, ada_b).reshape(depth, 8, N_MOD, d)
    t = n_ctx + seq
    msels = [jnp.stack([jnp.broadcast_to(mods[l, bsz], (bsz, N_MOD, d)), mods[l, :bsz]], axis=1)
             for l in range(depth)]
    xs, proj = _inproj(ctx, x, msels[0], _prep_w_in(w_in[0]))
    w1_all = exp_w1.astype(BF16)
    w3_all = exp_w3.astype(BF16)
    w2_all = exp_w2.astype(BF16).reshape(depth, N_EXPERTS // EXPERTS_PER_STEP, -1, d)
    for l in range(depth):
        msel = msels[l]
        z, og, ssd_f, ssd_b, gla_f, gla_b, rf, rb, bonus, gate = _token_mixer(proj, l, p)
        x1, h, gates = _outproj(xs, msel, ssd_f, ssd_b, z, rf, rb, bonus, gate, gla_f, gla_b, og,
                                row(ssd_norm_w[l]), row(rwkv_lnx_w[l]), row(rwkv_lnx_b[l]), row(gla_norm_w[l]),
                                w_out[l].astype(BF16), row(ln1_w[l]), row(ln1_b[l]),
                                router_w[l].T, jnp.broadcast_to(router_b[l][:, None], (N_EXPERTS, 128)))
        last = l == depth - 1
        h_moe, g_moe = (h[:, n_ctx:], gates[:, n_ctx:]) if last else (h, gates)
        routed = _moe(h_moe.reshape(-1, d), g_moe.reshape(-1, 128), w1_all, w3_all, w2_all, l)
        s13 = jnp.concatenate([sh_w1[l], sh_w3[l]], axis=1).astype(BF16)
        args = (x1, h, routed.reshape(bsz, -1, d), msel, s13, sh_w2[l].astype(BF16), row(ln2_w[l]), row(ln2_b[l]))
        if last:
            return _final(*args, None)
        xs, proj = _final(*args, (msels[l + 1], _prep_w_in(w_in[l + 1])))
```

```python
import functools

import jax
import jax.numpy as jnp
import numpy as np
from jax import lax
from jax.experimental import pallas as pl
from jax.experimental.pallas import tpu as pltpu

F32 = jnp.float32
BF16 = jnp.bfloat16

D_MODEL = 1024
N_CTX = 256
GRID_W = 64
N_MOD = 6
LN_EPS = 1e-5
DEPTH = 4
DN_ALPHA = (2 * DEPTH) ** 0.25

SSD_HEADS = 8
SSD_HEAD_DIM = 64
SSD_INNER = 512
SSD_STATE = 128
SSD_XBC = 1024

RWKV_WIDTH = 256
RWKV_HEAD_DIM = 64
RWKV_COLS = 1152
RWKV_LNX_EPS = 64e-5

GLA_K_WIDTH = 128
GLA_V_WIDTH = 256
GLA_KEY_DIM = 32
GLA_VAL_DIM = 64
GLA_GATE_LORA = 16
GLA_GATE_NORMALIZER = 16.0
GLA_SUB = 64
GLA_BLK = 16

N_EXPERTS = 64
TOP_K = 8
N_GROUPS = 8
TOPK_GROUPS = 4
EXPERT_DIM = 256
ROUTED_SCALE = 2.5
EXPERTS_PER_STEP = 8

TOK_BLOCK = 256
SCAN_BLOCK = 64
V7X_VMEM_LIMIT = 56 * 1024 * 1024

_IN_PIECES = (("z", 512), ("xbc", 1024), ("rw", 1152), ("gq", 128), ("gk", 128), ("gv", 256), ("og", 256),
              ("small", 128))
IN_PAD = sum(w for _, w in _IN_PIECES)


def _cparams(sem):
    return pltpu.CompilerParams(dimension_semantics=sem, vmem_limit_bytes=V7X_VMEM_LIMIT)


def _split3(a):
    hi = a.astype(BF16)
    r1 = a - hi.astype(F32)
    mid = r1.astype(BF16)
    lo = (r1 - mid.astype(F32)).astype(BF16)
    return hi, mid, lo


def _dot(a, b):
    return jnp.dot(a, b, preferred_element_type=F32)


def _dot_nt(a, b):
    return lax.dot_general(a, b, (((1,), (1,)), ((), ())), preferred_element_type=F32)


def _dot_x_exact(a, e):
    hi, mid, lo = _split3(a)
    return _dot(hi, e) + (_dot(mid, e) + _dot(lo, e))


def _dot_exact_x(e, a):
    hi, mid, lo = _split3(a)
    return _dot(e, hi) + (_dot(e, mid) + _dot(e, lo))


def _dot_hp(a, b):
    ah, am, _ = _split3(a)
    bh, bm, _ = _split3(b)
    return _dot(ah, bh) + (_dot(ah, bm) + _dot(am, bh))


def _sigmoid(x):
    return 1.0 / (1.0 + jnp.exp(-x))


def _silu(x):
    return x * _sigmoid(x)


def _softplus(x):
    return jnp.maximum(x, 0.0) + jnp.log(1.0 + jnp.exp(-jnp.abs(x)))


def _layernorm(x):
    mu = jnp.mean(x, axis=-1, keepdims=True)
    xc = x - mu
    var = jnp.mean(xc * xc, axis=-1, keepdims=True)
    return xc * lax.rsqrt(var + LN_EPS)


def _iota(shape, dim):
    return lax.broadcasted_iota(jnp.int32, shape, dim)


def _tri(n, upper):
    r = _iota((n, n), 0)
    c = _iota((n, n), 1)
    return jnp.where(upper, c - r, r - c) >= 0


def _group_ones(n, g):
    r = _iota((n, n), 0) // g
    c = _iota((n, n), 1) // g
    return (r == c).astype(BF16)


def _run_interleaved(gens):
    results = [None] * len(gens)
    active = list(range(len(gens)))
    while active:
        for i in list(active):
            try:
                next(gens[i])
            except StopIteration as stop:
                results[i] = stop.value
                active.remove(i)
    return results


def _ada_body(c_ref, w_ref, b_ref, o_ref):
    a = _silu(c_ref[...]).astype(BF16)
    o_ref[0] = _dot(a, w_ref[0].astype(BF16)) + b_ref[0]


def _ada_all(cond, ada_w, ada_b):
    nl, d, n = ada_w.shape
    tn = 1536
    rows = cond.shape[0]
    return pl.pallas_call(
        _ada_body,
        grid=(nl, n // tn),
        in_specs=[pl.BlockSpec((rows, d), lambda l, k: (0, 0)),
                  pl.BlockSpec((1, d, tn), lambda l, k: (l, 0, k)),
                  pl.BlockSpec((1, 1, tn), lambda l, k: (l, 0, k))],
        out_specs=pl.BlockSpec((1, rows, tn), lambda l, k: (l, 0, k)),
        out_shape=jax.ShapeDtypeStruct((nl, rows, n), F32),
        compiler_params=_cparams(("arbitrary", "arbitrary")),
        name="ada_mod",
    )(cond, ada_w, ada_b.reshape(nl, 1, n))


def _inproj_store(x, m, w_ref, out_refs):
    u = _layernorm(x) * (1.0 + m[1:2]) + m[0:1]
    y = _dot(u.astype(BF16), w_ref[...])
    off = 0
    for (_, wdt), o in zip(_IN_PIECES, out_refs):
        o[0] = y[:, off:off + wdt]
        off += wdt


def _inproj_body(ctx_ref, x_ref, m_ref, w_ref, xs_ref, *out_refs):
    x = jnp.where(pl.program_id(1) == 0, ctx_ref[0], x_ref[0])
    xs_ref[0] = x
    _inproj_store(x, m_ref[0, 0], w_ref, out_refs)


def _inproj(ctx, x, msel, w_in_p):
    b, seq, d = x.shape
    t = seq + N_CTX
    nb = t // TOK_BLOCK
    tok = lambda w: pl.BlockSpec((1, TOK_BLOCK, w), lambda i, j: (i, j, 0))
    outs = pl.pallas_call(
        _inproj_body,
        grid=(b, nb),
        in_specs=[pl.BlockSpec((1, N_CTX, d), lambda i, j: (i, 0, 0)),
                  pl.BlockSpec((1, TOK_BLOCK, d), lambda i, j: (i, jnp.maximum(j - N_CTX // TOK_BLOCK, 0), 0)),
                  pl.BlockSpec((1, 1, N_MOD, d), lambda i, j: (i, jnp.minimum(j, 1), 0, 0)),
                  pl.BlockSpec((d, IN_PAD), lambda i, j: (0, 0))],
        out_specs=[tok(d)] + [tok(w) for _, w in _IN_PIECES],
        out_shape=[jax.ShapeDtypeStruct((b, t, d), F32)] + [jax.ShapeDtypeStruct((b, t, w), F32)
                                                            for _, w in _IN_PIECES],
        compiler_params=_cparams(("arbitrary", "arbitrary")),
        name="in_proj",
    )(ctx, x, msel, w_in_p)
    return outs[0], outs[1:]


def _ssd_body(nbatch, nb, fwd_refs, bwd_refs, cw_ref, cb_ref, dtb_ref, alog_ref, dsk_ref, yf_ref, yb_ref, h_scr):
    q = TOK_BLOCK
    j = pl.program_id(0)

    @pl.when(j == 0)
    def _():
        h_scr[...] = jnp.zeros_like(h_scr)

    lane = _iota((1, 128), 1)
    neg_a = jnp.where(lane < 2 * SSD_HEADS, -jnp.exp(alog_ref[...]), 0.0)
    cw = cw_ref[0]
    er = _iota((128, SSD_INNER), 0)
    ec = _iota((128, SSD_INNER), 1)
    er8 = _iota((128, 128), 0)
    ec8 = _iota((128, 128), 1)
    lane_q = _iota((q, 128), 1)
    hpg = SSD_HEADS // 2

    def stream(refs, bi, upper, si, blk):
        xbc_ref, hp_ref, hn_ref, small_ref = refs
        d = 1 if upper else 0
        seg_first = (blk == 0) | (blk == 1)
        seg_last = (blk == 0) | (blk == nb - 1)
        cur = xbc_ref[bi]
        row_id = _iota((q, 1), 0)
        before = jnp.where(seg_first, 0.0, hp_ref[bi][7:8])
        after = jnp.where(seg_last, 0.0, hn_ref[bi][0:1])
        prev = jnp.where(row_id == 0, before, pltpu.roll(cur, 1, axis=0))
        nxt = jnp.where(row_id == q - 1, after, pltpu.roll(cur, q - 1, axis=0))
        act = _silu(prev * cw[0:1] + cur * cw[1:2] + nxt * cw[2:3] + cb_ref[...])
        xs = act[:, :SSD_INNER]
        b_grp = [act[:, 512 + 128 * g:640 + 128 * g] for g in range(2)]
        c_grp = [act[:, 768 + 128 * g:896 + 128 * g].astype(BF16) for g in range(2)]

        tri_b = _tri(q, upper)
        tri = tri_b.astype(BF16)
        tri_t = _tri(q, not upper).astype(BF16)
        dt_all = _softplus(small_ref[bi] + dtb_ref[...])
        la_all = dt_all * neg_a
        e_exp = (er == SSD_HEADS * d + ec // SSD_HEAD_DIM).astype(BF16)
        e8 = ((er8 == SSD_HEADS * d + ec8) & (ec8 < SSD_HEADS)).astype(BF16)
        e_head = (er == ec // SSD_HEAD_DIM).astype(BF16)
        dt_exp = _dot_x_exact(dt_all, e_exp)
        la8 = _dot_x_exact(la_all, e8)
        cs8 = _dot_exact_x(tri, la8)
        cs_t = _dot_x_exact(la8.T, tri_t)
        cs_exp = _dot_x_exact(cs8, e_head)
        tot_exp = cs_exp[0:1] if upper else cs_exp[q - 1:q]

        xdt = xs * dt_exp
        gmats = [_dot_nt(c_grp[g], b_grp[g].astype(BF16)) for g in range(2)]
        ys = []
        for p in range(SSD_HEADS // 2):
            xdt_p = xdt[:, 128 * p:128 * (p + 1)].astype(BF16)
            halves = []
            for h in (2 * p, 2 * p + 1):
                lmat = jnp.where(tri_b, jnp.exp(cs8[:, h:h + 1] - cs_t[h:h + 1, :]), 0.0)
                halves.append(_dot((gmats[h // hpg] * lmat).astype(BF16), xdt_p))
            ys.append(jnp.where(lane_q < SSD_HEAD_DIM, halves[0], halves[1]))
        y = jnp.concatenate(ys, axis=1)

        xd = (xdt * jnp.exp(tot_exp - cs_exp)).astype(BF16)
        offs = []
        for g in range(2):
            hg = h_scr[si, g]
            offs.append(_dot(c_grp[g], hg.astype(BF16)))
            hn = _dot(b_grp[g].T.astype(BF16), xd[:, 256 * g:256 * (g + 1)])
            h_scr[si, g] = hg * jnp.exp(tot_exp[:, 256 * g:256 * (g + 1)]) + hn
        y = y + jnp.concatenate(offs, axis=1) * jnp.exp(cs_exp)
        if not upper:
            y = y + dsk_ref[...] * xs
        return y

    blk_b = jnp.where(j == 0, 0, nb - j)
    for bi in range(nbatch):
        yf_ref[bi] = stream(fwd_refs, bi, False, bi, j)
        yb_ref[bi] = stream(bwd_refs, bi, True, nbatch + bi, blk_b)


def _ssd(xbc, small, conv_w, conv_b, dtb_pad, alog_pad, dskip_exp):
    b, t, _ = xbc.shape
    nb = t // TOK_BLOCK
    r8 = TOK_BLOCK // 8

    def bwd_blk(j):
        return jnp.where(j == 0, 0, nb - j)

    def specs(blk):
        return [pl.BlockSpec((b, TOK_BLOCK, SSD_XBC), lambda j: (0, blk(j), 0)),
                pl.BlockSpec((b, 8, SSD_XBC), lambda j: (0, jnp.maximum(blk(j) * r8 - 1, 0), 0)),
                pl.BlockSpec((b, 8, SSD_XBC), lambda j: (0, jnp.minimum(blk(j) * r8 + r8, t // 8 - 1), 0)),
                pl.BlockSpec((b, TOK_BLOCK, 128), lambda j: (0, blk(j), 0))]

    def body(*refs):
        _ssd_body(b, nb, refs[0:4], refs[4:8], *refs[8:])

    const2 = lambda j: (0, 0)
    oshape = jax.ShapeDtypeStruct((b, t, SSD_INNER), F32)
    return pl.pallas_call(
        body,
        grid=(nb,),
        in_specs=specs(lambda j: j) + specs(bwd_blk) + [
            pl.BlockSpec((1, 3, SSD_XBC), lambda j: (0, 0, 0)),
            pl.BlockSpec((1, SSD_XBC), const2),
            pl.BlockSpec((1, 128), const2),
            pl.BlockSpec((1, 128), const2),
            pl.BlockSpec((1, SSD_INNER), const2)],
        out_specs=[pl.BlockSpec((b, TOK_BLOCK, SSD_INNER), lambda j: (0, j, 0)),
                   pl.BlockSpec((b, TOK_BLOCK, SSD_INNER), lambda j: (0, bwd_blk(j), 0))],
        out_shape=[oshape, oshape],
        scratch_shapes=[pltpu.VMEM((2 * b, 2, SSD_STATE, 256), F32)],
        compiler_params=_cparams(("arbitrary",)),
        name="ssd",
    )(xbc, xbc, xbc, small, xbc, xbc, xbc, small, conv_w[None], conv_b, dtb_pad, alog_pad, dskip_exp)


def _gla_body(nbatch, fwd_refs, bwd_refs, gu_ref, gb_ref, of_ref, ob_ref, s_scr):
    j = pl.program_id(0)
    c = GLA_SUB
    nsub = TOK_BLOCK // c

    @pl.when(j == 0)
    def _():
        s_scr[...] = jnp.zeros_like(s_scr)

    lr = _iota((128, 128), 0)
    lc = _iota((128, 128), 1)
    lane_k = _iota((c, GLA_K_WIDTH), 1) // GLA_KEY_DIM
    lane_kx = _iota((GLA_BLK * (c // GLA_BLK) * (c // GLA_BLK - 1) // 2, GLA_K_WIDTH), 1) // GLA_KEY_DIM
    lane_v = _iota((c, GLA_V_WIDTH), 1) // GLA_VAL_DIM
    st_mask = (_iota((GLA_V_WIDTH, GLA_K_WIDTH), 0) // GLA_VAL_DIM
               == _iota((GLA_V_WIDTH, GLA_K_WIDTH), 1) // GLA_KEY_DIM)

    nblk = c // GLA_BLK
    tok_r = _iota((c, c), 0)
    tok_c = _iota((c, c), 1)
    col_blk = _iota((GLA_BLK, c), 1) // GLA_BLK

    def direction_consts(upper):
        first = (tok_r // GLA_BLK) * GLA_BLK + (GLA_BLK - 1 if upper else 0)
        last = (tok_r // GLA_BLK) * GLA_BLK + (0 if upper else GLA_BLK - 1)
        pairs = [(ti, tj) for ti in range(nblk) for tj in range(nblk) if (tj > ti if upper else tj < ti)]
        return dict(tri_b=_tri(c, upper), sel_first=(tok_c == first).astype(BF16),
                    sel_last=(tok_c == last).astype(BF16), pairs=pairs)

    consts = {False: direction_consts(False), True: direction_consts(True)}

    def stream(refs, bi, upper, st):
        q_ref, k_ref, v_ref, small_ref = refs
        d = 1 if upper else 0
        cst = consts[upper]
        tri_b = cst["tri_b"]
        tri = tri_b.astype(BF16)
        gsel = ((lr == 2 * SSD_HEADS + GLA_GATE_LORA * d + lc) & (lc < GLA_GATE_LORA)).astype(BF16)
        gd = _dot_x_exact(small_ref[bi], gsel)
        yield
        pre = _dot_hp(gd, gu_ref[d]) + gb_ref[d:d + 1]
        la_all = -_softplus(-pre) * (1.0 / GLA_GATE_NORMALIZER)
        yield
        outs = [None] * nsub
        for si in range(nsub):
            sub = nsub - 1 - si if upper else si
            lo = sub * c
            la = la_all[lo:lo + c]
            qq = q_ref[bi, lo:lo + c, :] * (GLA_KEY_DIM ** -0.5)
            kk = k_ref[bi, lo:lo + c, :]
            vv = v_ref[bi, lo:lo + c, :]
            cs = _dot_exact_x(tri, la)
            yield
            rs = _dot_exact_x(cst["sel_first"], cs - la)
            re = _dot_exact_x(cst["sel_last"], cs)
            yield
            tot = jnp.sum(la, axis=0, keepdims=True)
            qt = qq * jnp.exp(cs)
            kdec = (kk * jnp.exp(tot - cs)).astype(BF16)
            q_in = qq * jnp.exp(cs - rs)
            k_in = (kk * jnp.exp(jnp.minimum(rs - cs, 80.0))).astype(BF16)
            k_out = (kk * jnp.exp(re - cs)).astype(BF16)
            q_cross = jnp.concatenate(
                [q_in[GLA_BLK * ti:GLA_BLK * (ti + 1)]
                 * jnp.exp(rs[GLA_BLK * ti:GLA_BLK * ti + 1] - re[GLA_BLK * tj:GLA_BLK * tj + 1])
                 for ti, tj in cst["pairs"]], axis=0)
            vb = vv.astype(BF16)
            o = _dot_nt(qt.astype(BF16), st.astype(BF16))
            amats = []
            for h in range(GLA_K_WIDTH // GLA_KEY_DIM):
                r_in = _dot_nt(jnp.where(lane_k == h, q_in, 0.0).astype(BF16), k_in)
                r_cross = _dot_nt(jnp.where(lane_kx == h, q_cross, 0.0).astype(BF16), k_out)
                rows = []
                for ti in range(nblk):
                    blk_rows = slice(GLA_BLK * ti, GLA_BLK * (ti + 1))
                    part = jnp.where(tri_b[blk_rows] & (col_blk == ti), r_in[blk_rows], 0.0)
                    for pi, (pti, ptj) in enumerate(cst["pairs"]):
                        if pti == ti:
                            part = part + jnp.where(col_blk == ptj, r_cross[GLA_BLK * pi:GLA_BLK * (pi + 1)], 0.0)
                    rows.append(part)
                amats.append(jnp.concatenate(rows, axis=0).astype(BF16))
            upd = _dot(vv.T.astype(BF16), kdec)
            yield
            for h, a in enumerate(amats):
                o = o + jnp.where(lane_v == h, _dot(a, vb), 0.0)
            st = st * jnp.exp(tot) + jnp.where(st_mask, upd, 0.0)
            outs[sub] = o
            yield
        return jnp.concatenate(outs, axis=0), st

    gens = ([stream(fwd_refs, bi, False, s_scr[bi]) for bi in range(nbatch)]
            + [stream(bwd_refs, bi, True, s_scr[nbatch + bi]) for bi in range(nbatch)])
    for si, (o, st) in enumerate(_run_interleaved(gens)):
        if si < nbatch:
            of_ref[si] = o
        else:
            ob_ref[si - nbatch] = o
        s_scr[si] = st


def _gla(gq, gk, gv, small, gu_pad, gb):
    b, t, _ = gq.shape
    nb = t // TOK_BLOCK

    def bwd_blk(j):
        return jnp.where(j == 0, 0, nb - j)

    def specs(blk):
        return [pl.BlockSpec((b, TOK_BLOCK, GLA_K_WIDTH), lambda j: (0, blk(j), 0)),
                pl.BlockSpec((b, TOK_BLOCK, GLA_K_WIDTH), lambda j: (0, blk(j), 0)),
                pl.BlockSpec((b, TOK_BLOCK, GLA_V_WIDTH), lambda j: (0, blk(j), 0)),
                pl.BlockSpec((b, TOK_BLOCK, 128), lambda j: (0, blk(j), 0))]

    def body(*refs):
        _gla_body(b, refs[0:4], refs[4:8], *refs[8:])

    oshape = jax.ShapeDtypeStruct((b, t, GLA_V_WIDTH), F32)
    return pl.pallas_call(
        body,
        grid=(nb,),
        in_specs=specs(lambda j: j) + specs(bwd_blk) + [
            pl.BlockSpec((2, 128, GLA_K_WIDTH), lambda j: (0, 0, 0)),
            pl.BlockSpec((2, GLA_K_WIDTH), lambda j: (0, 0))],
        out_specs=[pl.BlockSpec((b, TOK_BLOCK, GLA_V_WIDTH), lambda j: (0, j, 0)),
                   pl.BlockSpec((b, TOK_BLOCK, GLA_V_WIDTH), lambda j: (0, bwd_blk(j), 0))],
        out_shape=[oshape, oshape],
        scratch_shapes=[pltpu.VMEM((2 * b, GLA_V_WIDTH, GLA_K_WIDTH), F32)],
        compiler_params=_cparams(("arbitrary",)),
        name="gla",
    )(gq, gk, gv, small, gq, gk, gv, small, gu_pad, gb)


_RWKV_OUTS = ("r", "v", "kk", "lw_f", "k_f", "kka_f", "lw_b", "k_b", "kka_b", "bonus", "gate")


def _rwkv_prep_body(nb, rw_ref, hp_ref, hn_ref, mu_ref, w0_ref, w2_ref, a0_ref, a2_ref, g2_ref, kkp_ref, ka_ref,
                    rk_ref, *refs):
    outs = dict(zip(_RWKV_OUTS, refs))
    q = TOK_BLOCK
    j = pl.program_id(1)
    is_ctx = j == 0
    z = rw_ref[0]
    t = _iota((q, 1), 0)
    col = t % GRID_W
    prev = jnp.where(jnp.where(is_ctx, t, col) == 0, 0.0, pltpu.roll(z, 1, axis=0))
    nxt = jnp.where(jnp.where(is_ctx, t - (q - 1), col - (GRID_W - 1)) == 0, 0.0, pltpu.roll(z, q - 1, axis=0))
    up = jnp.concatenate([jnp.where(j == 1, 0.0, hp_ref[0]), z[:q - GRID_W]], axis=0)
    down = jnp.concatenate([z[GRID_W:], jnp.where(j == nb - 1, 0.0, hn_ref[0])], axis=0)
    mu = mu_ref[...]
    vert = mu[2:3] * (up - z) + mu[3:4] * (down - z)
    mixed = z + mu[0:1] * (prev - z) + mu[1:2] * (nxt - z) + jnp.where(is_ctx, 0.0, vert)

    w = RWKV_WIDTH
    r = mixed[:, 0:w]
    k = mixed[:, w:2 * w]
    v = mixed[:, 2 * w:3 * w]
    wd = mixed[:, 3 * w:3 * w + 128]
    ad = mixed[:, 3 * w + 128:4 * w]
    gdr = mixed[:, 4 * w:4 * w + 128]

    lw = w0_ref[...] + _dot(jnp.tanh(wd).astype(BF16), w2_ref[...])
    log_dec = -jnp.exp(-_softplus(-lw) - 0.5)
    a = _sigmoid(a0_ref[...] + _dot(ad.astype(BF16), a2_ref[...]))
    ones64 = _group_ones(w, RWKV_HEAD_DIM)
    kkv = k * kkp_ref[...]
    nrm = jnp.maximum(jnp.sqrt(_dot_x_exact(kkv * kkv, ones64)), 1e-12)
    kkn = kkv / nrm
    ka = ka_ref[...]
    a_f, a_b = a[:, :w], a[:, w:]
    k_f = k * (1.0 + (a_f - 1.0) * ka)
    k_b = k * (1.0 + (a_b - 1.0) * ka)
    outs["r"][0] = r
    outs["v"][0] = v
    outs["kk"][0] = kkn
    outs["lw_f"][0] = log_dec[:, :w]
    outs["lw_b"][0] = log_dec[:, w:]
    outs["k_f"][0] = k_f
    outs["k_b"][0] = k_b
    outs["kka_f"][0] = kkn * a_f
    outs["kka_b"][0] = kkn * a_b
    outs["bonus"][0] = _dot_x_exact(r * (k_f + k_b) * rk_ref[...], ones64) * v
    outs["gate"][0] = _dot(_sigmoid(gdr).astype(BF16), g2_ref[...])


def _rwkv_prep(rw, mu, w0cat, w2bd, a0cat, a2bd, g2, kkp, ka, rk):
    b, t, _ = rw.shape
    nb = t // TOK_BLOCK
    hb = TOK_BLOCK // GRID_W
    w = RWKV_WIDTH
    c2 = lambda i, j: (0, 0)
    return pl.pallas_call(
        functools.partial(_rwkv_prep_body, nb),
        grid=(b, nb),
        in_specs=[pl.BlockSpec((1, TOK_BLOCK, RWKV_COLS), lambda i, j: (i, j, 0)),
                  pl.BlockSpec((1, GRID_W, RWKV_COLS), lambda i, j: (i, jnp.maximum(j * hb - 1, 0), 0)),
                  pl.BlockSpec((1, GRID_W, RWKV_COLS),
                               lambda i, j: (i, jnp.minimum(j * hb + hb, t // GRID_W - 1), 0)),
                  pl.BlockSpec((4, RWKV_COLS), c2),
                  pl.BlockSpec((1, 2 * w), c2),
                  pl.BlockSpec((128, 2 * w), c2),
                  pl.BlockSpec((1, 2 * w), c2),
                  pl.BlockSpec((128, 2 * w), c2),
                  pl.BlockSpec((128, w), c2),
                  pl.BlockSpec((1, w), c2),
                  pl.BlockSpec((1, w), c2),
                  pl.BlockSpec((1, w), c2)],
        out_specs=[pl.BlockSpec((1, TOK_BLOCK, w), lambda i, j: (i, j, 0)) for _ in _RWKV_OUTS],
        out_shape=[jax.ShapeDtypeStruct((b, t, w), F32) for _ in _RWKV_OUTS],
        compiler_params=_cparams(("arbitrary", "arbitrary")),
        name="rwkv_prep",
    )(rw, rw, rw, mu, w0cat, w2bd, a0cat, a2bd, g2, kkp, ka, rk)


def _rwkv_scan_body(nbatch, fwd_refs, bwd_refs, yf_ref, yb_ref, s_scr):
    j = pl.program_id(0)
    n = SCAN_BLOCK
    hd = RWKV_HEAD_DIM

    @pl.when(j == 0)
    def _():
        s_scr[...] = jnp.zeros_like(s_scr)

    nh = RWKV_WIDTH // hd
    w = RWKV_WIDTH
    assert n == hd
    lane_head = _iota((1, w), 1) // hd
    cat_c = _iota((n, nh * n), 0)
    cat_cc = _iota((n, nh * n), 1) % n
    eye_cat = (cat_c == cat_cc).astype(F32)
    bd_mask = (_iota((w, w), 0) // hd) == (_iota((w, w), 1) // hd)

    def stack(x):
        return jnp.concatenate([jnp.where(lane_head == h, x, 0.0) for h in range(nh)], axis=0)

    def chunk(refs, bi, upper, s_prev):
        r, v, kk, lw, k, kka = (ref[bi] for ref in refs)
        tri = _tri(n, upper).astype(BF16)
        strict = (cat_cc > cat_c) if upper else (cat_cc < cat_c)
        incl = (cat_cc >= cat_c) if upper else (cat_cc <= cat_c)
        cs = _dot_exact_x(tri, lw)
        yield
        tot = jnp.sum(lw, axis=0, keepdims=True)
        g_inv = jnp.exp(-cs)
        dec_end = jnp.exp(tot - cs)
        a_t = (-kk * jnp.exp(cs - lw)).astype(BF16)
        r_t = (r * jnp.exp(cs)).astype(BF16)
        bk_s = jnp.concatenate([stack(kka * g_inv), stack(k * g_inv)], axis=0).astype(BF16)
        bk_dec = jnp.concatenate([kka * dec_end, k * dec_end], axis=0).astype(BF16)
        v_s = stack(v).astype(BF16)
        x = _dot_nt(jnp.concatenate([a_t, r_t], axis=0), bk_s)
        yield
        nmat = jnp.where(strict, x[:n, :nh * n], 0.0)
        ak = jnp.where(strict, x[:n, nh * n:], 0.0).astype(BF16)
        rb = jnp.where(incl, x[n:, :nh * n], 0.0).astype(BF16)
        rk = jnp.where(incl, x[n:, nh * n:], 0.0).astype(BF16)
        s_b = s_prev.astype(BF16)
        w0 = _dot_nt(a_t, s_b) + _dot(ak, v_s)
        y0 = _dot_nt(r_t, s_b) + _dot(rk, v_s)
        tinv = eye_cat + nmat
        p = nmat.astype(BF16)
        p_s = stack(nmat).astype(BF16)
        for i in range(n.bit_length() - 2):
            p2 = _dot(p, p_s)
            yield
            p_s = stack(p2).astype(BF16)
            tinv = tinv + _dot(tinv.astype(BF16), p_s)
            p = p2.astype(BF16)
            yield
        u = _dot(tinv.astype(BF16), stack(w0).astype(BF16))
        yield
        y = y0 + _dot(rb, stack(u).astype(BF16))
        uv = jnp.concatenate([u, v], axis=0).astype(BF16)
        upd = lax.dot_general(uv, bk_dec, (((0,), (0,)), ((), ())), preferred_element_type=F32)
        s_new = s_prev * jnp.exp(tot) + jnp.where(bd_mask, upd, 0.0)
        return y, s_new

    gens = ([chunk(fwd_refs, bi, False, s_scr[bi]) for bi in range(nbatch)]
            + [chunk(bwd_refs, bi, True, s_scr[nbatch + bi]) for bi in range(nbatch)])
    for si, (y, s_new) in enumerate(_run_interleaved(gens)):
        if si < nbatch:
            yf_ref[si] = y
        else:
            yb_ref[si - nbatch] = y
        s_scr[si] = s_new


def _rwkv_scan(pre):
    b, t, w = pre["r"].shape
    nbs = t // SCAN_BLOCK
    ctx_blocks = N_CTX // SCAN_BLOCK

    def fwd_blk(j):
        return j

    def bwd_blk(j):
        return jnp.where(j < ctx_blocks, ctx_blocks - 1 - j, nbs + ctx_blocks - 1 - j)

    in_f = pl.BlockSpec((b, SCAN_BLOCK, w), lambda j: (0, fwd_blk(j), 0))
    in_b = pl.BlockSpec((b, SCAN_BLOCK, w), lambda j: (0, bwd_blk(j), 0))
    oshape = jax.ShapeDtypeStruct((b, t, w), F32)

    def body(*refs):
        _rwkv_scan_body(b, refs[0:6], refs[6:12], refs[12], refs[13], refs[14])

    return pl.pallas_call(
        body,
        grid=(nbs,),
        in_specs=[in_f] * 6 + [in_b] * 6,
        out_specs=[in_f, in_b],
        out_shape=[oshape, oshape],
        scratch_shapes=[pltpu.VMEM((2 * b, w, w), F32)],
        compiler_params=_cparams(("arbitrary",)),
        name="rwkv_scan",
    )(pre["r"], pre["v"], pre["kk"], pre["lw_f"], pre["k_f"], pre["kka_f"],
      pre["r"], pre["v"], pre["kk"], pre["lw_b"], pre["k_b"], pre["kka_b"])


def _outproj_body(x_ref, m_ref, s0_ref, s1_ref, z_ref, rf_ref, rb_ref, bonus_ref, gate_ref, g0_ref, g1_ref, og_ref,
                  snw_ref, lxw_ref, lxb_ref, gnw_ref, wo_ref, l1w_ref, l1b_ref, rtw_ref, rtb_ref,
                  x1_ref, h_ref, g_ref):
    m = m_ref[0, 0]
    ys = (s0_ref[0] + s1_ref[0]) * _silu(z_ref[0])
    gw = SSD_INNER // 2
    parts = []
    for g in range(2):
        part = ys[:, gw * g:gw * (g + 1)]
        parts.append(part * lax.rsqrt(jnp.mean(part * part, axis=-1, keepdims=True) + 1e-5))
    y_ssd = jnp.concatenate(parts, axis=1) * snw_ref[...]
    ones64 = _group_ones(RWKV_WIDTH, RWKV_HEAD_DIM)
    inv = 1.0 / RWKV_HEAD_DIM
    yr = rf_ref[0] + rb_ref[0]
    mu = _dot_x_exact(yr, ones64) * inv
    yc = yr - mu
    var = _dot_x_exact(yc * yc, ones64) * inv
    yn = yc * lax.rsqrt(var + RWKV_LNX_EPS) * lxw_ref[...] + lxb_ref[...]
    y_rwkv = (yn + bonus_ref[0]) * gate_ref[0]
    o = g0_ref[0] + g1_ref[0]
    ms = _dot_x_exact(o * o, ones64) * (1.0 / GLA_VAL_DIM)
    y_gla = o * lax.rsqrt(ms + 1e-5) * gnw_ref[...] * _silu(og_ref[0])

    y_mix = jnp.concatenate([y_ssd, y_rwkv, y_gla], axis=1).astype(BF16)
    mix = _dot(y_mix, wo_ref[...])
    x1 = _layernorm(DN_ALPHA * x_ref[0] + m[2:3] * mix) * l1w_ref[...] + l1b_ref[...]
    x1_ref[0] = x1
    h = _layernorm(x1) * (1.0 + m[4:5]) + m[3:4]
    h_ref[0] = h.astype(BF16)
    g_ref[0] = _route(h, rtw_ref[...], rtb_ref[...])


def _outproj(xs, msel, ssd_f, ssd_b, z, rf, rb, bonus, gate, gla_f, gla_b, og, snw, lxw, lxb, gnw, wo, l1w, l1b, rtw, rtb):
    b, t, d = xs.shape
    nb = t // TOK_BLOCK
    tok = lambda w: pl.BlockSpec((1, TOK_BLOCK, w), lambda i, j: (i, j, 0))
    c2 = lambda i, j: (0, 0)
    row = lambda w: pl.BlockSpec((1, w), c2)
    return pl.pallas_call(
        _outproj_body,
        grid=(b, nb),
        in_specs=[tok(d),
                  pl.BlockSpec((1, 1, N_MOD, d), lambda i, j: (i, jnp.minimum(j, 1), 0, 0)),
                  tok(SSD_INNER), tok(SSD_INNER), tok(SSD_INNER),
                  tok(RWKV_WIDTH), tok(RWKV_WIDTH), tok(RWKV_WIDTH), tok(RWKV_WIDTH),
                  tok(GLA_V_WIDTH), tok(GLA_V_WIDTH), tok(GLA_V_WIDTH),
                  row(SSD_INNER), row(RWKV_WIDTH), row(RWKV_WIDTH), row(GLA_V_WIDTH),
                  pl.BlockSpec((d, d), c2), row(d), row(d),
                  pl.BlockSpec((N_EXPERTS, d), c2), pl.BlockSpec((N_EXPERTS, 128), c2)],
        out_specs=[tok(d), tok(d), tok(128)],
        out_shape=[jax.ShapeDtypeStruct((b, t, d), F32), jax.ShapeDtypeStruct((b, t, d), BF16),
                   jax.ShapeDtypeStruct((b, t, 128), F32)],
        compiler_params=_cparams(("arbitrary", "arbitrary")),
        name="out_proj_router",
    )(xs, msel, ssd_f, ssd_b, z, rf, rb, bonus, gate, gla_f, gla_b, og, snw, lxw, lxb, gnw, wo, l1w, l1b, rtw, rtb)


def _route(h, rw_t, rb_t):
    tm = h.shape[0]
    ne = N_EXPERTS
    per_group = ne // N_GROUPS
    ah, am, _ = _split3(rw_t)
    bh, bm, _ = _split3(h)
    logits = _dot_nt(ah, bh) + (_dot_nt(ah, bm) + _dot_nt(am, bh))
    scores = _sigmoid(logits)
    sel = scores + jnp.concatenate([rb_t] * (tm // 128), axis=1)
    neg = -jnp.inf
    sub8 = _iota((per_group, tm), 0).astype(F32)
    eidx = _iota((ne, tm), 0).astype(F32)

    def first_max(cur, idx, sentinel):
        mx = jnp.max(cur, axis=0, keepdims=True)
        first = jnp.min(jnp.where(cur == mx, idx, sentinel), axis=0, keepdims=True)
        return mx, idx == first

    gs = []
    for g in range(N_GROUPS):
        blk = sel[per_group * g:per_group * (g + 1)]
        m1, hit = first_max(blk, sub8, float(per_group))
        m2 = jnp.max(jnp.where(hit, neg, blk), axis=0, keepdims=True)
        gs.append(m1 + m2)
    cur = jnp.concatenate(gs, axis=0)
    allowed_g = jnp.zeros((N_GROUPS, tm), F32)
    for _ in range(TOPK_GROUPS):
        _, hit = first_max(cur, sub8, float(N_GROUPS))
        cur = jnp.where(hit, neg, cur)
        allowed_g = jnp.where(hit, 1.0, allowed_g)
    allowed = jnp.concatenate([jnp.broadcast_to(allowed_g[g:g + 1], (per_group, tm)) for g in range(N_GROUPS)],
                              axis=0)
    cur = jnp.where(allowed > 0.0, sel, neg)
    wts = jnp.zeros((ne, tm), F32)
    for _ in range(TOP_K):
        _, hit = first_max(cur, eidx, float(ne))
        cur = jnp.where(hit, neg, cur)
        wts = jnp.where(hit, scores, wts)
    gates_t = wts / jnp.sum(wts, axis=0, keepdims=True) * ROUTED_SCALE
    return jnp.concatenate([gates_t, jnp.zeros((128 - ne, tm), F32)], axis=0).T


def _moe_body(h_ref, g_ref, w1_ref, w3_ref, w2_ref, o_ref):
    e0 = pl.program_id(1) * EXPERTS_PER_STEP

    @pl.when(pl.program_id(1) == 0)
    def _():
        o_ref[...] = jnp.zeros_like(o_ref)

    hb = h_ref[...]
    gates = g_ref[...]
    lane = _iota(gates.shape, 1)
    hids = []
    for e in range(EXPERTS_PER_STEP):
        gcol = jnp.sum(jnp.where(lane == e0 + e, gates, 0.0), axis=-1, keepdims=True)
        hid = _silu(_dot(hb, w1_ref[e])) * _dot(hb, w3_ref[e])
        hids.append((hid * gcol).astype(BF16))
    o_ref[...] += _dot(jnp.concatenate(hids, axis=1), w2_ref[0])


def _moe(h2, gates, w1, w3, w2g, layer):
    n, d = h2.shape
    tm = n // 8 if n % (8 * 128) == 0 else 1024
    _, ne, _, f = w1.shape
    eps = EXPERTS_PER_STEP
    return pl.pallas_call(
        _moe_body,
        grid=(n // tm, ne // eps),
        in_specs=[pl.BlockSpec((tm, d), lambda i, g: (i, 0), pipeline_mode=pl.Buffered(1)),
                  pl.BlockSpec((tm, 128), lambda i, g: (i, 0), pipeline_mode=pl.Buffered(1)),
                  pl.BlockSpec((None, eps, d, f), lambda i, g: (layer, g, 0, 0)),
                  pl.BlockSpec((None, eps, d, f), lambda i, g: (layer, g, 0, 0)),
                  pl.BlockSpec((None, 1, eps * f, d), lambda i, g: (layer, g, 0, 0))],
        out_specs=pl.BlockSpec((tm, d), lambda i, g: (i, 0), pipeline_mode=pl.Buffered(1)),
        out_shape=jax.ShapeDtypeStruct((n, d), F32),
        compiler_params=_cparams(("arbitrary", "arbitrary")),
        name="moe_experts",
    )(h2, gates, w1, w3, w2g)


def _ffn_residual(x1_ref, h_ref, rt_ref, m, s13_ref, s2_ref, l2w_ref, l2b_ref):
    a = _dot(h_ref[0], s13_ref[...])
    fs = a.shape[1] // 2
    hid = (_silu(a[:, :fs]) * a[:, fs:]).astype(BF16)
    f = rt_ref[0] + _dot(hid, s2_ref[...])
    return _layernorm(DN_ALPHA * x1_ref[0] + m[5:6] * f) * l2w_ref[...] + l2b_ref[...]


def _final_body(x1_ref, h_ref, rt_ref, m_ref, s13_ref, s2_ref, l2w_ref, l2b_ref, o_ref):
    o_ref[0] = _ffn_residual(x1_ref, h_ref, rt_ref, m_ref[0, 0], s13_ref, s2_ref, l2w_ref, l2b_ref)


def _final_next_body(x1_ref, h_ref, rt_ref, m_ref, s13_ref, s2_ref, l2w_ref, l2b_ref, mn_ref, w_ref, o_ref,
                     *proj_refs):
    x2 = _ffn_residual(x1_ref, h_ref, rt_ref, m_ref[0, 0], s13_ref, s2_ref, l2w_ref, l2b_ref)
    o_ref[0] = x2
    _inproj_store(x2, mn_ref[0, 0], w_ref, proj_refs)


def _final(x1, h, routed, msel, s13, s2, l2w, l2b, nxt):
    b, t, d = x1.shape
    nb = t // TOK_BLOCK
    skip = N_CTX // TOK_BLOCK if nxt is None else 0
    tok = pl.BlockSpec((1, TOK_BLOCK, d), lambda i, j: (i, j + skip, 0))
    mspec = pl.BlockSpec((1, 1, N_MOD, d), lambda i, j: (i, jnp.minimum(j + skip, 1), 0, 0))
    c2 = lambda i, j: (0, 0)
    out_tok = lambda w: pl.BlockSpec((1, TOK_BLOCK, w), lambda i, j: (i, j, 0))
    in_specs = [tok, tok, out_tok(d) if nxt is None else tok, mspec, pl.BlockSpec(s13.shape, c2),
                pl.BlockSpec(s2.shape, c2), pl.BlockSpec((1, d), c2), pl.BlockSpec((1, d), c2)]
    if nxt is None:
        return pl.pallas_call(
            _final_body,
            grid=(b, nb - skip),
            in_specs=in_specs,
            out_specs=out_tok(d),
            out_shape=jax.ShapeDtypeStruct((b, t - skip * TOK_BLOCK, d), F32),
            compiler_params=_cparams(("arbitrary", "arbitrary")),
            name="shared_ffn_ln",
        )(x1, h, routed, msel, s13, s2, l2w, l2b)
    msel_n, w_in_n = nxt
    outs = pl.pallas_call(
        _final_next_body,
        grid=(b, nb),
        in_specs=in_specs + [mspec, pl.BlockSpec((d, IN_PAD), c2)],
        out_specs=[out_tok(d)] + [out_tok(w) for _, w in _IN_PIECES],
        out_shape=[jax.ShapeDtypeStruct((b, t, d), F32)] + [jax.ShapeDtypeStruct((b, t, w), F32)
                                                            for _, w in _IN_PIECES],
        compiler_params=_cparams(("arbitrary", "arbitrary")),
        name="shared_ffn_ln_in_proj",
    )(x1, h, routed, msel, s13, s2, l2w, l2b, msel_n, w_in_n)
    return outs[0], outs[1:]


def _prep_w_in(w):
    z, xbc, dt, rw, gq, gk, gv, gd, og = jnp.split(
        w, np.cumsum([512, 1024, 16, 1152, 128, 128, 256, 32, 256])[:-1].tolist(), axis=1)
    small = jnp.concatenate([dt, gd, jnp.zeros((w.shape[0], 128 - 48), w.dtype)], axis=1)
    return jnp.concatenate([z, xbc, rw, gq, gk, gv, og, small], axis=1).astype(BF16)


def _block_diag2(m):
    r, c = m.shape[1:]
    zero = jnp.zeros((r, c), m.dtype)
    return jnp.concatenate([jnp.concatenate([m[0], zero], axis=1),
                            jnp.concatenate([zero, m[1]], axis=1)], axis=0)


def _pad_lanes(v, n=128):
    v = v.reshape(1, -1)
    return jnp.pad(v, ((0, 0), (0, n - v.shape[1])))


def _token_mixer(proj, l, p):
    z, xbc, rw, gq, gk, gv, og, small = proj
    ssd_f, ssd_b = _ssd(xbc, small, p["ssd_conv_w"][l], p["ssd_conv_b"][l].reshape(1, -1),
                 _pad_lanes(p["ssd_dt_bias"][l]), _pad_lanes(p["ssd_a_log"][l]),
                 jnp.repeat(p["ssd_d"][l], SSD_HEAD_DIM).reshape(1, -1))
    gu_pad = jnp.pad(p["gla_gu"][l], ((0, 0), (0, 128 - GLA_GATE_LORA), (0, 0)))
    gla_f, gla_b = _gla(gq, gk, gv, small, gu_pad, p["gla_gb"][l])
    pre = dict(zip(_RWKV_OUTS, _rwkv_prep(
        rw, p["rwkv_mu"][l], p["rwkv_w0"][l].reshape(1, -1), _block_diag2(p["rwkv_w2"][l]).astype(BF16),
        p["rwkv_a0"][l].reshape(1, -1), _block_diag2(p["rwkv_a2"][l]).astype(BF16),
        p["rwkv_g2"][l].astype(BF16), p["rwkv_kk"][l].reshape(1, -1), p["rwkv_ka"][l].reshape(1, -1),
        p["rwkv_rk"][l].reshape(1, -1))))
    rf, rb = _rwkv_scan(pre)
    return z, og, ssd_f, ssd_b, gla_f, gla_b, rf, rb, pre["bonus"], pre["gate"]


def kernel(x, c, ctx, c_ctx, ada_w, ada_b, w_in, ssd_conv_w, ssd_conv_b, ssd_dt_bias, ssd_a_log, ssd_d, ssd_norm_w, rwkv_mu, rwkv_w0, rwkv_w2, rwkv_a0, rwkv_a2, rwkv_g2, rwkv_kk, rwkv_ka, rwkv_rk, rwkv_lnx_w, rwkv_lnx_b, gla_gu, gla_gb, gla_norm_w, w_out, ln1_w, ln1_b, ln2_w, ln2_b, router_w, router_b, exp_w1, exp_w3, exp_w2, sh_w1, sh_w3, sh_w2):
    p = dict(ssd_conv_w=ssd_conv_w, ssd_conv_b=ssd_conv_b, ssd_dt_bias=ssd_dt_bias, ssd_a_log=ssd_a_log,
             ssd_d=ssd_d, rwkv_mu=rwkv_mu, rwkv_w0=rwkv_w0, rwkv_w2=rwkv_w2, rwkv_a0=rwkv_a0, rwkv_a2=rwkv_a2,
             rwkv_g2=rwkv_g2, rwkv_kk=rwkv_kk, rwkv_ka=rwkv_ka, rwkv_rk=rwkv_rk, gla_gu=gla_gu, gla_gb=gla_gb)
    bsz, seq, d = x.shape
    n_ctx = ctx.shape[1]
    assert n_ctx == N_CTX == TOK_BLOCK and d == D_MODEL and seq % TOK_BLOCK == 0 and seq // GRID_W >= 8
    depth = ada_w.shape[0]
    row = lambda a: a.reshape(1, -1)

    cond = jnp.concatenate([c, c_ctx[None], jnp.zeros((8 - bsz - 1, d), F32)], axis=0)
    mods = _ada_all(cond, ada_w, ada_b).reshape(depth, 8, N_MOD, d)
    t = n_ctx + seq
    msels = [jnp.stack([jnp.broadcast_to(mods[l, bsz], (bsz, N_MOD, d)), mods[l, :bsz]], axis=1)
             for l in range(depth)]
    xs, proj = _inproj(ctx, x, msels[0], _prep_w_in(w_in[0]))
    w1_all = exp_w1.astype(BF16)
    w3_all = exp_w3.astype(BF16)
    w2_all = exp_w2.astype(BF16).reshape(depth, N_EXPERTS // EXPERTS_PER_STEP, -1, d)
    for l in range(depth):
        msel = msels[l]
        z, og, ssd_f, ssd_b, gla_f, gla_b, rf, rb, bonus, gate = _token_mixer(proj, l, p)
        x1, h, gates = _outproj(xs, msel, ssd_f, ssd_b, z, rf, rb, bonus, gate, gla_f, gla_b, og,
                                row(ssd_norm_w[l]), row(rwkv_lnx_w[l]), row(rwkv_lnx_b[l]), row(gla_norm_w[l]),
                                w_out[l].astype(BF16), row(ln1_w[l]), row(ln1_b[l]),
                                router_w[l].T, jnp.broadcast_to(router_b[l][:, None], (N_EXPERTS, 128)))
        last = l == depth - 1
        h_moe, g_moe = (h[:, n_ctx:], gates[:, n_ctx:]) if last else (h, gates)
        routed = _moe(h_moe.reshape(-1, d), g_moe.reshape(-1, 128), w1_all, w3_all, w2_all, l)
        s13 = jnp.concatenate([sh_w1[l], sh_w3[l]], axis=1).astype(BF16)
        args = (x1, h, routed.reshape(bsz, -1, d), msel, s13, sh_w2[l].astype(BF16), row(ln2_w[l]), row(ln2_b[l]))
        if last:
            return _final(*args, None)
        xs, proj = _final(*args, (msels[l + 1], _prep_w_in(w_in[l + 1])))
```
